```python
import math
import jax, jax.numpy as jnp
from jax import lax
import numpy as np

D_MODEL = 4096
BATCH = 8
SEQ = 2048
DEPTH = 2

N_META = 16
BLK = 128
PAD_FRONT = BLK - N_META
A_HEADS = 8
A_HDIM = 128
IDX_HEADS = 16
IDX_HDIM = 64
IDX_ROPE = 32
TOPK_MAX = 256
B_HEADS = 16
B_KV_HEADS = 2
B_HDIM = 64
WINDOW = 128
C_HEADS = 8
C_Q_RANK = 1024
C_KV_RANK = 512
C_NOPE = 128
C_ROPE = 64
C_VDIM = 128
D_HEADS = 8
D_HDIM = 64
GROUP_WIDTH = A_HEADS * A_HDIM
MIX_WIDTH = A_HEADS * A_HDIM + B_HEADS * B_HDIM + C_HEADS * C_VDIM + D_HEADS * 2 * D_HDIM
D_FF = 11008
CONV_W = 3
ROPE_BASE = 10000.0
ALPHA = (2 * DEPTH) ** 0.25
BETA = (8 * DEPTH) ** -0.25
NEG = -1e30
IN_SIZES = (
    A_HEADS * A_HDIM, A_HDIM, A_HDIM,
    IDX_HEADS * IDX_HDIM, IDX_HDIM, IDX_HEADS,
    B_HEADS * B_HDIM, B_KV_HEADS * B_HDIM, B_KV_HEADS * B_HDIM,
    C_Q_RANK, C_KV_RANK, C_ROPE,
    D_HEADS * 2 * D_HDIM, D_HEADS * 2 * D_HDIM, D_HEADS * 2 * D_HDIM,
)
IN_COLS = sum(IN_SIZES)

kernel_name = "hymba_dsa_swa_mla_diff_convffn_deepnorm"


def layer_norm(x, g, b, eps=1e-5):
    xf = x.astype(jnp.float32)
    mu = xf.mean(-1, keepdims=True)
    var = jnp.square(xf - mu).mean(-1, keepdims=True)
    return ((xf - mu) * lax.rsqrt(var + eps) * g.astype(jnp.float32) + b.astype(jnp.float32)).astype(x.dtype)


def rms_norm(x, g, eps=1e-6):
    xf = x.astype(jnp.float32)
    y = xf * lax.rsqrt(jnp.mean(xf * xf, -1, keepdims=True) + eps)
    return (y * g.astype(jnp.float32)).astype(x.dtype)


def rope(x, pos):
    half = x.shape[-1] // 2
    inv = ROPE_BASE ** (-jnp.arange(half, dtype=jnp.float32) / half)
    ang = pos.astype(jnp.float32)[:, None] * inv[None]
    cos, sin = jnp.cos(ang)[:, None, :], jnp.sin(ang)[:, None, :]
    x1 = x[..., :half].astype(jnp.float32)
    x2 = x[..., half:].astype(jnp.float32)
    return jnp.concatenate([x1 * cos - x2 * sin, x2 * cos + x1 * sin], -1).astype(x.dtype)


def rope_part(x, pos, n):
    return jnp.concatenate([rope(x[..., :n], pos), x[..., n:]], -1)


def alibi_slopes(n):
    return 2.0 ** (-8.0 * jnp.arange(1, n + 1, dtype=jnp.float32) / n)


def split_blocks(x):
    b, p = x.shape[:2]
    return jnp.moveaxis(x.reshape(b, p // BLK, BLK, *x.shape[2:]), 1, 0)


def merge_blocks(y):
    y = jnp.moveaxis(y, 0, 1)
    return y.reshape(y.shape[0], -1, *y.shape[3:])


def map_query_blocks(fn, *qs):
    nb = qs[0].shape[1] // BLK
    starts = jnp.arange(nb, dtype=jnp.int32) * BLK
    out = lax.map(lambda a: fn(*a), (starts, *[split_blocks(q) for q in qs]))
    return merge_blocks(out)


def block_visibility(q0, p):
    qi = q0 + jnp.arange(BLK, dtype=jnp.int32)
    ki = jnp.arange(p, dtype=jnp.int32)
    vis = (ki[None] <= qi[:, None]) & (ki[None] >= PAD_FRONT)
    dist = (qi[:, None] - ki[None]).astype(jnp.float32)
    return qi, vis, dist


def dsa_mixer(q, k, v, iq, ik, iw, k_top):
    p_len = k.shape[1]
    slopes = alibi_slopes(A_HEADS)
    iscale = (IDX_HEADS * IDX_HDIM) ** -0.5

    def block(q0, qb, iqb, iwb):
        qi, vis, _ = block_visibility(q0, p_len)
        rel = jax.nn.relu(jnp.einsum('bqhd,bkd->bqhk', iqb, ik).astype(jnp.float32))
        score = jnp.einsum('bqh,bqhk->bqk', iwb.astype(jnp.float32), rel) * iscale
        score = jnp.where(vis[None], score, NEG)
        _, sel = lax.top_k(score, k_top)
        kg = jax.vmap(lambda kk, ii: kk[ii])(k, sel)
        vg = jax.vmap(lambda vv, ii: vv[ii])(v, sel)
        ok = (sel <= qi[None, :, None]) & (sel >= PAD_FRONT)
        dist = (qi[None, :, None] - sel).astype(jnp.float32)
        s = jnp.einsum('bqhd,bqkd->bhqk', qb, kg).astype(jnp.float32) * A_HDIM ** -0.5
        s = s - slopes[None, :, None, None] * dist[:, None]
        s = jnp.where(ok[:, None], s, NEG)
        pr = jax.nn.softmax(s, axis=-1).astype(v.dtype)
        return jnp.einsum('bhqk,bqkd->bqhd', pr, vg)

    return map_query_blocks(block, q, iq, iw)


def swa_sink_mixer(q, k, v, sinks):
    b, p_len = q.shape[:2]
    nb = p_len // BLK
    grp = B_HEADS // B_KV_HEADS
    qb = q.reshape(b, nb, BLK, B_KV_HEADS, grp, B_HDIM)

    def band(x):
        xb = x.reshape(b, nb, BLK, B_KV_HEADS, B_HDIM)
        prev = jnp.pad(xb, ((0, 0), (1, 0), (0, 0), (0, 0), (0, 0)))[:, :-1]
        return jnp.concatenate([prev, xb], axis=2)

    kw, vw = band(k), band(v)
    blk = jnp.arange(nb, dtype=jnp.int32)[:, None] * BLK
    qi = blk + jnp.arange(BLK, dtype=jnp.int32)[None]
    ki = blk - BLK + jnp.arange(2 * BLK, dtype=jnp.int32)[None]
    diff = qi[:, :, None] - ki[:, None, :]
    ok = (diff >= 0) & (diff < WINDOW) & (ki[:, None, :] >= PAD_FRONT)
    slopes = alibi_slopes(B_HEADS).reshape(B_KV_HEADS, grp)
    s = jnp.einsum('bnqhgd,bnkhd->bnhgqk', qb, kw).astype(jnp.float32) * B_HDIM ** -0.5
    s = s - slopes[None, None, :, :, None, None] * diff[:, None, None].astype(jnp.float32)
    s = jnp.where(ok[:, None, None], s, NEG)
    sink = sinks.astype(jnp.float32).reshape(B_KV_HEADS, grp)
    sink_col = jnp.broadcast_to(sink[None, None, :, :, None, None], s.shape[:-1] + (1,))
    pr = jax.nn.softmax(jnp.concatenate([s, sink_col], -1), axis=-1)[..., :-1]
    o = jnp.einsum('bnhgqk,bnkhd->bnqhgd', pr.astype(v.dtype), vw)
    return o.reshape(b, p_len, B_HEADS, B_HDIM)


def mla_mixer(q, k, v):
    p_len = k.shape[1]
    scale = (C_NOPE + C_ROPE) ** -0.5

    def block(q0, qb):
        _, vis, _ = block_visibility(q0, p_len)
        s = jnp.einsum('bqhd,bkhd->bhqk', qb, k).astype(jnp.float32) * scale
        s = jnp.where(vis[None, None], s, NEG)
        pr = jax.nn.softmax(s, axis=-1).astype(v.dtype)
        return jnp.einsum('bhqk,bkhd->bqhd', pr, v)

    return map_query_blocks(block, q)


def diff_mixer(q, k, v, lam, g_sub, lam_init):
    p_len = k.shape[1]
    slopes = alibi_slopes(D_HEADS)

    def block(q0, qb):
        _, vis, dist = block_visibility(q0, p_len)
        s = jnp.einsum('bqhcd,bkhcd->bhcqk', qb, k).astype(jnp.float32) * D_HDIM ** -0.5
        s = s - slopes[None, :, None, None, None] * dist[None, None, None]
        s = jnp.where(vis[None, None, None], s, NEG)
        pr = jax.nn.softmax(s, axis=-1)
        a = pr[:, :, 0] - lam * pr[:, :, 1]
        return jnp.einsum('bhqk,bkhe->bqhe', a.astype(v.dtype), v)

    o = map_query_blocks(block, q)
    return rms_norm(o, g_sub) * (1.0 - lam_init)


def conv_ffn(x, valid, w_up, conv_w, conv_b, w_down):
    h = x @ w_up
    h = h * valid[None, :, None].astype(h.dtype)
    h = lax.conv_general_dilated(
        h, conv_w[:, None, :].astype(h.dtype), window_strides=(1,), padding=[(CONV_W - 1, 0)],
        dimension_numbers=('NWC', 'WIO', 'NWC'), feature_group_count=h.shape[-1]) + conv_b
    gate, up = jnp.split(h, 2, axis=-1)
    return (jax.nn.silu(gate) * up) @ w_down


def setup_inputs(seed: int = 0) -> dict:
    key = jax.random.key(seed)
    ks = jax.random.split(key, 26)
    L = DEPTH

    def nrm(k, shape, scale):
        return jax.random.normal(k, shape, jnp.float32) * scale

    return {
        "x": nrm(ks[0], (BATCH, SEQ, D_MODEL), 1.0),
        "meta_tokens": nrm(ks[1], (N_META, D_MODEL), 1.0),
        "ln_in_g": 1.0 + nrm(ks[2], (D_MODEL,), 0.01),
        "ln_in_b": nrm(ks[3], (D_MODEL,), 0.01),
        "w_in": nrm(ks[4], (L, D_MODEL, IN_COLS), D_MODEL ** -0.5),
        "g_cq": 1.0 + nrm(ks[5], (L, C_Q_RANK), 0.01),
        "g_ckv": 1.0 + nrm(ks[6], (L, C_KV_RANK), 0.01),
        "w_uq": nrm(ks[7], (L, C_Q_RANK, C_HEADS * (C_NOPE + C_ROPE)), C_Q_RANK ** -0.5),
        "w_ukv": nrm(ks[8], (L, C_KV_RANK, C_HEADS * (C_NOPE + C_VDIM)), C_KV_RANK ** -0.5),
        "sinks": nrm(ks[9], (L, B_HEADS), 0.5),
        "lam_q1": nrm(ks[10], (L, D_HDIM), 0.1),
        "lam_k1": nrm(ks[11], (L, D_HDIM), 0.1),
        "lam_q2": nrm(ks[12], (L, D_HDIM), 0.1),
        "lam_k2": nrm(ks[13], (L, D_HDIM), 0.1),
        "g_diff": 1.0 + nrm(ks[14], (L, 2 * D_HDIM), 0.01),
        "g_grp": 1.0 + nrm(ks[15], (L, 3, GROUP_WIDTH), 0.01),
        "w_out": nrm(ks[16], (L, MIX_WIDTH, D_MODEL), BETA * MIX_WIDTH ** -0.5),
        "ln1_g": 1.0 + nrm(ks[17], (L, D_MODEL), 0.01),
        "ln1_b": nrm(ks[18], (L, D_MODEL), 0.01),
        "w_up": nrm(ks[19], (L, D_MODEL, 2 * D_FF), D_MODEL ** -0.5),
        "conv_w": nrm(ks[20], (L, CONV_W, 2 * D_FF), CONV_W ** -0.5),
        "conv_b": nrm(ks[21], (L, 2 * D_FF), 0.01),
        "w_down": nrm(ks[22], (L, D_FF, D_MODEL), BETA * D_FF ** -0.5),
        "ln2_g": 1.0 + nrm(ks[23], (L, D_MODEL), 0.01),
        "ln2_b": nrm(ks[24], (L, D_MODEL), 0.01),
    }


def reference(x, meta_tokens, ln_in_g, ln_in_b, w_in, g_cq, g_ckv, w_uq, w_ukv, sinks,
              lam_q1, lam_k1, lam_q2, lam_k2, g_diff, g_grp, w_out, ln1_g, ln1_b,
              w_up, conv_w, conv_b, w_down, ln2_g, ln2_b):
    b, s_len, _ = x.shape
    k_top = min(TOPK_MAX, s_len // 4)
    pad = jnp.zeros((b, PAD_FRONT, D_MODEL), x.dtype)
    meta = jnp.broadcast_to(meta_tokens[None].astype(x.dtype), (b, N_META, D_MODEL))
    h = jnp.concatenate([pad, meta, x], axis=1)
    p_len = h.shape[1]
    idx = jnp.arange(p_len, dtype=jnp.int32)
    valid = idx >= PAD_FRONT
    pos = idx - PAD_FRONT
    h = layer_norm(h, ln_in_g, ln_in_b)
    splits = np.cumsum(IN_SIZES)[:-1].tolist()

    for l in range(DEPTH):
        proj = h @ w_in[l]
        (a_q, a_k, a_v, i_q, i_k, i_w, b_q, b_k, b_v,
         c_cq, c_ckv, c_kr, d_q, d_k, d_v) = jnp.split(proj, splits, axis=-1)

        iq = rope_part(i_q.reshape(b, p_len, IDX_HEADS, IDX_HDIM), pos, IDX_ROPE)
        ik = rope_part(i_k.reshape(b, p_len, 1, IDX_HDIM), pos, IDX_ROPE)[:, :, 0]
        o_a = dsa_mixer(a_q.reshape(b, p_len, A_HEADS, A_HDIM), a_k, a_v, iq, ik, i_w, k_top)
        o_a = rms_norm(o_a.reshape(b, p_len, -1), g_grp[l, 0])

        o_b = swa_sink_mixer(b_q.reshape(b, p_len, B_HEADS, B_HDIM),
                             b_k.reshape(b, p_len, B_KV_HEADS, B_HDIM),
                             b_v.reshape(b, p_len, B_KV_HEADS, B_HDIM), sinks[l])
        o_b = rms_norm(o_b.reshape(b, p_len, -1), g_grp[l, 1])

        cq = (rms_norm(c_cq, g_cq[l]) @ w_uq[l]).reshape(b, p_len, C_HEADS, C_NOPE + C_ROPE)
        ckv = (rms_norm(c_ckv, g_ckv[l]) @ w_ukv[l]).reshape(b, p_len, C_HEADS, C_NOPE + C_VDIM)
        q_c = jnp.concatenate([cq[..., :C_NOPE], rope(cq[..., C_NOPE:], pos)], -1)
        k_r = jnp.broadcast_to(rope(c_kr.reshape(b, p_len, 1, C_ROPE), pos), (b, p_len, C_HEADS, C_ROPE))
        k_c = jnp.concatenate([ckv[..., :C_NOPE], k_r], -1)
        o_c = mla_mixer(q_c, k_c, ckv[..., C_NOPE:])
        o_c = rms_norm(o_c.reshape(b, p_len, -1), g_grp[l, 2])

        lam_init = 0.8 - 0.6 * math.exp(-0.3 * l)
        lam = (jnp.exp(jnp.sum(lam_q1[l].astype(jnp.float32) * lam_k1[l].astype(jnp.float32)))
               - jnp.exp(jnp.sum(lam_q2[l].astype(jnp.float32) * lam_k2[l].astype(jnp.float32)))
               + lam_init)
        o_d = diff_mixer(d_q.reshape(b, p_len, D_HEADS, 2, D_HDIM),
                         d_k.reshape(b, p_len, D_HEADS, 2, D_HDIM),
                         d_v.reshape(b, p_len, D_HEADS, 2 * D_HDIM), lam, g_diff[l], lam_init)
        o_d = o_d.reshape(b, p_len, -1)

        mix = jnp.concatenate([o_a, o_b, o_c, o_d], axis=-1) @ w_out[l]
        h = layer_norm(ALPHA * h + mix, ln1_g[l], ln1_b[l])

        ffn = conv_ffn(h, valid, w_up[l], conv_w[l], conv_b[l], w_down[l])
        h = layer_norm(ALPHA * h + ffn, ln2_g[l], ln2_b[l])

    return h[:, BLK:]
```

```python
import functools
import math

import jax
import jax.numpy as jnp
import numpy as np
from jax import lax
from jax.experimental import pallas as pl
from jax.experimental.pallas import tpu as pltpu

N_META = 16
BLK = 128
PAD_FRONT = BLK - N_META
A_HEADS, A_HDIM = 8, 128
IDX_HEADS, IDX_HDIM, IDX_ROPE = 16, 64, 32
TOPK_MAX = 256
B_HEADS, B_KV_HEADS, B_HDIM = 16, 2, 64
WINDOW = 128
C_HEADS, C_Q_RANK, C_KV_RANK, C_NOPE, C_ROPE, C_VDIM = 8, 1024, 512, 128, 64, 128
D_HEADS, D_HDIM = 8, 64
GROUP_WIDTH = 1024
CONV_W = 3
ROPE_BASE = 10000.0
NEG = -1e30
IN_SIZES = (1024, 128, 128, 1024, 64, 16, 1024, 128, 128, 1024, 512, 64, 1024, 1024, 1024)

LANES = 128
BF16_SUBLANES = 16
VMEM_LIMIT = 56 * 1024 * 1024
FF_TILE = 512

F32 = jnp.float32
BF16 = jnp.bfloat16


def _cparams(n_axes):
    return pltpu.CompilerParams(dimension_semantics=("arbitrary",) * n_axes, vmem_limit_bytes=VMEM_LIMIT)


def _row_tile(p_len, target):
    best = BF16_SUBLANES
    for t in range(BF16_SUBLANES, target + 1, BF16_SUBLANES):
        if p_len % t == 0:
            best = t
    return best


def _dot(a, b):
    return jnp.dot(a, b, preferred_element_type=F32)


def _dot_nt(a, b):
    return lax.dot_general(a, b, (((1,), (1,)), ((), ())), preferred_element_type=F32)


def _ln_rows(x, g, b):
    mu = jnp.mean(x, axis=-1, keepdims=True)
    xc = x - mu
    var = jnp.mean(xc * xc, axis=-1, keepdims=True)
    return xc * lax.rsqrt(var + 1e-5) * g + b


def _ln_in_kernel(x_ref, g_ref, b_ref, h_ref, hb_ref):
    y = _ln_rows(x_ref[...], g_ref[...], b_ref[...])
    h_ref[...] = y
    hb_ref[...] = y.astype(BF16)


def _ln_in(x, g, b, tm):
    t, d = x.shape
    return pl.pallas_call(
        _ln_in_kernel,
        grid=(t // tm,),
        in_specs=[pl.BlockSpec((tm, d), lambda i: (i, 0)),
                  pl.BlockSpec((1, d), lambda i: (0, 0)),
                  pl.BlockSpec((1, d), lambda i: (0, 0))],
        out_specs=[pl.BlockSpec((tm, d), lambda i: (i, 0)),
                   pl.BlockSpec((tm, d), lambda i: (i, 0))],
        out_shape=[jax.ShapeDtypeStruct((t, d), F32), jax.ShapeDtypeStruct((t, d), BF16)],
        compiler_params=_cparams(1),
        name="ln_in",
    )(x, g.reshape(1, d), b.reshape(1, d))


def _res_ln_kernel(h_ref, y_ref, g_ref, b_ref, o_ref, ob_ref, *, alpha, tm, tiles_per_seq, zero_pad):
    out = _ln_rows(alpha * h_ref[...] + y_ref[...], g_ref[...], b_ref[...])
    o_ref[...] = out
    if zero_pad:
        p0 = (pl.program_id(0) % tiles_per_seq) * tm
        pos = p0 + lax.broadcasted_iota(jnp.int32, out.shape, 0)
        out = jnp.where(pos >= PAD_FRONT, out, 0.0)
    ob_ref[...] = out.astype(BF16)


def _res_ln(h, y, g, b, alpha, tm, p_len, zero_pad):
    t, d = h.shape
    kern = functools.partial(_res_ln_kernel, alpha=alpha, tm=tm, tiles_per_seq=p_len // tm, zero_pad=zero_pad)
    return pl.pallas_call(
        kern,
        grid=(t // tm,),
        in_specs=[pl.BlockSpec((tm, d), lambda i: (i, 0)),
                  pl.BlockSpec((tm, d), lambda i: (i, 0)),
                  pl.BlockSpec((1, d), lambda i: (0, 0)),
                  pl.BlockSpec((1, d), lambda i: (0, 0))],
        out_specs=[pl.BlockSpec((tm, d), lambda i: (i, 0)),
                   pl.BlockSpec((tm, d), lambda i: (i, 0))],
        out_shape=[jax.ShapeDtypeStruct((t, d), F32), jax.ShapeDtypeStruct((t, d), BF16)],
        compiler_params=_cparams(1),
        name="res_ln",
    )(h, y, g.reshape(1, d), b.reshape(1, d))


def _mm_kernel(a_ref, w_ref, o_ref):
    o_ref[...] = _dot(a_ref[...], w_ref[...]).astype(o_ref.dtype)


def _mm(a, w, tm, tn, out_dtype, name):
    t, k = a.shape
    n = w.shape[1]
    return pl.pallas_call(
        _mm_kernel,
        grid=(n // tn, t // tm),
        in_specs=[pl.BlockSpec((tm, k), lambda j, i: (i, 0)),
                  pl.BlockSpec((k, tn), lambda j, i: (0, j))],
        out_specs=pl.BlockSpec((tm, tn), lambda j, i: (i, j)),
        out_shape=jax.ShapeDtypeStruct((t, n), out_dtype),
        compiler_params=_cparams(2),
        name=name,
    )(a, w)


def _rope128(x, c, s1, s2, half):
    return x * c + pltpu.roll(x, LANES - half, 1) * s1 + pltpu.roll(x, half, 1) * s2


def _rope_tables(pos, rot_dim, period):
    half = rot_dim // 2
    inv = ROPE_BASE ** (-jnp.arange(half, dtype=F32) / half)
    ang = pos.astype(F32)[:, None] * inv[None]
    cos, sin = jnp.cos(ang), jnp.sin(ang)
    n = pos.shape[0]
    lane = np.arange(LANES) % period
    li = lane % half
    cos_l, sin_l = cos[:, li], sin[:, li]
    first = jnp.asarray(lane < half)[None]
    second = jnp.asarray((lane >= half) & (lane < rot_dim))[None]
    c = jnp.where(first | second, cos_l, 1.0)
    s1 = jnp.where(first, -sin_l, 0.0)
    s2 = jnp.where(second, sin_l, 0.0)
    return jnp.stack([c, s1, s2]).astype(F32).reshape(3, n, LANES)


def _rms(x, g, eps=1e-6):
    return x * lax.rsqrt(jnp.mean(x * x, axis=-1, keepdims=True) + eps) * g


def _attn_a_kernel(iq_ref, wq_ref, ikw_ref, tq_ref, tk_ref, aq_ref, ak_ref, av_ref, bias_ref, g_ref,
                   o_ref, key_ref, mask_ref, qs_ref, os_ref, *, tq, p_len, k_top):
    q0 = pl.program_id(1) * tq
    half = IDX_ROPE // 2
    ik = _rope128(ikw_ref[...], tk_ref[0], tk_ref[1], tk_ref[2], half)[:, :IDX_HDIM].astype(BF16)
    w = wq_ref[...][:, IDX_HDIM:IDX_HDIM + IDX_HEADS] * ((IDX_HEADS * IDX_HDIM) ** -0.5)
    tc, ts1, ts2 = tq_ref[0], tq_ref[1], tq_ref[2]
    score = jnp.zeros((tq, p_len), F32)
    for c in range(IDX_HEADS * IDX_HDIM // LANES):
        chunk = _rope128(iq_ref[:, c * LANES:(c + 1) * LANES], tc, ts1, ts2, half).astype(BF16)
        for j in range(LANES // IDX_HDIM):
            h = c * (LANES // IDX_HDIM) + j
            rel = jnp.maximum(_dot_nt(chunk[:, j * IDX_HDIM:(j + 1) * IDX_HDIM], ik), 0.0)
            score = score + w[:, h:h + 1] * rel
    qpos = q0 + lax.broadcasted_iota(jnp.int32, (tq, p_len), 0)
    kpos = lax.broadcasted_iota(jnp.int32, (tq, p_len), 1)
    vis = jnp.where(kpos <= qpos, jnp.where(kpos >= PAD_FRONT, 1.0, 0.0), 0.0)
    score = jnp.where(vis > 0.0, score, NEG)
    bits = lax.bitcast_convert_type(score, jnp.int32)
    key_ref[...] = jnp.where(bits < 0, bits ^ jnp.int32(0x7FFFFFFF), bits)
    kf = jnp.float32(k_top)

    def count(pred):
        return jnp.sum(jnp.where(pred, 1.0, 0.0), axis=1, keepdims=True)

    def vbody(i, lo):
        cand = lo + jnp.left_shift(jnp.int32(1), 31 - i)
        return jnp.where(count(key_ref[...] >= cand) >= kf, cand, lo)

    thr = lax.fori_loop(0, 32, vbody, jnp.full((tq, 1), -2 ** 31, jnp.int32))
    key = key_ref[...]
    n_gt = count(key > thr)
    n_eq = count(key == thr)
    need = kf - n_gt
    n_eq_vis = jnp.sum(jnp.where(key == thr, vis, 0.0), axis=1, keepdims=True)
    mask_ref[...] = jnp.where(key >= thr, jnp.where(vis > 0.0, 0.0, NEG), NEG)
    surplus = jnp.where(n_eq_vis > 0.0, jnp.where(n_eq > need, 1.0, 0.0), 0.0)

    @pl.when(jnp.max(surplus) > 0.0)
    def _():
        nbits = max(1, (p_len - 1).bit_length())
        keyv = key_ref[...]
        col = lax.broadcasted_iota(jnp.int32, (tq, p_len), 1)

        def tbody(i, j):
            cand = j + jnp.left_shift(jnp.int32(1), nbits - 1 - i)
            f = jnp.sum(jnp.where(keyv == thr, jnp.where(col < cand, 1.0, 0.0), 0.0), axis=1, keepdims=True)
            return jnp.where(f < need, cand, j)

        jstar = lax.fori_loop(0, nbits, tbody, jnp.zeros((tq, 1), jnp.int32))
        chosen = jnp.where(keyv > thr, 1.0, jnp.where(keyv == thr, jnp.where(col <= jstar, 1.0, 0.0), 0.0))
        mask_ref[...] = jnp.where(chosen > 0.0, jnp.where(vis > 0.0, 0.0, NEG), NEG)

    for h in range(A_HEADS):
        qs_ref[h * tq:(h + 1) * tq, :] = aq_ref[:, h * A_HDIM:(h + 1) * A_HDIM]
    scale = A_HDIM ** -0.5

    def hbody(h, carry):
        r0 = pl.multiple_of(h * tq, tq)
        s = _dot_nt(qs_ref[pl.ds(r0, tq), :], ak_ref[...]) * scale + bias_ref[h] + mask_ref[...]
        m = jnp.max(s, axis=1, keepdims=True)
        e = jnp.exp(s - m)
        l = jnp.sum(e, axis=1, keepdims=True)
        os_ref[pl.ds(r0, tq), :] = _dot(e.astype(BF16), av_ref[...]) / l
        return carry

    lax.fori_loop(0, A_HEADS, hbody, 0)
    o = jnp.concatenate([os_ref[h * tq:(h + 1) * tq, :] for h in range(A_HEADS)], axis=1)
    o_ref[...] = _rms(o, g_ref[...]).astype(BF16)


def _attn_a(pf, pb, tabs_idx, bias, g, batch, p_len, tq, k_top, col):
    nq = p_len // tq
    t = batch * p_len
    kern = functools.partial(_attn_a_kernel, tq=tq, p_len=p_len, k_top=k_top)
    return pl.pallas_call(
        kern,
        grid=(batch, nq),
        in_specs=[
            pl.BlockSpec((tq, 1024), lambda b, q: (b * nq + q, col["i_q"] // 1024)),
            pl.BlockSpec((tq, LANES), lambda b, q: (b * nq + q, col["ikw"] // LANES)),
            pl.BlockSpec((p_len, LANES), lambda b, q: (b, col["ikw"] // LANES)),
            pl.BlockSpec((3, tq, LANES), lambda b, q: (0, q, 0)),
            pl.BlockSpec((3, p_len, LANES), lambda b, q: (0, 0, 0)),
            pl.BlockSpec((tq, 1024), lambda b, q: (b * nq + q, col["a_q"] // 1024)),
            pl.BlockSpec((p_len, LANES), lambda b, q: (b, col["a_k"] // LANES)),
            pl.BlockSpec((p_len, LANES), lambda b, q: (b, col["a_v"] // LANES)),
            pl.BlockSpec((A_HEADS, 1, p_len), lambda b, q: (0, 0, 0)),
            pl.BlockSpec((1, GROUP_WIDTH), lambda b, q: (0, 0)),
        ],
        out_specs=pl.BlockSpec((tq, GROUP_WIDTH), lambda b, q: (b * nq + q, 0)),
        out_shape=jax.ShapeDtypeStruct((t, GROUP_WIDTH), BF16),
        scratch_shapes=[pltpu.VMEM((tq, p_len), jnp.int32),
                        pltpu.VMEM((tq, p_len), F32),
                        pltpu.VMEM((A_HEADS * tq, A_HDIM), BF16),
                        pltpu.VMEM((A_HEADS * tq, A_HDIM), F32)],
        compiler_params=_cparams(2),
        name="attn_a",
    )(pf, pf, pf, tabs_idx, tabs_idx, pb, pb, pb, bias, g.reshape(1, GROUP_WIDTH))


def _attn_b_kernel(q_ref, kp_ref, kc_ref, vp_ref, vc_ref, bias_ref, sink_ref, g_ref, o_ref):
    n = pl.program_id(1)
    grp = B_HEADS // B_KV_HEADS
    first_col = PAD_FRONT - (n - 1) * BLK
    col = lax.broadcasted_iota(jnp.int32, (grp * BLK, 2 * BLK), 1)
    pad_mask = jnp.where(col >= first_col, 0.0, NEG)
    outs = []
    for gi in range(B_KV_HEADS):
        q = jnp.concatenate([q_ref[:, (gi * grp + j) * B_HDIM:(gi * grp + j + 1) * B_HDIM] for j in range(grp)], axis=0)
        k = jnp.concatenate([kp_ref[:, gi * B_HDIM:(gi + 1) * B_HDIM], kc_ref[:, gi * B_HDIM:(gi + 1) * B_HDIM]], axis=0)
        v = jnp.concatenate([vp_ref[:, gi * B_HDIM:(gi + 1) * B_HDIM], vc_ref[:, gi * B_HDIM:(gi + 1) * B_HDIM]], axis=0)
        s = _dot_nt(q, k) + bias_ref[gi] + pad_mask
        sink = sink_ref[gi]
        m = jnp.maximum(jnp.max(s, axis=1, keepdims=True), sink)
        e = jnp.exp(s - m)
        l = jnp.sum(e, axis=1, keepdims=True) + jnp.exp(sink - m)
        o = _dot(e.astype(BF16), v) / l
        outs.extend(o[j * BLK:(j + 1) * BLK, :] for j in range(grp))
    o = jnp.concatenate(outs, axis=1)
    o_ref[...] = _rms(o, g_ref[...]).astype(BF16)


def _attn_b(pb, bias, sink_rows, g, batch, p_len, col):
    nb = p_len // BLK
    t = batch * p_len
    grp = B_HEADS // B_KV_HEADS
    return pl.pallas_call(
        _attn_b_kernel,
        grid=(batch, nb),
        in_specs=[
            pl.BlockSpec((BLK, 1024), lambda b, n: (b * nb + n, col["b_q"] // 1024)),
            pl.BlockSpec((BLK, LANES), lambda b, n: (b * nb + jnp.maximum(n - 1, 0), col["b_k"] // LANES)),
            pl.BlockSpec((BLK, LANES), lambda b, n: (b * nb + n, col["b_k"] // LANES)),
            pl.BlockSpec((BLK, LANES), lambda b, n: (b * nb + jnp.maximum(n - 1, 0), col["b_v"] // LANES)),
            pl.BlockSpec((BLK, LANES), lambda b, n: (b * nb + n, col["b_v"] // LANES)),
            pl.BlockSpec((B_KV_HEADS, grp * BLK, 2 * BLK), lambda b, n: (0, 0, 0)),
            pl.BlockSpec((B_KV_HEADS, grp * BLK, 1), lambda b, n: (0, 0, 0)),
            pl.BlockSpec((1, GROUP_WIDTH), lambda b, n: (0, 0)),
        ],
        out_specs=pl.BlockSpec((BLK, GROUP_WIDTH), lambda b, n: (b * nb + n, 0)),
        out_shape=jax.ShapeDtypeStruct((t, GROUP_WIDTH), BF16),
        compiler_params=_cparams(2),
        name="attn_b",
    )(pb, pb, pb, pb, pb, bias, sink_rows, g.reshape(1, GROUP_WIDTH))


def _c_prep_kernel(cq_ref, ckv_ref, kr_ref, tab_ref, gq_ref, gkv_ref, wq_ref, wkv_ref, q_ref, kv_ref, kro_ref):
    half = C_ROPE // 2
    tc, ts1, ts2 = tab_ref[0], tab_ref[1], tab_ref[2]
    xq = _rms(cq_ref[...], gq_ref[...]).astype(BF16)
    q = _dot(xq, wq_ref[...])
    nope_w = C_HEADS * C_NOPE
    q_ref[:, :nope_w] = q[:, :nope_w].astype(BF16)
    for h in range(C_HEADS):
        lo = nope_w + h * LANES
        q_ref[:, lo:lo + LANES] = _rope128(q[:, lo:lo + LANES], tc, ts1, ts2, half).astype(BF16)
    xkv = _rms(ckv_ref[...], gkv_ref[...]).astype(BF16)
    kv_ref[...] = _dot(xkv, wkv_ref[...]).astype(BF16)
    kro_ref[...] = _rope128(kr_ref[...], tc, ts1, ts2, half).astype(BF16)


def _c_prep(pf, tabs, g_cq, g_ckv, w_uq_p, w_ukv_p, p_len, tm, col):
    t = pf.shape[0]
    tiles = p_len // tm
    wide = 2 * C_HEADS * LANES
    return pl.pallas_call(
        _c_prep_kernel,
        grid=(t // tm,),
        in_specs=[
            pl.BlockSpec((tm, C_Q_RANK), lambda i: (i, col["c_cq"] // C_Q_RANK)),
            pl.BlockSpec((tm, C_KV_RANK), lambda i: (i, col["c_ckv"] // C_KV_RANK)),
            pl.BlockSpec((tm, LANES), lambda i: (i, col["c_kr"] // LANES)),
            pl.BlockSpec((3, tm, LANES), lambda i: (0, i % tiles, 0)),
            pl.BlockSpec((1, C_Q_RANK), lambda i: (0, 0)),
            pl.BlockSpec((1, C_KV_RANK), lambda i: (0, 0)),
            pl.BlockSpec((C_Q_RANK, wide), lambda i: (0, 0)),
            pl.BlockSpec((C_KV_RANK, wide), lambda i: (0, 0)),
        ],
        out_specs=[pl.BlockSpec((tm, wide), lambda i: (i, 0)),
                   pl.BlockSpec((tm, wide), lambda i: (i, 0)),
                   pl.BlockSpec((tm, LANES), lambda i: (i, 0))],
        out_shape=[jax.ShapeDtypeStruct((t, wide), BF16),
                   jax.ShapeDtypeStruct((t, wide), BF16),
                   jax.ShapeDtypeStruct((t, LANES), BF16)],
        compiler_params=_cparams(1),
        name="c_prep",
    )(pf, pf, pf, tabs, g_cq.reshape(1, -1), g_ckv.reshape(1, -1), w_uq_p, w_ukv_p)


def _causal_mask(q0, tq, p_len):
    qpos = q0 + lax.broadcasted_iota(jnp.int32, (tq, p_len), 0)
    kpos = lax.broadcasted_iota(jnp.int32, (tq, p_len), 1)
    return jnp.where(kpos <= qpos, jnp.where(kpos >= PAD_FRONT, 0.0, NEG), NEG)


def _attn_c_kernel(q_ref, kv_ref, kr_ref, g_ref, o_ref, *, tq, p_len):
    mask = _causal_mask(pl.program_id(1) * tq, tq, p_len)
    scale = (C_NOPE + C_ROPE) ** -0.5
    nope_w = C_HEADS * C_NOPE
    kr = kr_ref[...]
    outs = []
    for h in range(C_HEADS):
        s = _dot_nt(q_ref[:, h * C_NOPE:(h + 1) * C_NOPE], kv_ref[:, h * C_NOPE:(h + 1) * C_NOPE])
        s = s + _dot_nt(q_ref[:, nope_w + h * LANES:nope_w + (h + 1) * LANES], kr)
        s = s * scale + mask
        m = jnp.max(s, axis=1, keepdims=True)
        e = jnp.exp(s - m)
        l = jnp.sum(e, axis=1, keepdims=True)
        outs.append(_dot(e.astype(BF16), kv_ref[:, nope_w + h * C_VDIM:nope_w + (h + 1) * C_VDIM]) / l)
    o = jnp.concatenate(outs, axis=1)
    o_ref[...] = _rms(o, g_ref[...]).astype(BF16)


def _attn_c(qc, kvc, krc, g, batch, p_len, tq):
    nq = p_len // tq
    t = batch * p_len
    wide = 2 * C_HEADS * LANES
    kern = functools.partial(_attn_c_kernel, tq=tq, p_len=p_len)
    return pl.pallas_call(
        kern,
        grid=(batch, nq),
        in_specs=[pl.BlockSpec((tq, wide), lambda b, q: (b * nq + q, 0)),
                  pl.BlockSpec((p_len, wide), lambda b, q: (b, 0)),
                  pl.BlockSpec((p_len, LANES), lambda b, q: (b, 0)),
                  pl.BlockSpec((1, GROUP_WIDTH), lambda b, q: (0, 0))],
        out_specs=pl.BlockSpec((tq, GROUP_WIDTH), lambda b, q: (b * nq + q, 0)),
        out_shape=jax.ShapeDtypeStruct((t, GROUP_WIDTH), BF16),
        compiler_params=_cparams(2),
        name="attn_c",
    )(qc, kvc, krc, g.reshape(1, GROUP_WIDTH))


def _attn_d_kernel(q_ref, k_ref, v_ref, bias_ref, lam_ref, g_ref, o_ref, *, tq, p_len, lam_init):
    mask = _causal_mask(pl.program_id(1) * tq, tq, p_len)
    lp = lam_ref[...]
    lam = (jnp.exp(jnp.sum(lp[0:1] * lp[1:2], axis=1, keepdims=True))
           - jnp.exp(jnp.sum(lp[2:3] * lp[3:4], axis=1, keepdims=True)) + lam_init)
    g = g_ref[...]
    outs = []
    for h in range(D_HEADS):
        bm = bias_ref[h] + mask
        es, ls = [], []
        for c in range(2):
            lo = (2 * h + c) * D_HDIM
            s = _dot_nt(q_ref[:, lo:lo + D_HDIM], k_ref[:, lo:lo + D_HDIM]) + bm
            m = jnp.max(s, axis=1, keepdims=True)
            e = jnp.exp(s - m)
            es.append(e)
            ls.append(jnp.sum(e, axis=1, keepdims=True))
        a = es[0] * (1.0 / ls[0]) - es[1] * (lam / ls[1])
        o = _dot(a.astype(BF16), v_ref[:, h * 2 * D_HDIM:(h + 1) * 2 * D_HDIM])
        outs.append(_rms(o, g) * (1.0 - lam_init))
    o_ref[...] = jnp.concatenate(outs, axis=1).astype(BF16)


def _attn_d(pb, bias, lam_p, g, batch, p_len, tq, lam_init, col):
    nq = p_len // tq
    t = batch * p_len
    kern = functools.partial(_attn_d_kernel, tq=tq, p_len=p_len, lam_init=lam_init)
    return pl.pallas_call(
        kern,
        grid=(batch, nq),
        in_specs=[pl.BlockSpec((tq, 1024), lambda b, q: (b * nq + q, col["d_q"] // 1024)),
                  pl.BlockSpec((p_len, 1024), lambda b, q: (b, col["d_k"] // 1024)),
                  pl.BlockSpec((p_len, 1024), lambda b, q: (b, col["d_v"] // 1024)),
                  pl.BlockSpec((D_HEADS, 1, p_len), lambda b, q: (0, 0, 0)),
                  pl.BlockSpec((4, D_HDIM), lambda b, q: (0, 0)),
                  pl.BlockSpec((1, 2 * D_HDIM), lambda b, q: (0, 0))],
        out_specs=pl.BlockSpec((tq, GROUP_WIDTH), lambda b, q: (b * nq + q, 0)),
        out_shape=jax.ShapeDtypeStruct((t, GROUP_WIDTH), BF16),
        compiler_params=_cparams(2),
        name="attn_d",
    )(pb, pb, pb, bias, lam_p, g.reshape(1, 2 * D_HDIM))


def _ffn_up_kernel(x_ref, halo_ref, wg_ref, wu_ref, pg_ref, pu_ref, o_ref, *, tm, tiles_per_seq):
    i = pl.program_id(1)
    keep = jnp.where(i % tiles_per_seq != 0, 1.0, 0.0)
    x = x_ref[...]
    xh = halo_ref[...]

    def branch(w_ref, p_ref):
        w = w_ref[...]
        full = jnp.concatenate([_dot(xh, w) * keep, _dot(x, w)], axis=0)
        p = p_ref[...]
        y = (p[2:3] * full[BF16_SUBLANES:]
             + p[1:2] * pltpu.roll(full, 1, 0)[BF16_SUBLANES:]
             + p[0:1] * pltpu.roll(full, 2, 0)[BF16_SUBLANES:])
        return y + p[3:4]

    gate = branch(wg_ref, pg_ref)
    up = branch(wu_ref, pu_ref)
    o_ref[...] = (gate * jax.nn.sigmoid(gate) * up).astype(BF16)


def _ffn_up(hb, wg, wu, pg, pu, p_len, tm, tn):
    t, d = hb.shape
    f = wg.shape[1]
    tiles = p_len // tm
    hblk = tm // BF16_SUBLANES
    kern = functools.partial(_ffn_up_kernel, tm=tm, tiles_per_seq=tiles)
    return pl.pallas_call(
        kern,
        grid=(f // tn, t // tm),
        in_specs=[pl.BlockSpec((tm, d), lambda j, i: (i, 0)),
                  pl.BlockSpec((BF16_SUBLANES, d), lambda j, i: (jnp.maximum(i * hblk - 1, 0), 0)),
                  pl.BlockSpec((d, tn), lambda j, i: (0, j)),
                  pl.BlockSpec((d, tn), lambda j, i: (0, j)),
                  pl.BlockSpec((8, tn), lambda j, i: (0, j)),
                  pl.BlockSpec((8, tn), lambda j, i: (0, j))],
        out_specs=pl.BlockSpec((tm, tn), lambda j, i: (i, j)),
        out_shape=jax.ShapeDtypeStruct((t, f), BF16),
        compiler_params=_cparams(2),
        name="ffn_up",
    )(hb, hb, wg, wu, pg, pu)


def _offsets(names, sizes):
    out, o = {}, 0
    for n, s in zip(names, sizes):
        out[n] = o
        o += s
    return out, o


_BF_NAMES = ("a_q", "b_q", "d_q", "d_k", "d_v", "a_k", "a_v", "b_k", "b_v")
_BF_SIZES = (1024, 1024, 1024, 1024, 1024, 128, 128, 128, 128)
_F32_NAMES = ("i_q", "c_cq", "c_ckv", "ikw", "c_kr")
_F32_SIZES = (1024, 1024, 512, 128, 128)
COL_BF, N_BF = _offsets(_BF_NAMES, _BF_SIZES)
COL_F32, N_F32 = _offsets(_F32_NAMES, _F32_SIZES)


def _prep_w_in(w):
    d = w.shape[0]
    seg = dict(zip(("a_q", "a_k", "a_v", "i_q", "i_k", "i_w", "b_q", "b_k", "b_v", "c_cq", "c_ckv", "c_kr",
                    "d_q", "d_k", "d_v"), jnp.split(w, np.cumsum(IN_SIZES)[:-1].tolist(), axis=1)))
    seg["b_q"] = seg["b_q"] * (B_HDIM ** -0.5)
    seg["d_q"] = seg["d_q"] * (D_HDIM ** -0.5)
    wb = jnp.concatenate([seg[n] for n in _BF_NAMES], axis=1).astype(BF16)
    zeros = lambda n: jnp.zeros((d, n), w.dtype)
    ikw = jnp.concatenate([seg["i_k"], seg["i_w"], zeros(LANES - IDX_HDIM - IDX_HEADS)], axis=1)
    ckr = jnp.concatenate([seg["c_kr"], zeros(LANES - C_ROPE)], axis=1)
    wf = jnp.concatenate([seg["i_q"], seg["c_cq"], seg["c_ckv"], ikw, ckr], axis=1).astype(BF16)
    return wb, wf


def _prep_w_uq(w):
    r = w.shape[0]
    w3 = w.reshape(r, C_HEADS, C_NOPE + C_ROPE)
    nope = w3[:, :, :C_NOPE].reshape(r, C_HEADS * C_NOPE)
    rope = jnp.pad(w3[:, :, C_NOPE:], ((0, 0), (0, 0), (0, LANES - C_ROPE))).reshape(r, C_HEADS * LANES)
    return jnp.concatenate([nope, rope], axis=1).astype(BF16)


def _prep_w_ukv(w):
    r = w.shape[0]
    w3 = w.reshape(r, C_HEADS, C_NOPE + C_VDIM)
    return jnp.concatenate([w3[:, :, :C_NOPE].reshape(r, -1), w3[:, :, C_NOPE:].reshape(r, -1)], axis=1).astype(BF16)


def _prep_ffn(w_up, conv_w, conv_b, w_down):
    d_ff = w_down.shape[0]
    f_pad = -(-d_ff // FF_TILE) * FF_TILE
    padc = lambda a: jnp.pad(a, ((0, 0), (0, f_pad - d_ff)))
    wg = padc(w_up[:, :d_ff]).astype(BF16)
    wu = padc(w_up[:, d_ff:]).astype(BF16)

    def pack(lo):
        rows = jnp.concatenate([conv_w[:, lo:lo + d_ff], conv_b[None, lo:lo + d_ff],
                                jnp.zeros((8 - CONV_W - 1, d_ff), F32)], axis=0)
        return padc(rows)

    wd = jnp.pad(w_down, ((0, f_pad - d_ff), (0, 0))).astype(BF16)
    return wg, wu, pack(0), pack(d_ff), wd


def _alibi(n):
    return 2.0 ** (-8.0 * np.arange(1, n + 1, dtype=np.float64) / n)


def _swa_bias():
    grp = B_HEADS // B_KV_HEADS
    r = np.arange(BLK)[:, None]
    c = np.arange(2 * BLK)[None, :]
    diff = (r + BLK - c).astype(np.float64)
    ok = (diff >= 0) & (diff < WINDOW)
    slopes = _alibi(B_HEADS).reshape(B_KV_HEADS, grp)
    bias = np.where(ok[None, None], -slopes[:, :, None, None] * diff[None, None], NEG)
    return jnp.asarray(bias.reshape(B_KV_HEADS, grp * BLK, 2 * BLK), F32)


def kernel(x, meta_tokens, ln_in_g, ln_in_b, w_in, g_cq, g_ckv, w_uq, w_ukv, sinks, lam_q1, lam_k1, lam_q2, lam_k2,
           g_diff, g_grp, w_out, ln1_g, ln1_b, w_up, conv_w, conv_b, w_down, ln2_g, ln2_b):
    batch, s_len, d = x.shape
    depth = w_in.shape[0]
    p_len = s_len + BLK
    t = batch * p_len
    k_top = min(TOPK_MAX, s_len // 4)
    alpha = (2 * depth) ** 0.25
    tm = _row_tile(p_len, 544)
    tm_ln = _row_tile(p_len, 272)
    tq = BLK
    grp = B_HEADS // B_KV_HEADS

    pad = jnp.zeros((batch, PAD_FRONT, d), x.dtype)
    meta = jnp.broadcast_to(meta_tokens[None].astype(x.dtype), (batch, N_META, d))
    h0 = jnp.concatenate([pad, meta, x], axis=1).reshape(t, d)
    h, hb = _ln_in(h0, ln_in_g, ln_in_b, tm_ln)

    pos = jnp.arange(p_len, dtype=jnp.int32) - PAD_FRONT
    tabs_idx = _rope_tables(pos, IDX_ROPE, IDX_HDIM)
    tabs_c = _rope_tables(pos, C_ROPE, LANES)
    kidx = np.arange(p_len, dtype=np.float64)
    bias_a = jnp.asarray(_alibi(A_HEADS)[:, None, None] * kidx[None, None, :], F32)
    bias_d = jnp.asarray(_alibi(D_HEADS)[:, None, None] * kidx[None, None, :], F32)
    bias_b = _swa_bias()

    for l in range(depth):
        wb, wf = _prep_w_in(w_in[l])
        pb = _mm(hb, wb, tm, N_BF // 4, BF16, "proj_bf16")
        pf = _mm(hb, wf, tm, N_F32 // 2, F32, "proj_f32")

        o_a = _attn_a(pf, pb, tabs_idx, bias_a, g_grp[l, 0], batch, p_len, tq, k_top, {**COL_F32, **COL_BF})
        sink_rows = jnp.repeat(sinks[l].astype(F32).reshape(B_KV_HEADS, grp), BLK, axis=1)[..., None]
        o_b = _attn_b(pb, bias_b, sink_rows, g_grp[l, 1], batch, p_len, COL_BF)
        qc, kvc, krc = _c_prep(pf, tabs_c, g_cq[l], g_ckv[l], _prep_w_uq(w_uq[l]), _prep_w_ukv(w_ukv[l]),
                               p_len, tm_ln, COL_F32)
        o_c = _attn_c(qc, kvc, krc, g_grp[l, 2], batch, p_len, tq)
        lam_init = 0.8 - 0.6 * math.exp(-0.3 * l)
        lam_p = jnp.stack([lam_q1[l], lam_k1[l], lam_q2[l], lam_k2[l]]).astype(F32)
        o_d = _attn_d(pb, bias_d, lam_p, g_diff[l], batch, p_len, tq, lam_init, COL_BF)

        mix_in = jnp.concatenate([o_a, o_b, o_c, o_d], axis=1)
        mix = _mm(mix_in, w_out[l].astype(BF16), tm, min(1024, d), F32, "out_proj")
        h, hb_ffn = _res_ln(h, mix, ln1_g[l], ln1_b[l], alpha, tm_ln, p_len, True)

        wg, wu, pg, pu, wd = _prep_ffn(w_up[l], conv_w[l], conv_b[l], w_down[l])
        act = _ffn_up(hb_ffn, wg, wu, pg, pu, p_len, tm, min(FF_TILE, wg.shape[1]))
        ffn = _mm(act, wd, tm, min(512, d), F32, "ffn_down")
        h, hb = _res_ln(h, ffn, ln2_g[l], ln2_b[l], alpha, tm_ln, p_len, False)

    return h.reshape(batch, p_len, d)[:, BLK:]
```

```python
import functools
import math

import jax
import jax.numpy as jnp
import numpy as np
from jax import lax
from jax.experimental import pallas as pl
from jax.experimental.pallas import tpu as pltpu

N_META = 16
BLK = 128
PAD_FRONT = BLK - N_META
A_HEADS, A_HDIM = 8, 128
IDX_HEADS, IDX_HDIM, IDX_ROPE = 16, 64, 32
TOPK_MAX = 256
B_HEADS, B_KV_HEADS, B_HDIM = 16, 2, 64
WINDOW = 128
C_HEADS, C_Q_RANK, C_KV_RANK, C_NOPE, C_ROPE, C_VDIM = 8, 1024, 512, 128, 64, 128
D_HEADS, D_HDIM = 8, 64
GROUP_WIDTH = 1024
CONV_W = 3
ROPE_BASE = 10000.0
NEG = -1e30
IN_SIZES = (1024, 128, 128, 1024, 64, 16, 1024, 128, 128, 1024, 512, 64, 1024, 1024, 1024)

LANES = 128
BF16_SUBLANES = 16
VMEM_LIMIT = 56 * 1024 * 1024
FF_TILE = 256
N_EXTENT_VARIANTS = 6

F32 = jnp.float32
BF16 = jnp.bfloat16


def _cparams(n_axes):
    return pltpu.CompilerParams(dimension_semantics=("arbitrary",) * n_axes, vmem_limit_bytes=VMEM_LIMIT)


def _row_tile(p_len, target):
    best = BF16_SUBLANES
    for t in range(BF16_SUBLANES, target + 1, BF16_SUBLANES):
        if p_len % t == 0:
            best = t
    return best


def _dot(a, b):
    return jnp.dot(a, b, preferred_element_type=F32)


def _dot_nt(a, b):
    return lax.dot_general(a, b, (((1,), (1,)), ((), ())), preferred_element_type=F32)


def _ln_rows(x, g, b):
    mu = jnp.mean(x, axis=-1, keepdims=True)
    xc = x - mu
    var = jnp.mean(xc * xc, axis=-1, keepdims=True)
    return xc * lax.rsqrt(var + 1e-5) * g + b


def _ln_in_kernel(x_ref, g_ref, b_ref, h_ref, hb_ref):
    y = _ln_rows(x_ref[...], g_ref[...], b_ref[...])
    h_ref[...] = y
    hb_ref[...] = y.astype(BF16)


def _ln_in(x, g, b, tm):
    t, d = x.shape
    return pl.pallas_call(
        _ln_in_kernel,
        grid=(t // tm,),
        in_specs=[pl.BlockSpec((tm, d), lambda i: (i, 0)),
                  pl.BlockSpec((1, d), lambda i: (0, 0)),
                  pl.BlockSpec((1, d), lambda i: (0, 0))],
        out_specs=[pl.BlockSpec((tm, d), lambda i: (i, 0)),
                   pl.BlockSpec((tm, d), lambda i: (i, 0))],
        out_shape=[jax.ShapeDtypeStruct((t, d), F32), jax.ShapeDtypeStruct((t, d), BF16)],
        compiler_params=_cparams(1),
        name="ln_in",
    )(x, g.reshape(1, d), b.reshape(1, d))


def _res_ln_kernel(h_ref, y_ref, g_ref, b_ref, o_ref, ob_ref, *, alpha, tm, tiles_per_seq, zero_pad):
    out = _ln_rows(alpha * h_ref[...] + y_ref[...], g_ref[...], b_ref[...])
    o_ref[...] = out
    if zero_pad:
        p0 = (pl.program_id(0) % tiles_per_seq) * tm
        pos = p0 + lax.broadcasted_iota(jnp.int32, out.shape, 0)
        out = jnp.where(pos >= PAD_FRONT, out, 0.0)
    ob_ref[...] = out.astype(BF16)


def _res_ln(h, y, g, b, alpha, tm, p_len, zero_pad):
    t, d = h.shape
    kern = functools.partial(_res_ln_kernel, alpha=alpha, tm=tm, tiles_per_seq=p_len // tm, zero_pad=zero_pad)
    return pl.pallas_call(
        kern,
        grid=(t // tm,),
        in_specs=[pl.BlockSpec((tm, d), lambda i: (i, 0)),
                  pl.BlockSpec((tm, d), lambda i: (i, 0)),
                  pl.BlockSpec((1, d), lambda i: (0, 0)),
                  pl.BlockSpec((1, d), lambda i: (0, 0))],
        out_specs=[pl.BlockSpec((tm, d), lambda i: (i, 0)),
                   pl.BlockSpec((tm, d), lambda i: (i, 0))],
        out_shape=[jax.ShapeDtypeStruct((t, d), F32), jax.ShapeDtypeStruct((t, d), BF16)],
        compiler_params=_cparams(1),
        name="res_ln",
    )(h, y, g.reshape(1, d), b.reshape(1, d))


def _mm_kernel(a_ref, w_ref, o_ref):
    o_ref[...] = _dot(a_ref[...], w_ref[...]).astype(o_ref.dtype)


def _mm(a, w, tm, tn, out_dtype, name):
    t, k = a.shape
    n = w.shape[1]
    return pl.pallas_call(
        _mm_kernel,
        grid=(n // tn, t // tm),
        in_specs=[pl.BlockSpec((tm, k), lambda j, i: (i, 0)),
                  pl.BlockSpec((k, tn), lambda j, i: (0, j))],
        out_specs=pl.BlockSpec((tm, tn), lambda j, i: (i, j)),
        out_shape=jax.ShapeDtypeStruct((t, n), out_dtype),
        compiler_params=_cparams(2),
        name=name,
    )(a, w)


def _rope128(x, c, s1, s2, half):
    return x * c + pltpu.roll(x, LANES - half, 1) * s1 + pltpu.roll(x, half, 1) * s2


def _rope_tables(pos, rot_dim, period):
    half = rot_dim // 2
    inv = ROPE_BASE ** (-jnp.arange(half, dtype=F32) / half)
    ang = pos.astype(F32)[:, None] * inv[None]
    cos, sin = jnp.cos(ang), jnp.sin(ang)
    n = pos.shape[0]
    lane = np.arange(LANES) % period
    li = lane % half
    cos_l, sin_l = cos[:, li], sin[:, li]
    first = jnp.asarray(lane < half)[None]
    second = jnp.asarray((lane >= half) & (lane < rot_dim))[None]
    c = jnp.where(first | second, cos_l, 1.0)
    s1 = jnp.where(first, -sin_l, 0.0)
    s2 = jnp.where(second, sin_l, 0.0)
    return jnp.stack([c, s1, s2]).astype(F32).reshape(3, n, LANES)


def _rms(x, g, eps=1e-6):
    return x * lax.rsqrt(jnp.mean(x * x, axis=-1, keepdims=True) + eps) * g


def _key_extents(nblk):
    step = -(-nblk // N_EXTENT_VARIANTS)
    return [(lo, min(lo + step, nblk), min(lo + step, nblk)) for lo in range(0, nblk, step)]


def _for_each_extent(qi, nblk, body):
    for lo, hi, eb in _key_extents(nblk):
        @pl.when(jnp.logical_and(qi >= lo, qi < hi))
        def _():
            body(eb * BLK)


def _attn_a_kernel(iq_ref, wq_ref, ikw_ref, tq_ref, tk_ref, aq_ref, ak_ref, av_ref, bias_ref, g_ref,
                   o_ref, key_ref, mask_ref, qs_ref, os_ref, *, tq, p_len, k_top):
    qi = pl.program_id(1)
    q0 = qi * tq
    half = IDX_ROPE // 2
    for h in range(A_HEADS):
        qs_ref[h * tq:(h + 1) * tq, :] = aq_ref[:, h * A_HDIM:(h + 1) * A_HDIM]

    def body(ext):
        ik = _rope128(ikw_ref[:ext, :], tk_ref[0, :ext, :], tk_ref[1, :ext, :], tk_ref[2, :ext, :],
                      half)[:, :IDX_HDIM].astype(BF16)
        w = wq_ref[...][:, IDX_HDIM:IDX_HDIM + IDX_HEADS] * ((IDX_HEADS * IDX_HDIM) ** -0.5)
        tc, ts1, ts2 = tq_ref[0], tq_ref[1], tq_ref[2]
        score = jnp.zeros((tq, ext), F32)
        for c in range(IDX_HEADS * IDX_HDIM // LANES):
            chunk = _rope128(iq_ref[:, c * LANES:(c + 1) * LANES], tc, ts1, ts2, half).astype(BF16)
            for j in range(LANES // IDX_HDIM):
                h = c * (LANES // IDX_HDIM) + j
                rel = jnp.maximum(_dot_nt(chunk[:, j * IDX_HDIM:(j + 1) * IDX_HDIM], ik), 0.0)
                score = score + w[:, h:h + 1] * rel
        qpos = q0 + lax.broadcasted_iota(jnp.int32, (tq, ext), 0)
        kpos = lax.broadcasted_iota(jnp.int32, (tq, ext), 1)
        vis = jnp.where(kpos <= qpos, jnp.where(kpos >= PAD_FRONT, 1.0, 0.0), 0.0)
        score = jnp.where(vis > 0.0, score, NEG)
        bits = lax.bitcast_convert_type(score, jnp.int32)
        key_ref[:, :ext] = jnp.where(bits < 0, bits ^ jnp.int32(0x7FFFFFFF), bits)
        kf = jnp.float32(k_top)

        def count(pred):
            return jnp.sum(jnp.where(pred, 1.0, 0.0), axis=1, keepdims=True)

        def vbody(i, lo):
            cand = lo + jnp.left_shift(jnp.int32(1), 31 - i)
            return jnp.where(count(key_ref[:, :ext] >= cand) >= kf, cand, lo)

        thr = lax.fori_loop(0, 32, vbody, jnp.full((tq, 1), -2 ** 31, jnp.int32))
        key = key_ref[:, :ext]
        n_gt = count(key > thr)
        n_eq = count(key == thr)
        need = kf - n_gt
        n_eq_vis = jnp.sum(jnp.where(key == thr, vis, 0.0), axis=1, keepdims=True)
        mask_ref[:, :ext] = jnp.where(key >= thr, jnp.where(vis > 0.0, 0.0, NEG), NEG)
        surplus = jnp.where(n_eq_vis > 0.0, jnp.where(n_eq > need, 1.0, 0.0), 0.0)

        @pl.when(jnp.max(surplus) > 0.0)
        def _():
            nbits = max(1, (ext - 1).bit_length())
            keyv = key_ref[:, :ext]
            col = lax.broadcasted_iota(jnp.int32, (tq, ext), 1)

            def tbody(i, j):
                cand = j + jnp.left_shift(jnp.int32(1), nbits - 1 - i)
                f = jnp.sum(jnp.where(keyv == thr, jnp.where(col < cand, 1.0, 0.0), 0.0), axis=1, keepdims=True)
                return jnp.where(f < need, cand, j)

            jstar = lax.fori_loop(0, nbits, tbody, jnp.zeros((tq, 1), jnp.int32))
            chosen = jnp.where(keyv > thr, 1.0, jnp.where(keyv == thr, jnp.where(col <= jstar, 1.0, 0.0), 0.0))
            mask_ref[:, :ext] = jnp.where(chosen > 0.0, jnp.where(vis > 0.0, 0.0, NEG), NEG)

        scale = A_HDIM ** -0.5

        def hbody(h, carry):
            r0 = pl.multiple_of(h * tq, tq)
            s = _dot_nt(qs_ref[pl.ds(r0, tq), :], ak_ref[:ext, :]) * scale + bias_ref[h, :, :ext] + mask_ref[:, :ext]
            m = jnp.max(s, axis=1, keepdims=True)
            e = jnp.exp(s - m)
            l = jnp.sum(e, axis=1, keepdims=True)
            os_ref[pl.ds(r0, tq), :] = _dot(e.astype(BF16), av_ref[:ext, :]) / l
            return carry

        lax.fori_loop(0, A_HEADS, hbody, 0)

    _for_each_extent(qi, p_len // BLK, body)
    o = jnp.concatenate([os_ref[h * tq:(h + 1) * tq, :] for h in range(A_HEADS)], axis=1)
    o_ref[...] = _rms(o, g_ref[...]).astype(BF16)


def _attn_a(pf, pb, tabs_idx, bias, g, batch, p_len, tq, k_top, col):
    nq = p_len // tq
    t = batch * p_len
    kern = functools.partial(_attn_a_kernel, tq=tq, p_len=p_len, k_top=k_top)
    return pl.pallas_call(
        kern,
        grid=(batch, nq),
        in_specs=[
            pl.BlockSpec((tq, 1024), lambda b, q: (b * nq + q, col["i_q"] // 1024)),
            pl.BlockSpec((tq, LANES), lambda b, q: (b * nq + q, col["ikw"] // LANES)),
            pl.BlockSpec((p_len, LANES), lambda b, q: (b, col["ikw"] // LANES)),
            pl.BlockSpec((3, tq, LANES), lambda b, q: (0, q, 0)),
            pl.BlockSpec((3, p_len, LANES), lambda b, q: (0, 0, 0)),
            pl.BlockSpec((tq, 1024), lambda b, q: (b * nq + q, col["a_q"] // 1024)),
            pl.BlockSpec((p_len, LANES), lambda b, q: (b, col["a_k"] // LANES)),
            pl.BlockSpec((p_len, LANES), lambda b, q: (b, col["a_v"] // LANES)),
            pl.BlockSpec((A_HEADS, 1, p_len), lambda b, q: (0, 0, 0)),
            pl.BlockSpec((1, GROUP_WIDTH), lambda b, q: (0, 0)),
        ],
        out_specs=pl.BlockSpec((tq, GROUP_WIDTH), lambda b, q: (b * nq + q, 0)),
        out_shape=jax.ShapeDtypeStruct((t, GROUP_WIDTH), BF16),
        scratch_shapes=[pltpu.VMEM((tq, p_len), jnp.int32),
                        pltpu.VMEM((tq, p_len), F32),
                        pltpu.VMEM((A_HEADS * tq, A_HDIM), BF16),
                        pltpu.VMEM((A_HEADS * tq, A_HDIM), F32)],
        compiler_params=_cparams(2),
        name="attn_a",
    )(pf, pf, pf, tabs_idx, tabs_idx, pb, pb, pb, bias, g.reshape(1, GROUP_WIDTH))


def _attn_b_kernel(q_ref, kp_ref, kc_ref, vp_ref, vc_ref, bias_ref, sink_ref, g_ref, o_ref):
    n = pl.program_id(1)
    grp = B_HEADS // B_KV_HEADS
    first_col = PAD_FRONT - (n - 1) * BLK
    col = lax.broadcasted_iota(jnp.int32, (grp * BLK, 2 * BLK), 1)
    pad_mask = jnp.where(col >= first_col, 0.0, NEG)
    outs = []
    for gi in range(B_KV_HEADS):
        q = jnp.concatenate([q_ref[:, (gi * grp + j) * B_HDIM:(gi * grp + j + 1) * B_HDIM] for j in range(grp)], axis=0)
        k = jnp.concatenate([kp_ref[:, gi * B_HDIM:(gi + 1) * B_HDIM], kc_ref[:, gi * B_HDIM:(gi + 1) * B_HDIM]], axis=0)
        v = jnp.concatenate([vp_ref[:, gi * B_HDIM:(gi + 1) * B_HDIM], vc_ref[:, gi * B_HDIM:(gi + 1) * B_HDIM]], axis=0)
        s = _dot_nt(q, k) + bias_ref[gi] + pad_mask
        sink = sink_ref[gi]
        m = jnp.maximum(jnp.max(s, axis=1, keepdims=True), sink)
        e = jnp.exp(s - m)
        l = jnp.sum(e, axis=1, keepdims=True) + jnp.exp(sink - m)
        o = _dot(e.astype(BF16), v) / l
        outs.extend(o[j * BLK:(j + 1) * BLK, :] for j in range(grp))
    o = jnp.concatenate(outs, axis=1)
    o_ref[...] = _rms(o, g_ref[...]).astype(BF16)


def _attn_b(pb, bias, sink_rows, g, batch, p_len, col):
    nb = p_len // BLK
    t = batch * p_len
    grp = B_HEADS // B_KV_HEADS
    return pl.pallas_call(
        _attn_b_kernel,
        grid=(batch, nb),
        in_specs=[
            pl.BlockSpec((BLK, 1024), lambda b, n: (b * nb + n, col["b_q"] // 1024)),
            pl.BlockSpec((BLK, LANES), lambda b, n: (b * nb + jnp.maximum(n - 1, 0), col["b_k"] // LANES)),
            pl.BlockSpec((BLK, LANES), lambda b, n: (b * nb + n, col["b_k"] // LANES)),
            pl.BlockSpec((BLK, LANES), lambda b, n: (b * nb + jnp.maximum(n - 1, 0), col["b_v"] // LANES)),
            pl.BlockSpec((BLK, LANES), lambda b, n: (b * nb + n, col["b_v"] // LANES)),
            pl.BlockSpec((B_KV_HEADS, grp * BLK, 2 * BLK), lambda b, n: (0, 0, 0)),
            pl.BlockSpec((B_KV_HEADS, grp * BLK, 1), lambda b, n: (0, 0, 0)),
            pl.BlockSpec((1, GROUP_WIDTH), lambda b, n: (0, 0)),
        ],
        out_specs=pl.BlockSpec((BLK, GROUP_WIDTH), lambda b, n: (b * nb + n, 0)),
        out_shape=jax.ShapeDtypeStruct((t, GROUP_WIDTH), BF16),
        compiler_params=_cparams(2),
        name="attn_b",
    )(pb, pb, pb, pb, pb, bias, sink_rows, g.reshape(1, GROUP_WIDTH))


def _c_prep_kernel(cq_ref, ckv_ref, kr_ref, tab_ref, gq_ref, gkv_ref, wq_ref, wkv_ref, q_ref, kv_ref, kro_ref):
    half = C_ROPE // 2
    tc, ts1, ts2 = tab_ref[0], tab_ref[1], tab_ref[2]
    xq = _rms(cq_ref[...], gq_ref[...]).astype(BF16)
    q = _dot(xq, wq_ref[...])
    nope_w = C_HEADS * C_NOPE
    q_ref[:, :nope_w] = q[:, :nope_w].astype(BF16)
    for h in range(C_HEADS):
        lo = nope_w + h * LANES
        q_ref[:, lo:lo + LANES] = _rope128(q[:, lo:lo + LANES], tc, ts1, ts2, half).astype(BF16)
    xkv = _rms(ckv_ref[...], gkv_ref[...]).astype(BF16)
    kv_ref[...] = _dot(xkv, wkv_ref[...]).astype(BF16)
    kro_ref[...] = _rope128(kr_ref[...], tc, ts1, ts2, half).astype(BF16)


def _c_prep(pf, tabs, g_cq, g_ckv, w_uq_p, w_ukv_p, p_len, tm, col):
    t = pf.shape[0]
    tiles = p_len // tm
    wide = 2 * C_HEADS * LANES
    return pl.pallas_call(
        _c_prep_kernel,
        grid=(t // tm,),
        in_specs=[
            pl.BlockSpec((tm, C_Q_RANK), lambda i: (i, col["c_cq"] // C_Q_RANK)),
            pl.BlockSpec((tm, C_KV_RANK), lambda i: (i, col["c_ckv"] // C_KV_RANK)),
            pl.BlockSpec((tm, LANES), lambda i: (i, col["c_kr"] // LANES)),
            pl.BlockSpec((3, tm, LANES), lambda i: (0, i % tiles, 0)),
            pl.BlockSpec((1, C_Q_RANK), lambda i: (0, 0)),
            pl.BlockSpec((1, C_KV_RANK), lambda i: (0, 0)),
            pl.BlockSpec((C_Q_RANK, wide), lambda i: (0, 0)),
            pl.BlockSpec((C_KV_RANK, wide), lambda i: (0, 0)),
        ],
        out_specs=[pl.BlockSpec((tm, wide), lambda i: (i, 0)),
                   pl.BlockSpec((tm, wide), lambda i: (i, 0)),
                   pl.BlockSpec((tm, LANES), lambda i: (i, 0))],
        out_shape=[jax.ShapeDtypeStruct((t, wide), BF16),
                   jax.ShapeDtypeStruct((t, wide), BF16),
                   jax.ShapeDtypeStruct((t, LANES), BF16)],
        compiler_params=_cparams(1),
        name="c_prep",
    )(pf, pf, pf, tabs, g_cq.reshape(1, -1), g_ckv.reshape(1, -1), w_uq_p, w_ukv_p)


def _causal_mask(q0, tq, ext):
    qpos = q0 + lax.broadcasted_iota(jnp.int32, (tq, ext), 0)
    kpos = lax.broadcasted_iota(jnp.int32, (tq, ext), 1)
    return jnp.where(kpos <= qpos, jnp.where(kpos >= PAD_FRONT, 0.0, NEG), NEG)


def _attn_c_kernel(q_ref, kv_ref, kr_ref, g_ref, o_ref, os_ref, *, tq, p_len):
    qi = pl.program_id(1)
    scale = (C_NOPE + C_ROPE) ** -0.5
    nope_w = C_HEADS * C_NOPE

    def body(ext):
        mask = _causal_mask(qi * tq, tq, ext)
        kr = kr_ref[:ext, :]
        for h in range(C_HEADS):
            s = _dot_nt(q_ref[:, h * C_NOPE:(h + 1) * C_NOPE], kv_ref[:ext, h * C_NOPE:(h + 1) * C_NOPE])
            s = s + _dot_nt(q_ref[:, nope_w + h * LANES:nope_w + (h + 1) * LANES], kr)
            s = s * scale + mask
            m = jnp.max(s, axis=1, keepdims=True)
            e = jnp.exp(s - m)
            l = jnp.sum(e, axis=1, keepdims=True)
            v = kv_ref[:ext, nope_w + h * C_VDIM:nope_w + (h + 1) * C_VDIM]
            os_ref[:, h * C_VDIM:(h + 1) * C_VDIM] = _dot(e.astype(BF16), v) / l

    _for_each_extent(qi, p_len // BLK, body)
    o_ref[...] = _rms(os_ref[...], g_ref[...]).astype(BF16)


def _attn_c(qc, kvc, krc, g, batch, p_len, tq):
    nq = p_len // tq
    t = batch * p_len
    wide = 2 * C_HEADS * LANES
    kern = functools.partial(_attn_c_kernel, tq=tq, p_len=p_len)
    return pl.pallas_call(
        kern,
        grid=(batch, nq),
        in_specs=[pl.BlockSpec((tq, wide), lambda b, q: (b * nq + q, 0)),
                  pl.BlockSpec((p_len, wide), lambda b, q: (b, 0)),
                  pl.BlockSpec((p_len, LANES), lambda b, q: (b, 0)),
                  pl.BlockSpec((1, GROUP_WIDTH), lambda b, q: (0, 0))],
        out_specs=pl.BlockSpec((tq, GROUP_WIDTH), lambda b, q: (b * nq + q, 0)),
        out_shape=jax.ShapeDtypeStruct((t, GROUP_WIDTH), BF16),
        scratch_shapes=[pltpu.VMEM((tq, GROUP_WIDTH), F32)],
        compiler_params=_cparams(2),
        name="attn_c",
    )(qc, kvc, krc, g.reshape(1, GROUP_WIDTH))


def _attn_d_kernel(q_ref, k_ref, v_ref, bias_ref, lam_ref, g_ref, o_ref, os_ref, *, tq, p_len, lam_init):
    qi = pl.program_id(1)
    lp = lam_ref[...]
    lam = (jnp.exp(jnp.sum(lp[0:1] * lp[1:2], axis=1, keepdims=True))
           - jnp.exp(jnp.sum(lp[2:3] * lp[3:4], axis=1, keepdims=True)) + lam_init)
    vw = 2 * D_HDIM

    def body(ext):
        mask = _causal_mask(qi * tq, tq, ext)
        for h in range(D_HEADS):
            bm = bias_ref[h, :, :ext] + mask
            es, ls = [], []
            for c in range(2):
                lo = (2 * h + c) * D_HDIM
                s = _dot_nt(q_ref[:, lo:lo + D_HDIM], k_ref[:ext, lo:lo + D_HDIM]) + bm
                m = jnp.max(s, axis=1, keepdims=True)
                e = jnp.exp(s - m)
                es.append(e)
                ls.append(jnp.sum(e, axis=1, keepdims=True))
            a = es[0] * (1.0 / ls[0]) - es[1] * (lam / ls[1])
            os_ref[:, h * vw:(h + 1) * vw] = _dot(a.astype(BF16), v_ref[:ext, h * vw:(h + 1) * vw])

    _for_each_extent(qi, p_len // BLK, body)
    g = g_ref[...]
    outs = [_rms(os_ref[:, h * vw:(h + 1) * vw], g) * (1.0 - lam_init) for h in range(D_HEADS)]
    o_ref[...] = jnp.concatenate(outs, axis=1).astype(BF16)


def _attn_d(pb, bias, lam_p, g, batch, p_len, tq, lam_init, col):
    nq = p_len // tq
    t = batch * p_len
    kern = functools.partial(_attn_d_kernel, tq=tq, p_len=p_len, lam_init=lam_init)
    return pl.pallas_call(
        kern,
        grid=(batch, nq),
        in_specs=[pl.BlockSpec((tq, 1024), lambda b, q: (b * nq + q, col["d_q"] // 1024)),
                  pl.BlockSpec((p_len, 1024), lambda b, q: (b, col["d_k"] // 1024)),
                  pl.BlockSpec((p_len, 1024), lambda b, q: (b, col["d_v"] // 1024)),
                  pl.BlockSpec((D_HEADS, 1, p_len), lambda b, q: (0, 0, 0)),
                  pl.BlockSpec((4, D_HDIM), lambda b, q: (0, 0)),
                  pl.BlockSpec((1, 2 * D_HDIM), lambda b, q: (0, 0))],
        out_specs=pl.BlockSpec((tq, GROUP_WIDTH), lambda b, q: (b * nq + q, 0)),
        out_shape=jax.ShapeDtypeStruct((t, GROUP_WIDTH), BF16),
        scratch_shapes=[pltpu.VMEM((tq, GROUP_WIDTH), F32)],
        compiler_params=_cparams(2),
        name="attn_d",
    )(pb, pb, pb, bias, lam_p, g.reshape(1, 2 * D_HDIM))


def _ffn_up_kernel(x_ref, wg_ref, wu_ref, cwg_ref, cwu_ref, cbg_ref, cbu_ref, o_ref, wgb_ref, wub_ref, carry_ref,
                   *, tm, tiles_per_seq):
    i = pl.program_id(1)

    @pl.when(i == 0)
    def _():
        wgb_ref[...] = wg_ref[...].astype(BF16)
        wub_ref[...] = wu_ref[...].astype(BF16)

    @pl.when(i % tiles_per_seq == 0)
    def _():
        carry_ref[...] = jnp.zeros_like(carry_ref)

    x = x_ref[...]

    def branch(b, wb_ref, cw_ref, cb_ref):
        cur = _dot(x, wb_ref[...])
        full = jnp.concatenate([carry_ref[b], cur], axis=0)
        carry_ref[b] = cur[tm - 8:, :]
        cw = cw_ref[...]
        y = cw[2:3] * cur + cw[1:2] * pltpu.roll(full, 1, 0)[8:] + cw[0:1] * pltpu.roll(full, 2, 0)[8:]
        return y + cb_ref[...]

    gate = branch(0, wgb_ref, cwg_ref, cbg_ref)
    up = branch(1, wub_ref, cwu_ref, cbu_ref)
    o_ref[...] = (gate * jax.nn.sigmoid(gate) * up).astype(BF16)


def _ffn_up(hb, w_up, conv_w, conv_b, p_len, tm, tn):
    t, d = hb.shape
    f = w_up.shape[1] // 2
    assert f % tn == 0 and tm % 8 == 0
    nj = f // tn
    kern = functools.partial(_ffn_up_kernel, tm=tm, tiles_per_seq=p_len // tm)
    cb = conv_b.reshape(1, 2 * f)
    return pl.pallas_call(
        kern,
        grid=(nj, t // tm),
        in_specs=[pl.BlockSpec((tm, d), lambda j, i: (i, 0)),
                  pl.BlockSpec((d, tn), lambda j, i: (0, j)),
                  pl.BlockSpec((d, tn), lambda j, i: (0, nj + j)),
                  pl.BlockSpec((CONV_W, tn), lambda j, i: (0, j)),
                  pl.BlockSpec((CONV_W, tn), lambda j, i: (0, nj + j)),
                  pl.BlockSpec((1, tn), lambda j, i: (0, j)),
                  pl.BlockSpec((1, tn), lambda j, i: (0, nj + j))],
        out_specs=pl.BlockSpec((tm, tn), lambda j, i: (i, j)),
        out_shape=jax.ShapeDtypeStruct((t, f), BF16),
        scratch_shapes=[pltpu.VMEM((d, tn), BF16), pltpu.VMEM((d, tn), BF16), pltpu.VMEM((2, 8, tn), F32)],
        compiler_params=_cparams(2),
        name="ffn_up",
    )(hb, w_up, w_up, conv_w, conv_w, cb, cb)


def _offsets(names, sizes):
    out, o = {}, 0
    for n, s in zip(names, sizes):
        out[n] = o
        o += s
    return out, o


_BF_NAMES = ("a_q", "b_q", "d_q", "d_k", "d_v", "a_k", "a_v", "b_k", "b_v")
_BF_SIZES = (1024, 1024, 1024, 1024, 1024, 128, 128, 128, 128)
_F32_NAMES = ("i_q", "c_cq", "c_ckv", "ikw", "c_kr")
_F32_SIZES = (1024, 1024, 512, 128, 128)
COL_BF, N_BF = _offsets(_BF_NAMES, _BF_SIZES)
COL_F32, N_F32 = _offsets(_F32_NAMES, _F32_SIZES)


def _prep_w_in(w):
    d = w.shape[0]
    seg = dict(zip(("a_q", "a_k", "a_v", "i_q", "i_k", "i_w", "b_q", "b_k", "b_v", "c_cq", "c_ckv", "c_kr",
                    "d_q", "d_k", "d_v"), jnp.split(w, np.cumsum(IN_SIZES)[:-1].tolist(), axis=1)))
    seg["b_q"] = seg["b_q"] * (B_HDIM ** -0.5)
    seg["d_q"] = seg["d_q"] * (D_HDIM ** -0.5)
    wb = jnp.concatenate([seg[n] for n in _BF_NAMES], axis=1).astype(BF16)
    zeros = lambda n: jnp.zeros((d, n), w.dtype)
    ikw = jnp.concatenate([seg["i_k"], seg["i_w"], zeros(LANES - IDX_HDIM - IDX_HEADS)], axis=1)
    ckr = jnp.concatenate([seg["c_kr"], zeros(LANES - C_ROPE)], axis=1)
    wf = jnp.concatenate([seg["i_q"], seg["c_cq"], seg["c_ckv"], ikw, ckr], axis=1).astype(BF16)
    return wb, wf


def _prep_w_uq(w):
    r = w.shape[0]
    w3 = w.reshape(r, C_HEADS, C_NOPE + C_ROPE)
    nope = w3[:, :, :C_NOPE].reshape(r, C_HEADS * C_NOPE)
    rope = jnp.pad(w3[:, :, C_NOPE:], ((0, 0), (0, 0), (0, LANES - C_ROPE))).reshape(r, C_HEADS * LANES)
    return jnp.concatenate([nope, rope], axis=1).astype(BF16)


def _prep_w_ukv(w):
    r = w.shape[0]
    w3 = w.reshape(r, C_HEADS, C_NOPE + C_VDIM)
    return jnp.concatenate([w3[:, :, :C_NOPE].reshape(r, -1), w3[:, :, C_NOPE:].reshape(r, -1)], axis=1).astype(BF16)


def _alibi(n):
    return 2.0 ** (-8.0 * np.arange(1, n + 1, dtype=np.float64) / n)


def _swa_bias():
    grp = B_HEADS // B_KV_HEADS
    r = np.arange(BLK)[:, None]
    c = np.arange(2 * BLK)[None, :]
    diff = (r + BLK - c).astype(np.float64)
    ok = (diff >= 0) & (diff < WINDOW)
    slopes = _alibi(B_HEADS).reshape(B_KV_HEADS, grp)
    bias = np.where(ok[None, None], -slopes[:, :, None, None] * diff[None, None], NEG)
    return jnp.asarray(bias.reshape(B_KV_HEADS, grp * BLK, 2 * BLK), F32)


def kernel(x, meta_tokens, ln_in_g, ln_in_b, w_in, g_cq, g_ckv, w_uq, w_ukv, sinks, lam_q1, lam_k1, lam_q2, lam_k2,
           g_diff, g_grp, w_out, ln1_g, ln1_b, w_up, conv_w, conv_b, w_down, ln2_g, ln2_b):
    batch, s_len, d = x.shape
    depth = w_in.shape[0]
    p_len = s_len + BLK
    t = batch * p_len
    k_top = min(TOPK_MAX, s_len // 4)
    alpha = (2 * depth) ** 0.25
    tm = _row_tile(p_len, 544)
    tm_ln = _row_tile(p_len, 272)
    tq = BLK
    grp = B_HEADS // B_KV_HEADS

    pad = jnp.zeros((batch, PAD_FRONT, d), x.dtype)
    meta = jnp.broadcast_to(meta_tokens[None].astype(x.dtype), (batch, N_META, d))
    h0 = jnp.concatenate([pad, meta, x], axis=1).reshape(t, d)
    h, hb = _ln_in(h0, ln_in_g, ln_in_b, tm_ln)

    pos = jnp.arange(p_len, dtype=jnp.int32) - PAD_FRONT
    tabs_idx = _rope_tables(pos, IDX_ROPE, IDX_HDIM)
    tabs_c = _rope_tables(pos, C_ROPE, LANES)
    kidx = np.arange(p_len, dtype=np.float64)
    bias_a = jnp.asarray(_alibi(A_HEADS)[:, None, None] * kidx[None, None, :], F32)
    bias_d = jnp.asarray(_alibi(D_HEADS)[:, None, None] * kidx[None, None, :], F32)
    bias_b = _swa_bias()

    for l in range(depth):
        wb, wf = _prep_w_in(w_in[l])
        pb = _mm(hb, wb, tm, N_BF // 4, BF16, "proj_bf16")
        pf = _mm(hb, wf, tm, N_F32 // 2, F32, "proj_f32")

        o_a = _attn_a(pf, pb, tabs_idx, bias_a, g_grp[l, 0], batch, p_len, tq, k_top, {**COL_F32, **COL_BF})
        sink_rows = jnp.repeat(sinks[l].astype(F32).reshape(B_KV_HEADS, grp), BLK, axis=1)[..., None]
        o_b = _attn_b(pb, bias_b, sink_rows, g_grp[l, 1], batch, p_len, COL_BF)
        qc, kvc, krc = _c_prep(pf, tabs_c, g_cq[l], g_ckv[l], _prep_w_uq(w_uq[l]), _prep_w_ukv(w_ukv[l]),
                               p_len, tm_ln, COL_F32)
        o_c = _attn_c(qc, kvc, krc, g_grp[l, 2], batch, p_len, tq)
        lam_init = 0.8 - 0.6 * math.exp(-0.3 * l)
        lam_p = jnp.stack([lam_q1[l], lam_k1[l], lam_q2[l], lam_k2[l]]).astype(F32)
        o_d = _attn_d(pb, bias_d, lam_p, g_diff[l], batch, p_len, tq, lam_init, COL_BF)

        mix_in = jnp.concatenate([o_a, o_b, o_c, o_d], axis=1)
        mix = _mm(mix_in, w_out[l].astype(BF16), tm, min(1024, d), F32, "out_proj")
        h, hb_ffn = _res_ln(h, mix, ln1_g[l], ln1_b[l], alpha, tm_ln, p_len, True)

        act = _ffn_up(hb_ffn, w_up[l], conv_w[l], conv_b[l], p_len, tm, FF_TILE)
        ffn = _mm(act, w_down[l].astype(BF16), tm, min(512, d), F32, "ffn_down")
        h, hb = _res_ln(h, ffn, ln2_g[l], ln2_b[l], alpha, tm_ln, p_len, False)

    return h.reshape(batch, p_len, d)[:, BLK:]
```

```python
import functools
import math

import jax
import jax.numpy as jnp
import numpy as np
from jax import lax
from jax.experimental import pallas as pl
from jax.experimental.pallas import tpu as pltpu

N_META = 16
BLK = 128
PAD_FRONT = BLK - N_META
A_HEADS, A_HDIM = 8, 128
IDX_HEADS, IDX_HDIM, IDX_ROPE = 16, 64, 32
TOPK_MAX = 256
B_HEADS, B_KV_HEADS, B_HDIM = 16, 2, 64
WINDOW = 128
C_HEADS, C_Q_RANK, C_KV_RANK, C_NOPE, C_ROPE, C_VDIM = 8, 1024, 512, 128, 64, 128
D_HEADS, D_HDIM = 8, 64
GROUP_WIDTH = 1024
CONV_W = 3
ROPE_BASE = 10000.0
NEG = -1e30
IN_SIZES = (1024, 128, 128, 1024, 64, 16, 1024, 128, 128, 1024, 512, 64, 1024, 1024, 1024)

LANES = 128
BF16_SUBLANES = 16
VMEM_LIMIT = 56 * 1024 * 1024
FF_TILE = 256
N_EXTENT_VARIANTS = 6
N_EXTENT_VARIANTS_D = 3

F32 = jnp.float32
BF16 = jnp.bfloat16


def _cparams(n_axes):
    return pltpu.CompilerParams(dimension_semantics=("arbitrary",) * n_axes, vmem_limit_bytes=VMEM_LIMIT)


def _row_tile(p_len, target):
    best = BF16_SUBLANES
    for t in range(BF16_SUBLANES, target + 1, BF16_SUBLANES):
        if p_len % t == 0:
            best = t
    return best


def _dot(a, b):
    return jnp.dot(a, b, preferred_element_type=F32)


def _dot_nt(a, b):
    return lax.dot_general(a, b, (((1,), (1,)), ((), ())), preferred_element_type=F32)


def _ln_rows(x, g, b):
    mu = jnp.mean(x, axis=-1, keepdims=True)
    xc = x - mu
    var = jnp.mean(xc * xc, axis=-1, keepdims=True)
    return xc * lax.rsqrt(var + 1e-5) * g + b


def _ln_in_kernel(x_ref, meta_ref, g_ref, b_ref, h_ref, hb_ref):
    def emit(rows):
        y = _ln_rows(rows, g_ref[...], b_ref[...])
        h_ref[...] = y
        hb_ref[...] = y.astype(BF16)

    @pl.when(pl.program_id(1) == 0)
    def _():
        emit(jnp.concatenate([jnp.zeros((PAD_FRONT, meta_ref.shape[1]), F32), meta_ref[...]], axis=0))

    @pl.when(pl.program_id(1) > 0)
    def _():
        emit(x_ref[...])


def _ln_in(x, meta, g, b):
    batch, s_len, d = x.shape
    nb = s_len // BLK + 1
    t = batch * nb * BLK
    return pl.pallas_call(
        _ln_in_kernel,
        grid=(batch, nb),
        in_specs=[pl.BlockSpec((None, BLK, d), lambda bi, n: (bi, jnp.maximum(n - 1, 0), 0)),
                  pl.BlockSpec((N_META, d), lambda bi, n: (0, 0)),
                  pl.BlockSpec((1, d), lambda bi, n: (0, 0)),
                  pl.BlockSpec((1, d), lambda bi, n: (0, 0))],
        out_specs=[pl.BlockSpec((BLK, d), lambda bi, n: (bi * nb + n, 0)),
                   pl.BlockSpec((BLK, d), lambda bi, n: (bi * nb + n, 0))],
        out_shape=[jax.ShapeDtypeStruct((t, d), F32), jax.ShapeDtypeStruct((t, d), BF16)],
        compiler_params=_cparams(2),
        name="ln_in",
    )(x, meta.astype(F32), g.reshape(1, d), b.reshape(1, d))


def _res_ln_kernel(h_ref, y_ref, g_ref, b_ref, o_ref, ob_ref, *, alpha, tm, tiles_per_seq, zero_pad):
    out = _ln_rows(alpha * h_ref[...] + y_ref[...], g_ref[...], b_ref[...])
    o_ref[...] = out
    if zero_pad:
        p0 = (pl.program_id(0) % tiles_per_seq) * tm
        pos = p0 + lax.broadcasted_iota(jnp.int32, out.shape, 0)
        out = jnp.where(pos >= PAD_FRONT, out, 0.0)
    ob_ref[...] = out.astype(BF16)


def _res_ln(h, y, g, b, alpha, tm, p_len, zero_pad):
    t, d = h.shape
    kern = functools.partial(_res_ln_kernel, alpha=alpha, tm=tm, tiles_per_seq=p_len // tm, zero_pad=zero_pad)
    return pl.pallas_call(
        kern,
        grid=(t // tm,),
        in_specs=[pl.BlockSpec((tm, d), lambda i: (i, 0)),
                  pl.BlockSpec((tm, d), lambda i: (i, 0)),
                  pl.BlockSpec((1, d), lambda i: (0, 0)),
                  pl.BlockSpec((1, d), lambda i: (0, 0))],
        out_specs=[pl.BlockSpec((tm, d), lambda i: (i, 0)),
                   pl.BlockSpec((tm, d), lambda i: (i, 0))],
        out_shape=[jax.ShapeDtypeStruct((t, d), F32), jax.ShapeDtypeStruct((t, d), BF16)],
        compiler_params=_cparams(1),
        name="res_ln",
    )(h, y, g.reshape(1, d), b.reshape(1, d))


def _res_ln_final_kernel(h_ref, y_ref, g_ref, b_ref, o_ref, *, alpha):
    o_ref[...] = _ln_rows(alpha * h_ref[...] + y_ref[...], g_ref[...], b_ref[...])


def _res_ln_final(h, y, g, b, alpha, batch, p_len):
    t, d = h.shape
    nb = p_len // BLK
    out_idx = lambda i: ((i // nb) * (nb - 1) + jnp.maximum(i % nb - 1, 0), 0)
    return pl.pallas_call(
        functools.partial(_res_ln_final_kernel, alpha=alpha),
        grid=(t // BLK,),
        in_specs=[pl.BlockSpec((BLK, d), lambda i: (i, 0)),
                  pl.BlockSpec((BLK, d), lambda i: (i, 0)),
                  pl.BlockSpec((1, d), lambda i: (0, 0)),
                  pl.BlockSpec((1, d), lambda i: (0, 0))],
        out_specs=pl.BlockSpec((BLK, d), out_idx),
        out_shape=jax.ShapeDtypeStruct((batch * (p_len - BLK), d), F32),
        compiler_params=_cparams(1),
        name="res_ln_final",
    )(h, y, g.reshape(1, d), b.reshape(1, d))


def _mm_kernel(a_ref, w_ref, o_ref):
    o_ref[...] = _dot(a_ref[...], w_ref[...]).astype(o_ref.dtype)


def _mm(a, w, tm, tn, out_dtype, name):
    t, k = a.shape
    n = w.shape[1]
    return pl.pallas_call(
        _mm_kernel,
        grid=(n // tn, t // tm),
        in_specs=[pl.BlockSpec((tm, k), lambda j, i: (i, 0)),
                  pl.BlockSpec((k, tn), lambda j, i: (0, j))],
        out_specs=pl.BlockSpec((tm, tn), lambda j, i: (i, j)),
        out_shape=jax.ShapeDtypeStruct((t, n), out_dtype),
        compiler_params=_cparams(2),
        name=name,
    )(a, w)


def _rope128(x, c, s1, s2, half):
    return x * c + pltpu.roll(x, LANES - half, 1) * s1 + pltpu.roll(x, half, 1) * s2


def _rope_tables(pos, rot_dim, period):
    half = rot_dim // 2
    inv = ROPE_BASE ** (-jnp.arange(half, dtype=F32) / half)
    ang = pos.astype(F32)[:, None] * inv[None]
    cos, sin = jnp.cos(ang), jnp.sin(ang)
    n = pos.shape[0]
    lane = np.arange(LANES) % period
    li = lane % half
    cos_l, sin_l = cos[:, li], sin[:, li]
    first = jnp.asarray(lane < half)[None]
    second = jnp.asarray((lane >= half) & (lane < rot_dim))[None]
    c = jnp.where(first | second, cos_l, 1.0)
    s1 = jnp.where(first, -sin_l, 0.0)
    s2 = jnp.where(second, sin_l, 0.0)
    return jnp.stack([c, s1, s2]).astype(F32).reshape(3, n, LANES)


def _rms(x, g, eps=1e-6):
    return x * lax.rsqrt(jnp.mean(x * x, axis=-1, keepdims=True) + eps) * g


def _key_extents(nblk, nvar):
    step = -(-nblk // nvar)
    return [(lo, min(lo + step, nblk), min(lo + step, nblk)) for lo in range(0, nblk, step)]


def _for_each_extent(qi, nblk, body, nvar=N_EXTENT_VARIANTS):
    for lo, hi, eb in _key_extents(nblk, nvar):
        @pl.when(jnp.logical_and(qi >= lo, qi < hi))
        def _():
            body(eb * BLK)


def _attn_a_kernel(iq_ref, wq_ref, ikw_ref, tq_ref, tk_ref, aq_ref, ak_ref, av_ref, bias_ref, g_ref,
                   o_ref, key_ref, mask_ref, qs_ref, os_ref, *, tq, p_len, k_top):
    qi = pl.program_id(1)
    q0 = qi * tq
    half = IDX_ROPE // 2
    for h in range(A_HEADS):
        qs_ref[h * tq:(h + 1) * tq, :] = aq_ref[:, h * A_HDIM:(h + 1) * A_HDIM]

    def body(ext):
        ik = _rope128(ikw_ref[:ext, :], tk_ref[0, :ext, :], tk_ref[1, :ext, :], tk_ref[2, :ext, :],
                      half)[:, :IDX_HDIM].astype(BF16)
        w = wq_ref[...][:, IDX_HDIM:IDX_HDIM + IDX_HEADS] * ((IDX_HEADS * IDX_HDIM) ** -0.5)
        tc, ts1, ts2 = tq_ref[0], tq_ref[1], tq_ref[2]
        score = jnp.zeros((tq, ext), F32)
        for c in range(IDX_HEADS * IDX_HDIM // LANES):
            chunk = _rope128(iq_ref[:, c * LANES:(c + 1) * LANES], tc, ts1, ts2, half).astype(BF16)
            for j in range(LANES // IDX_HDIM):
                h = c * (LANES // IDX_HDIM) + j
                rel = jnp.maximum(_dot_nt(chunk[:, j * IDX_HDIM:(j + 1) * IDX_HDIM], ik), 0.0)
                score = score + w[:, h:h + 1] * rel
        qpos = q0 + lax.broadcasted_iota(jnp.int32, (tq, ext), 0)
        kpos = lax.broadcasted_iota(jnp.int32, (tq, ext), 1)
        vis = jnp.where(kpos <= qpos, jnp.where(kpos >= PAD_FRONT, 1.0, 0.0), 0.0)
        score = jnp.where(vis > 0.0, score, NEG)
        bits = lax.bitcast_convert_type(score, jnp.int32)
        key_ref[:, :ext] = jnp.where(bits < 0, bits ^ jnp.int32(0x7FFFFFFF), bits)
        kf = jnp.float32(k_top)

        def count(pred):
            return jnp.sum(jnp.where(pred, 1.0, 0.0), axis=1, keepdims=True)

        def vbody(i, lo):
            cand = lo + jnp.left_shift(jnp.int32(1), 31 - i)
            return jnp.where(count(key_ref[:, :ext] >= cand) >= kf, cand, lo)

        thr = lax.fori_loop(0, 32, vbody, jnp.full((tq, 1), -2 ** 31, jnp.int32))
        key = key_ref[:, :ext]
        n_gt = count(key > thr)
        n_eq = count(key == thr)
        need = kf - n_gt
        n_eq_vis = jnp.sum(jnp.where(key == thr, vis, 0.0), axis=1, keepdims=True)
        mask_ref[:, :ext] = jnp.where(key >= thr, jnp.where(vis > 0.0, 0.0, NEG), NEG)
        surplus = jnp.where(n_eq_vis > 0.0, jnp.where(n_eq > need, 1.0, 0.0), 0.0)

        @pl.when(jnp.max(surplus) > 0.0)
        def _():
            nbits = max(1, (ext - 1).bit_length())
            keyv = key_ref[:, :ext]
            col = lax.broadcasted_iota(jnp.int32, (tq, ext), 1)

            def tbody(i, j):
                cand = j + jnp.left_shift(jnp.int32(1), nbits - 1 - i)
                f = jnp.sum(jnp.where(keyv == thr, jnp.where(col < cand, 1.0, 0.0), 0.0), axis=1, keepdims=True)
                return jnp.where(f < need, cand, j)

            jstar = lax.fori_loop(0, nbits, tbody, jnp.zeros((tq, 1), jnp.int32))
            chosen = jnp.where(keyv > thr, 1.0, jnp.where(keyv == thr, jnp.where(col <= jstar, 1.0, 0.0), 0.0))
            mask_ref[:, :ext] = jnp.where(chosen > 0.0, jnp.where(vis > 0.0, 0.0, NEG), NEG)

        scale = A_HDIM ** -0.5

        def hbody(h, carry):
            r0 = pl.multiple_of(h * tq, tq)
            s = _dot_nt(qs_ref[pl.ds(r0, tq), :], ak_ref[:ext, :]) * scale + bias_ref[h, :, :ext] + mask_ref[:, :ext]
            m = jnp.max(s, axis=1, keepdims=True)
            e = jnp.exp(s - m)
            l = jnp.sum(e, axis=1, keepdims=True)
            os_ref[pl.ds(r0, tq), :] = _dot(e.astype(BF16), av_ref[:ext, :]) / l
            return carry

        lax.fori_loop(0, A_HEADS, hbody, 0)

    _for_each_extent(qi, p_len // BLK, body)
    o = jnp.concatenate([os_ref[h * tq:(h + 1) * tq, :] for h in range(A_HEADS)], axis=1)
    o_ref[...] = _rms(o, g_ref[...]).astype(BF16)


def _attn_a(pf, pb, tabs_idx, bias, g, batch, p_len, tq, k_top, col):
    nq = p_len // tq
    t = batch * p_len
    kern = functools.partial(_attn_a_kernel, tq=tq, p_len=p_len, k_top=k_top)
    return pl.pallas_call(
        kern,
        grid=(batch, nq),
        in_specs=[
            pl.BlockSpec((tq, 1024), lambda b, q: (b * nq + q, col["i_q"] // 1024)),
            pl.BlockSpec((tq, LANES), lambda b, q: (b * nq + q, col["ikw"] // LANES)),
            pl.BlockSpec((p_len, LANES), lambda b, q: (b, col["ikw"] // LANES)),
            pl.BlockSpec((3, tq, LANES), lambda b, q: (0, q, 0)),
            pl.BlockSpec((3, p_len, LANES), lambda b, q: (0, 0, 0)),
            pl.BlockSpec((tq, 1024), lambda b, q: (b * nq + q, col["a_q"] // 1024)),
            pl.BlockSpec((p_len, LANES), lambda b, q: (b, col["a_k"] // LANES)),
            pl.BlockSpec((p_len, LANES), lambda b, q: (b, col["a_v"] // LANES)),
            pl.BlockSpec((A_HEADS, 1, p_len), lambda b, q: (0, 0, 0)),
            pl.BlockSpec((1, GROUP_WIDTH), lambda b, q: (0, 0)),
        ],
        out_specs=pl.BlockSpec((tq, GROUP_WIDTH), lambda b, q: (b * nq + q, 0)),
        out_shape=jax.ShapeDtypeStruct((t, GROUP_WIDTH), BF16),
        scratch_shapes=[pltpu.VMEM((tq, p_len), jnp.int32),
                        pltpu.VMEM((tq, p_len), F32),
                        pltpu.VMEM((A_HEADS * tq, A_HDIM), BF16),
                        pltpu.VMEM((A_HEADS * tq, A_HDIM), F32)],
        compiler_params=_cparams(2),
        name="attn_a",
    )(pf, pf, pf, tabs_idx, tabs_idx, pb, pb, pb, bias, g.reshape(1, GROUP_WIDTH))


def _attn_b_kernel(q_ref, kp_ref, kc_ref, vp_ref, vc_ref, bias_ref, sink_ref, g_ref, o_ref):
    n = pl.program_id(1)
    grp = B_HEADS // B_KV_HEADS
    first_col = PAD_FRONT - (n - 1) * BLK
    col = lax.broadcasted_iota(jnp.int32, (grp * BLK, 2 * BLK), 1)
    pad_mask = jnp.where(col >= first_col, 0.0, NEG)
    outs = []
    for gi in range(B_KV_HEADS):
        q = jnp.concatenate([q_ref[:, (gi * grp + j) * B_HDIM:(gi * grp + j + 1) * B_HDIM] for j in range(grp)], axis=0)
        k = jnp.concatenate([kp_ref[:, gi * B_HDIM:(gi + 1) * B_HDIM], kc_ref[:, gi * B_HDIM:(gi + 1) * B_HDIM]], axis=0)
        v = jnp.concatenate([vp_ref[:, gi * B_HDIM:(gi + 1) * B_HDIM], vc_ref[:, gi * B_HDIM:(gi + 1) * B_HDIM]], axis=0)
        s = _dot_nt(q, k) + bias_ref[gi] + pad_mask
        sink = sink_ref[gi]
        m = jnp.maximum(jnp.max(s, axis=1, keepdims=True), sink)
        e = jnp.exp(s - m)
        l = jnp.sum(e, axis=1, keepdims=True) + jnp.exp(sink - m)
        o = _dot(e.astype(BF16), v) / l
        outs.extend(o[j * BLK:(j + 1) * BLK, :] for j in range(grp))
    o = jnp.concatenate(outs, axis=1)
    o_ref[...] = _rms(o, g_ref[...]).astype(BF16)


def _attn_b(pb, bias, sink_rows, g, batch, p_len, col):
    nb = p_len // BLK
    t = batch * p_len
    grp = B_HEADS // B_KV_HEADS
    return pl.pallas_call(
        _attn_b_kernel,
        grid=(batch, nb),
        in_specs=[
            pl.BlockSpec((BLK, 1024), lambda b, n: (b * nb + n, col["b_q"] // 1024)),
            pl.BlockSpec((BLK, LANES), lambda b, n: (b * nb + jnp.maximum(n - 1, 0), col["b_k"] // LANES)),
            pl.BlockSpec((BLK, LANES), lambda b, n: (b * nb + n, col["b_k"] // LANES)),
            pl.BlockSpec((BLK, LANES), lambda b, n: (b * nb + jnp.maximum(n - 1, 0), col["b_v"] // LANES)),
            pl.BlockSpec((BLK, LANES), lambda b, n: (b * nb + n, col["b_v"] // LANES)),
            pl.BlockSpec((B_KV_HEADS, grp * BLK, 2 * BLK), lambda b, n: (0, 0, 0)),
            pl.BlockSpec((B_KV_HEADS, grp * BLK, 1), lambda b, n: (0, 0, 0)),
            pl.BlockSpec((1, GROUP_WIDTH), lambda b, n: (0, 0)),
        ],
        out_specs=pl.BlockSpec((BLK, GROUP_WIDTH), lambda b, n: (b * nb + n, 0)),
        out_shape=jax.ShapeDtypeStruct((t, GROUP_WIDTH), BF16),
        compiler_params=_cparams(2),
        name="attn_b",
    )(pb, pb, pb, pb, pb, bias, sink_rows, g.reshape(1, GROUP_WIDTH))


def _c_prep_kernel(cq_ref, ckv_ref, kr_ref, tab_ref, gq_ref, gkv_ref, wq_ref, wkv_ref, q_ref, kv_ref, kro_ref):
    half = C_ROPE // 2
    tc, ts1, ts2 = tab_ref[0], tab_ref[1], tab_ref[2]
    xq = _rms(cq_ref[...], gq_ref[...]).astype(BF16)
    q = _dot(xq, wq_ref[...])
    nope_w = C_HEADS * C_NOPE
    q_ref[:, :nope_w] = q[:, :nope_w].astype(BF16)
    for h in range(C_HEADS):
        lo = nope_w + h * LANES
        q_ref[:, lo:lo + LANES] = _rope128(q[:, lo:lo + LANES], tc, ts1, ts2, half).astype(BF16)
    xkv = _rms(ckv_ref[...], gkv_ref[...]).astype(BF16)
    kv_ref[...] = _dot(xkv, wkv_ref[...]).astype(BF16)
    kro_ref[...] = _rope128(kr_ref[...], tc, ts1, ts2, half).astype(BF16)


def _c_prep(pf, tabs, g_cq, g_ckv, w_uq_p, w_ukv_p, p_len, tm, col):
    t = pf.shape[0]
    tiles = p_len // tm
    wide = 2 * C_HEADS * LANES
    return pl.pallas_call(
        _c_prep_kernel,
        grid=(t // tm,),
        in_specs=[
            pl.BlockSpec((tm, C_Q_RANK), lambda i: (i, col["c_cq"] // C_Q_RANK)),
            pl.BlockSpec((tm, C_KV_RANK), lambda i: (i, col["c_ckv"] // C_KV_RANK)),
            pl.BlockSpec((tm, LANES), lambda i: (i, col["c_kr"] // LANES)),
            pl.BlockSpec((3, tm, LANES), lambda i: (0, i % tiles, 0)),
            pl.BlockSpec((1, C_Q_RANK), lambda i: (0, 0)),
            pl.BlockSpec((1, C_KV_RANK), lambda i: (0, 0)),
            pl.BlockSpec((C_Q_RANK, wide), lambda i: (0, 0)),
            pl.BlockSpec((C_KV_RANK, wide), lambda i: (0, 0)),
        ],
        out_specs=[pl.BlockSpec((tm, wide), lambda i: (i, 0)),
                   pl.BlockSpec((tm, wide), lambda i: (i, 0)),
                   pl.BlockSpec((tm, LANES), lambda i: (i, 0))],
        out_shape=[jax.ShapeDtypeStruct((t, wide), BF16),
                   jax.ShapeDtypeStruct((t, wide), BF16),
                   jax.ShapeDtypeStruct((t, LANES), BF16)],
        compiler_params=_cparams(1),
        name="c_prep",
    )(pf, pf, pf, tabs, g_cq.reshape(1, -1), g_ckv.reshape(1, -1), w_uq_p, w_ukv_p)


def _causal_mask(q0, tq, ext):
    qpos = q0 + lax.broadcasted_iota(jnp.int32, (tq, ext), 0)
    kpos = lax.broadcasted_iota(jnp.int32, (tq, ext), 1)
    return jnp.where(kpos <= qpos, jnp.where(kpos >= PAD_FRONT, 0.0, NEG), NEG)


def _attn_c_kernel(q_ref, kv_ref, kr_ref, g_ref, o_ref, os_ref, *, tq, p_len):
    qi = pl.program_id(1)
    scale = (C_NOPE + C_ROPE) ** -0.5
    nope_w = C_HEADS * C_NOPE

    def body(ext):
        mask = _causal_mask(qi * tq, tq, ext)
        kr = kr_ref[:ext, :]
        for h in range(C_HEADS):
            s = _dot_nt(q_ref[:, h * C_NOPE:(h + 1) * C_NOPE], kv_ref[:ext, h * C_NOPE:(h + 1) * C_NOPE])
            s = s + _dot_nt(q_ref[:, nope_w + h * LANES:nope_w + (h + 1) * LANES], kr)
            s = s * scale + mask
            m = jnp.max(s, axis=1, keepdims=True)
            e = jnp.exp(s - m)
            l = jnp.sum(e, axis=1, keepdims=True)
            v = kv_ref[:ext, nope_w + h * C_VDIM:nope_w + (h + 1) * C_VDIM]
            os_ref[:, h * C_VDIM:(h + 1) * C_VDIM] = _dot(e.astype(BF16), v) / l

    _for_each_extent(qi, p_len // BLK, body)
    o_ref[...] = _rms(os_ref[...], g_ref[...]).astype(BF16)


def _attn_c(qc, kvc, krc, g, batch, p_len, tq):
    nq = p_len // tq
    t = batch * p_len
    wide = 2 * C_HEADS * LANES
    kern = functools.partial(_attn_c_kernel, tq=tq, p_len=p_len)
    return pl.pallas_call(
        kern,
        grid=(batch, nq),
        in_specs=[pl.BlockSpec((tq, wide), lambda b, q: (b * nq + q, 0)),
                  pl.BlockSpec((p_len, wide), lambda b, q: (b, 0)),
                  pl.BlockSpec((p_len, LANES), lambda b, q: (b, 0)),
                  pl.BlockSpec((1, GROUP_WIDTH), lambda b, q: (0, 0))],
        out_specs=pl.BlockSpec((tq, GROUP_WIDTH), lambda b, q: (b * nq + q, 0)),
        out_shape=jax.ShapeDtypeStruct((t, GROUP_WIDTH), BF16),
        scratch_shapes=[pltpu.VMEM((tq, GROUP_WIDTH), F32)],
        compiler_params=_cparams(2),
        name="attn_c",
    )(qc, kvc, krc, g.reshape(1, GROUP_WIDTH))


def _attn_d_kernel(q_ref, k_ref, v_ref, bias_ref, lam_ref, g_ref, o_ref, os_ref, *, tq, p_len, lam_init):
    qi = pl.program_id(1)
    lp = lam_ref[...]
    lam = (jnp.exp(jnp.sum(lp[0:1] * lp[1:2], axis=1, keepdims=True))
           - jnp.exp(jnp.sum(lp[2:3] * lp[3:4], axis=1, keepdims=True)) + lam_init)
    vw = 2 * D_HDIM

    def body(ext):
        mask = _causal_mask(qi * tq, tq, ext)
        for h in range(D_HEADS):
            bm = bias_ref[h, :, :ext] + mask
            es, ls = [], []
            for c in range(2):
                lo = (2 * h + c) * D_HDIM
                s = _dot_nt(q_ref[:, lo:lo + D_HDIM], k_ref[:ext, lo:lo + D_HDIM]) + bm
                m = jnp.max(s, axis=1, keepdims=True)
                e = jnp.exp(s - m)
                es.append(e)
                ls.append(jnp.sum(e, axis=1, keepdims=True))
            a = es[0] * (1.0 / ls[0]) - es[1] * (lam / ls[1])
            os_ref[:, h * vw:(h + 1) * vw] = _dot(a.astype(BF16), v_ref[:ext, h * vw:(h + 1) * vw])

    _for_each_extent(qi, p_len // BLK, body, N_EXTENT_VARIANTS_D)
    g = g_ref[...]
    outs = [_rms(os_ref[:, h * vw:(h + 1) * vw], g) * (1.0 - lam_init) for h in range(D_HEADS)]
    o_ref[...] = jnp.concatenate(outs, axis=1).astype(BF16)


def _attn_d(pb, bias, lam_p, g, batch, p_len, tq, lam_init, col):
    nq = p_len // tq
    t = batch * p_len
    kern = functools.partial(_attn_d_kernel, tq=tq, p_len=p_len, lam_init=lam_init)
    return pl.pallas_call(
        kern,
        grid=(batch, nq),
        in_specs=[pl.BlockSpec((tq, 1024), lambda b, q: (b * nq + q, col["d_q"] // 1024)),
                  pl.BlockSpec((p_len, 1024), lambda b, q: (b, col["d_k"] // 1024)),
                  pl.BlockSpec((p_len, 1024), lambda b, q: (b, col["d_v"] // 1024)),
                  pl.BlockSpec((D_HEADS, 1, p_len), lambda b, q: (0, 0, 0)),
                  pl.BlockSpec((4, D_HDIM), lambda b, q: (0, 0)),
                  pl.BlockSpec((1, 2 * D_HDIM), lambda b, q: (0, 0))],
        out_specs=pl.BlockSpec((tq, GROUP_WIDTH), lambda b, q: (b * nq + q, 0)),
        out_shape=jax.ShapeDtypeStruct((t, GROUP_WIDTH), BF16),
        scratch_shapes=[pltpu.VMEM((tq, GROUP_WIDTH), F32)],
        compiler_params=_cparams(2),
        name="attn_d",
    )(pb, pb, pb, bias, lam_p, g.reshape(1, 2 * D_HDIM))


def _ffn_up_kernel(x_ref, wg_ref, wu_ref, cwg_ref, cwu_ref, cbg_ref, cbu_ref, o_ref,
                   wgb_ref, wub_ref, hprev_ref, carry_ref, *, tm, n_row_tiles, tiles_per_seq):
    s = pl.program_id(0)

    @pl.when(s % n_row_tiles == 0)
    def _():
        wgb_ref[...] = wg_ref[...].astype(BF16)
        wub_ref[...] = wu_ref[...].astype(BF16)

    @pl.when(s == 0)
    def _():
        hprev_ref[...] = jnp.zeros_like(hprev_ref)

    @pl.when((s + tiles_per_seq - 1) % tiles_per_seq == 0)
    def _():
        carry_ref[...] = jnp.zeros_like(carry_ref)

    x = x_ref[...]

    def branch(b, wb_ref, cw_ref, cb_ref):
        prev = hprev_ref[b]
        full = jnp.concatenate([carry_ref[b], prev], axis=0)
        carry_ref[b] = prev[tm - 8:, :]
        hprev_ref[b] = _dot(x, wb_ref[...])
        cw = cw_ref[...]
        y = cw[2:3] * prev + cw[1:2] * pltpu.roll(full, 1, 0)[8:] + cw[0:1] * pltpu.roll(full, 2, 0)[8:]
        return y + cb_ref[...]

    gate = branch(0, wgb_ref, cwg_ref, cbg_ref)
    up = branch(1, wub_ref, cwu_ref, cbu_ref)
    o_ref[...] = (gate * jax.nn.sigmoid(gate) * up).astype(BF16)


def _ffn_up(hb, w_up, conv_w, conv_b, layer, p_len, tm, tn):
    t, d = hb.shape
    f = w_up.shape[2] // 2
    assert f % tn == 0 and tm % 8 == 0 and p_len % tm == 0
    nj, nt = f // tn, t // tm
    n_tiles = nj * nt
    kern = functools.partial(_ffn_up_kernel, tm=tm, n_row_tiles=nt, tiles_per_seq=p_len // tm)
    cb = conv_b.reshape(conv_b.shape[0], 1, 2 * f)
    cur = lambda s: jnp.minimum(s, n_tiles - 1)
    prv = lambda s: jnp.maximum(s - 1, 0)
    return pl.pallas_call(
        kern,
        grid=(n_tiles + 1,),
        in_specs=[pl.BlockSpec((tm, d), lambda s: (cur(s) % nt, 0)),
                  pl.BlockSpec((None, d, tn), lambda s: (layer, 0, cur(s) // nt)),
                  pl.BlockSpec((None, d, tn), lambda s: (layer, 0, nj + cur(s) // nt)),
                  pl.BlockSpec((None, CONV_W, tn), lambda s: (layer, 0, prv(s) // nt)),
                  pl.BlockSpec((None, CONV_W, tn), lambda s: (layer, 0, nj + prv(s) // nt)),
                  pl.BlockSpec((None, 1, tn), lambda s: (layer, 0, prv(s) // nt)),
                  pl.BlockSpec((None, 1, tn), lambda s: (layer, 0, nj + prv(s) // nt))],
        out_specs=pl.BlockSpec((tm, tn), lambda s: (prv(s) % nt, prv(s) // nt)),
        out_shape=jax.ShapeDtypeStruct((t, f), BF16),
        scratch_shapes=[pltpu.VMEM((d, tn), BF16), pltpu.VMEM((d, tn), BF16),
                        pltpu.VMEM((2, tm, tn), F32), pltpu.VMEM((2, 8, tn), F32)],
        compiler_params=_cparams(1),
        name="ffn_up",
    )(hb, w_up, w_up, conv_w, conv_w, cb, cb)


def _offsets(names, sizes):
    out, o = {}, 0
    for n, s in zip(names, sizes):
        out[n] = o
        o += s
    return out, o


_BF_NAMES = ("a_q", "b_q", "d_q", "d_k", "d_v", "a_k", "a_v", "b_k", "b_v")
_BF_SIZES = (1024, 1024, 1024, 1024, 1024, 128, 128, 128, 128)
_F32_NAMES = ("i_q", "c_cq", "c_ckv", "ikw", "c_kr")
_F32_SIZES = (1024, 1024, 512, 128, 128)
COL_BF, N_BF = _offsets(_BF_NAMES, _BF_SIZES)
COL_F32, N_F32 = _offsets(_F32_NAMES, _F32_SIZES)


def _prep_w_in(w):
    d = w.shape[0]
    seg = dict(zip(("a_q", "a_k", "a_v", "i_q", "i_k", "i_w", "b_q", "b_k", "b_v", "c_cq", "c_ckv", "c_kr",
                    "d_q", "d_k", "d_v"), jnp.split(w, np.cumsum(IN_SIZES)[:-1].tolist(), axis=1)))
    seg["b_q"] = seg["b_q"] * (B_HDIM ** -0.5)
    seg["d_q"] = seg["d_q"] * (D_HDIM ** -0.5)
    wb = jnp.concatenate([seg[n] for n in _BF_NAMES], axis=1).astype(BF16)
    zeros = lambda n: jnp.zeros((d, n), w.dtype)
    ikw = jnp.concatenate([seg["i_k"], seg["i_w"], zeros(LANES - IDX_HDIM - IDX_HEADS)], axis=1)
    ckr = jnp.concatenate([seg["c_kr"], zeros(LANES - C_ROPE)], axis=1)
    wf = jnp.concatenate([seg["i_q"], seg["c_cq"], seg["c_ckv"], ikw, ckr], axis=1).astype(BF16)
    return wb, wf


def _prep_w_uq(w):
    r = w.shape[0]
    w3 = w.reshape(r, C_HEADS, C_NOPE + C_ROPE)
    nope = w3[:, :, :C_NOPE].reshape(r, C_HEADS * C_NOPE)
    rope = jnp.pad(w3[:, :, C_NOPE:], ((0, 0), (0, 0), (0, LANES - C_ROPE))).reshape(r, C_HEADS * LANES)
    return jnp.concatenate([nope, rope], axis=1).astype(BF16)


def _prep_w_ukv(w):
    r = w.shape[0]
    w3 = w.reshape(r, C_HEADS, C_NOPE + C_VDIM)
    return jnp.concatenate([w3[:, :, :C_NOPE].reshape(r, -1), w3[:, :, C_NOPE:].reshape(r, -1)], axis=1).astype(BF16)


def _alibi(n):
    return 2.0 ** (-8.0 * np.arange(1, n + 1, dtype=np.float64) / n)


def _swa_bias():
    grp = B_HEADS // B_KV_HEADS
    r = np.arange(BLK)[:, None]
    c = np.arange(2 * BLK)[None, :]
    diff = (r + BLK - c).astype(np.float64)
    ok = (diff >= 0) & (diff < WINDOW)
    slopes = _alibi(B_HEADS).reshape(B_KV_HEADS, grp)
    bias = np.where(ok[None, None], -slopes[:, :, None, None] * diff[None, None], NEG)
    return jnp.asarray(bias.reshape(B_KV_HEADS, grp * BLK, 2 * BLK), F32)


def kernel(x, meta_tokens, ln_in_g, ln_in_b, w_in, g_cq, g_ckv, w_uq, w_ukv, sinks, lam_q1, lam_k1, lam_q2, lam_k2,
           g_diff, g_grp, w_out, ln1_g, ln1_b, w_up, conv_w, conv_b, w_down, ln2_g, ln2_b):
    batch, s_len, d = x.shape
    depth = w_in.shape[0]
    p_len = s_len + BLK
    t = batch * p_len
    k_top = min(TOPK_MAX, s_len // 4)
    alpha = (2 * depth) ** 0.25
    tm = _row_tile(p_len, 544)
    tm_ln = _row_tile(p_len, 272)
    tm_ffn = _row_tile(p_len, 1088)
    tq = BLK
    grp = B_HEADS // B_KV_HEADS

    h, hb = _ln_in(x, meta_tokens, ln_in_g, ln_in_b)

    pos = jnp.arange(p_len, dtype=jnp.int32) - PAD_FRONT
    tabs_idx = _rope_tables(pos, IDX_ROPE, IDX_HDIM)
    tabs_c = _rope_tables(pos, C_ROPE, LANES)
    kidx = np.arange(p_len, dtype=np.float64)
    bias_a = jnp.asarray(_alibi(A_HEADS)[:, None, None] * kidx[None, None, :], F32)
    bias_d = jnp.asarray(_alibi(D_HEADS)[:, None, None] * kidx[None, None, :], F32)
    bias_b = _swa_bias()

    for l in range(depth):
        wb, wf = _prep_w_in(w_in[l])
        pb = _mm(hb, wb, tm, N_BF // 4, BF16, "proj_bf16")
        pf = _mm(hb, wf, tm, N_F32 // 2, F32, "proj_f32")

        o_a = _attn_a(pf, pb, tabs_idx, bias_a, g_grp[l, 0], batch, p_len, tq, k_top, {**COL_F32, **COL_BF})
        sink_rows = jnp.repeat(sinks[l].astype(F32).reshape(B_KV_HEADS, grp), BLK, axis=1)[..., None]
        o_b = _attn_b(pb, bias_b, sink_rows, g_grp[l, 1], batch, p_len, COL_BF)
        qc, kvc, krc = _c_prep(pf, tabs_c, g_cq[l], g_ckv[l], _prep_w_uq(w_uq[l]), _prep_w_ukv(w_ukv[l]),
                               p_len, tm_ln, COL_F32)
        o_c = _attn_c(qc, kvc, krc, g_grp[l, 2], batch, p_len, tq)
        lam_init = 0.8 - 0.6 * math.exp(-0.3 * l)
        lam_p = jnp.stack([lam_q1[l], lam_k1[l], lam_q2[l], lam_k2[l]]).astype(F32)
        o_d = _attn_d(pb, bias_d, lam_p, g_diff[l], batch, p_len, tq, lam_init, COL_BF)

        mix_in = jnp.concatenate([o_a, o_b, o_c, o_d], axis=1)
        mix = _mm(mix_in, w_out[l].astype(BF16), tm, min(1024, d), F32, "out_proj")
        h, hb_ffn = _res_ln(h, mix, ln1_g[l], ln1_b[l], alpha, tm_ln, p_len, True)

        act = _ffn_up(hb_ffn, w_up, conv_w, conv_b, l, p_len, tm_ffn, FF_TILE)
        ffn = _mm(act, w_down[l].astype(BF16), tm, min(512, d), F32, "ffn_down")
        if l + 1 < depth:
            h, hb = _res_ln(h, ffn, ln2_g[l], ln2_b[l], alpha, tm_ln, p_len, False)
        else:
            out = _res_ln_final(h, ffn, ln2_g[l], ln2_b[l], alpha, batch, p_len)

    return out.reshape(batch, s_len, d)
```

```python
import functools
import math

import jax
import jax.numpy as jnp
import numpy as np
from jax import lax
from jax.experimental import pallas as pl
from jax.experimental.pallas import tpu as pltpu

N_META = 16
BLK = 128
PAD_FRONT = BLK - N_META
A_HEADS, A_HDIM = 8, 128
IDX_HEADS, IDX_HDIM, IDX_ROPE = 16, 64, 32
TOPK_MAX = 256
B_HEADS, B_KV_HEADS, B_HDIM = 16, 2, 64
WINDOW = 128
C_HEADS, C_Q_RANK, C_KV_RANK, C_NOPE, C_ROPE, C_VDIM = 8, 1024, 512, 128, 64, 128
D_HEADS, D_HDIM = 8, 64
GROUP_WIDTH = 1024
CONV_W = 3
ROPE_BASE = 10000.0
NEG = -1e30
IN_SIZES = (1024, 128, 128, 1024, 64, 16, 1024, 128, 128, 1024, 512, 64, 1024, 1024, 1024)

LANES = 128
BF16_SUBLANES = 16
MXU_DEPTH = 256
VMEM_LIMIT = 56 * 1024 * 1024
FF_TILE = 256
N_EXTENT_VARIANTS = 6
N_EXTENT_VARIANTS_D = 3

F32 = jnp.float32
BF16 = jnp.bfloat16


def _cparams(n_axes):
    return pltpu.CompilerParams(dimension_semantics=("arbitrary",) * n_axes, vmem_limit_bytes=VMEM_LIMIT)


def _row_tile(p_len, target):
    best = BF16_SUBLANES
    for t in range(BF16_SUBLANES, target + 1, BF16_SUBLANES):
        if p_len % t == 0:
            best = t
    return best


def _dot(a, b):
    return jnp.dot(a, b, preferred_element_type=F32)


def _dot_nt(a, b):
    return lax.dot_general(a, b, (((1,), (1,)), ((), ())), preferred_element_type=F32)


def _ln_rows(x, g, b):
    mu = jnp.mean(x, axis=-1, keepdims=True)
    xc = x - mu
    var = jnp.mean(xc * xc, axis=-1, keepdims=True)
    return xc * lax.rsqrt(var + 1e-5) * g + b


def _ln_in_kernel(x_ref, meta_ref, g_ref, b_ref, h_ref, hb_ref):
    def emit(rows):
        y = _ln_rows(rows, g_ref[...], b_ref[...])
        h_ref[...] = y
        hb_ref[...] = y.astype(BF16)

    @pl.when(pl.program_id(1) == 0)
    def _():
        emit(jnp.concatenate([jnp.zeros((PAD_FRONT, meta_ref.shape[1]), F32), meta_ref[...]], axis=0))

    @pl.when(pl.program_id(1) > 0)
    def _():
        emit(x_ref[...])


def _ln_in(x, meta, g, b):
    batch, s_len, d = x.shape
    nb = s_len // BLK + 1
    t = batch * nb * BLK
    return pl.pallas_call(
        _ln_in_kernel,
        grid=(batch, nb),
        in_specs=[pl.BlockSpec((None, BLK, d), lambda bi, n: (bi, jnp.maximum(n - 1, 0), 0)),
                  pl.BlockSpec((N_META, d), lambda bi, n: (0, 0)),
                  pl.BlockSpec((1, d), lambda bi, n: (0, 0)),
                  pl.BlockSpec((1, d), lambda bi, n: (0, 0))],
        out_specs=[pl.BlockSpec((BLK, d), lambda bi, n: (bi * nb + n, 0)),
                   pl.BlockSpec((BLK, d), lambda bi, n: (bi * nb + n, 0))],
        out_shape=[jax.ShapeDtypeStruct((t, d), F32), jax.ShapeDtypeStruct((t, d), BF16)],
        compiler_params=_cparams(2),
        name="ln_in",
    )(x, meta.astype(F32), g.reshape(1, d), b.reshape(1, d))


def _res_ln_kernel(h_ref, y_ref, g_ref, b_ref, o_ref, ob_ref, *, alpha, tm, tiles_per_seq, zero_pad):
    out = _ln_rows(alpha * h_ref[...] + y_ref[...], g_ref[...], b_ref[...])
    o_ref[...] = out
    if zero_pad:
        p0 = (pl.program_id(0) % tiles_per_seq) * tm
        pos = p0 + lax.broadcasted_iota(jnp.int32, out.shape, 0)
        out = jnp.where(pos >= PAD_FRONT, out, 0.0)
    ob_ref[...] = out.astype(BF16)


def _res_ln(h, y, g, b, alpha, tm, p_len, zero_pad):
    t, d = h.shape
    kern = functools.partial(_res_ln_kernel, alpha=alpha, tm=tm, tiles_per_seq=p_len // tm, zero_pad=zero_pad)
    return pl.pallas_call(
        kern,
        grid=(t // tm,),
        in_specs=[pl.BlockSpec((tm, d), lambda i: (i, 0)),
                  pl.BlockSpec((tm, d), lambda i: (i, 0)),
                  pl.BlockSpec((1, d), lambda i: (0, 0)),
                  pl.BlockSpec((1, d), lambda i: (0, 0))],
        out_specs=[pl.BlockSpec((tm, d), lambda i: (i, 0)),
                   pl.BlockSpec((tm, d), lambda i: (i, 0))],
        out_shape=[jax.ShapeDtypeStruct((t, d), F32), jax.ShapeDtypeStruct((t, d), BF16)],
        compiler_params=_cparams(1),
        name="res_ln",
    )(h, y, g.reshape(1, d), b.reshape(1, d))


def _res_ln_final_kernel(h_ref, y_ref, g_ref, b_ref, o_ref, *, alpha):
    o_ref[...] = _ln_rows(alpha * h_ref[...] + y_ref[...], g_ref[...], b_ref[...])


def _res_ln_final(h, y, g, b, alpha, batch, p_len):
    t, d = h.shape
    nb = p_len // BLK
    out_idx = lambda i: ((i // nb) * (nb - 1) + jnp.maximum(i % nb - 1, 0), 0)
    return pl.pallas_call(
        functools.partial(_res_ln_final_kernel, alpha=alpha),
        grid=(t // BLK,),
        in_specs=[pl.BlockSpec((BLK, d), lambda i: (i, 0)),
                  pl.BlockSpec((BLK, d), lambda i: (i, 0)),
                  pl.BlockSpec((1, d), lambda i: (0, 0)),
                  pl.BlockSpec((1, d), lambda i: (0, 0))],
        out_specs=pl.BlockSpec((BLK, d), out_idx),
        out_shape=jax.ShapeDtypeStruct((batch * (p_len - BLK), d), F32),
        compiler_params=_cparams(1),
        name="res_ln_final",
    )(h, y, g.reshape(1, d), b.reshape(1, d))


def _mm_kernel(a_ref, w_ref, o_ref):
    o_ref[...] = _dot(a_ref[...], w_ref[...]).astype(o_ref.dtype)


def _mm(a, w, tm, tn, out_dtype, name):
    t, k = a.shape
    n = w.shape[1]
    return pl.pallas_call(
        _mm_kernel,
        grid=(n // tn, t // tm),
        in_specs=[pl.BlockSpec((tm, k), lambda j, i: (i, 0)),
                  pl.BlockSpec((k, tn), lambda j, i: (0, j))],
        out_specs=pl.BlockSpec((tm, tn), lambda j, i: (i, j)),
        out_shape=jax.ShapeDtypeStruct((t, n), out_dtype),
        compiler_params=_cparams(2),
        name=name,
    )(a, w)


def _rope128(x, c, s1, s2, half):
    return x * c + pltpu.roll(x, LANES - half, 1) * s1 + pltpu.roll(x, half, 1) * s2


def _rope_tables(pos, rot_dim, period):
    half = rot_dim // 2
    inv = ROPE_BASE ** (-jnp.arange(half, dtype=F32) / half)
    ang = pos.astype(F32)[:, None] * inv[None]
    cos, sin = jnp.cos(ang), jnp.sin(ang)
    n = pos.shape[0]
    lane = np.arange(LANES) % period
    li = lane % half
    cos_l, sin_l = cos[:, li], sin[:, li]
    first = jnp.asarray(lane < half)[None]
    second = jnp.asarray((lane >= half) & (lane < rot_dim))[None]
    c = jnp.where(first | second, cos_l, 1.0)
    s1 = jnp.where(first, -sin_l, 0.0)
    s2 = jnp.where(second, sin_l, 0.0)
    return jnp.stack([c, s1, s2]).astype(F32).reshape(3, n, LANES)


def _rms(x, g, eps=1e-6):
    return x * lax.rsqrt(jnp.mean(x * x, axis=-1, keepdims=True) + eps) * g


def _key_extents(nblk, nvar):
    step = -(-nblk // nvar)
    return [(lo, min(lo + step, nblk), min(lo + step, nblk)) for lo in range(0, nblk, step)]


def _for_each_extent(qi, nblk, body, nvar=N_EXTENT_VARIANTS):
    for lo, hi, eb in _key_extents(nblk, nvar):
        @pl.when(jnp.logical_and(qi >= lo, qi < hi))
        def _():
            body(eb * BLK)


def _attn_a_kernel(iq_ref, wq_ref, ikw_ref, tq_ref, tk_ref, aq_ref, ak_ref, av_ref, bias_ref, g_ref,
                   o_ref, key_ref, mask_ref, qs_ref, os_ref, lo_ref, cand_ref, cnt_ref, *, tq, p_len, k_top):
    qi = pl.program_id(1)
    q0 = qi * tq
    half = IDX_ROPE // 2
    for h in range(A_HEADS):
        qs_ref[h * tq:(h + 1) * tq, :] = aq_ref[:, h * A_HDIM:(h + 1) * A_HDIM]

    def body(ext):
        ik = _rope128(ikw_ref[:ext, :], tk_ref[0, :ext, :], tk_ref[1, :ext, :], tk_ref[2, :ext, :],
                      half)[:, :IDX_HDIM].astype(BF16)
        w = wq_ref[...][:, IDX_HDIM:IDX_HDIM + IDX_HEADS] * ((IDX_HEADS * IDX_HDIM) ** -0.5)
        tc, ts1, ts2 = tq_ref[0], tq_ref[1], tq_ref[2]
        score = jnp.zeros((tq, ext), F32)
        for c in range(IDX_HEADS * IDX_HDIM // LANES):
            chunk = _rope128(iq_ref[:, c * LANES:(c + 1) * LANES], tc, ts1, ts2, half).astype(BF16)
            for j in range(LANES // IDX_HDIM):
                h = c * (LANES // IDX_HDIM) + j
                rel = jnp.maximum(_dot_nt(chunk[:, j * IDX_HDIM:(j + 1) * IDX_HDIM], ik), 0.0)
                score = score + w[:, h:h + 1] * rel
        qpos = q0 + lax.broadcasted_iota(jnp.int32, (tq, ext), 0)
        kpos = lax.broadcasted_iota(jnp.int32, (tq, ext), 1)
        vis = jnp.where(kpos <= qpos, jnp.where(kpos >= PAD_FRONT, 1.0, 0.0), 0.0)
        score = jnp.where(vis > 0.0, score, NEG)
        bits = lax.bitcast_convert_type(score, jnp.int32)
        key_ref[:, :ext] = jnp.where(bits < 0, bits ^ jnp.int32(0x7FFFFFFF), bits)
        kf = jnp.float32(k_top)

        def count(pred):
            return jnp.sum(jnp.where(pred, 1.0, 0.0), axis=1, keepdims=True)

        hs = tq // 2

        def lane_counts(r0, cand):
            terms = [jnp.where(key_ref[r0:r0 + hs, c * LANES:(c + 1) * LANES] >= cand, 1.0, 0.0)
                     for c in range(ext // LANES)]
            while len(terms) > 1:
                terms = [terms[i] + terms[i + 1] if i + 1 < len(terms) else terms[i] for i in range(0, len(terms), 2)]
            return terms[0]

        lo_ref[...] = jnp.full((tq, LANES), -2 ** 31, jnp.int32)
        cand_ref[...] = jnp.full((hs, LANES), -2 ** 31, jnp.int32)
        cnt_ref[...] = jnp.zeros((hs, LANES), F32)

        def finish_second_half():
            tot = jnp.sum(cnt_ref[...], axis=1, keepdims=True)
            return jnp.where(tot >= kf, cand_ref[...], lo_ref[hs:, :])

        def vbody(i, carry):
            inc = jnp.left_shift(jnp.int32(1), 31 - i)
            lo_b = finish_second_half()
            lo_a = lo_ref[:hs, :]
            cand_a = lo_a + inc
            tot_a = jnp.sum(lane_counts(0, cand_a), axis=1, keepdims=True)
            cand_b = lo_b + inc
            cnt_ref[...] = lane_counts(hs, cand_b)
            cand_ref[...] = cand_b
            lo_ref[hs:, :] = lo_b
            lo_ref[:hs, :] = jnp.where(tot_a >= kf, cand_a, lo_a)
            return carry

        lax.fori_loop(0, 32, vbody, 0)
        lo_ref[hs:, :] = finish_second_half()
        thr = lo_ref[:, 0:1]
        key = key_ref[:, :ext]
        n_gt = count(key > thr)
        n_eq = count(key == thr)
        need = kf - n_gt
        n_eq_vis = jnp.sum(jnp.where(key == thr, vis, 0.0), axis=1, keepdims=True)
        mask_ref[:, :ext] = jnp.where(key >= thr, jnp.where(vis > 0.0, 0.0, NEG), NEG)
        surplus = jnp.where(n_eq_vis > 0.0, jnp.where(n_eq > need, 1.0, 0.0), 0.0)

        @pl.when(jnp.max(surplus) > 0.0)
        def _():
            nbits = max(1, (ext - 1).bit_length())
            keyv = key_ref[:, :ext]
            col = lax.broadcasted_iota(jnp.int32, (tq, ext), 1)

            def tbody(i, j):
                cand = j + jnp.left_shift(jnp.int32(1), nbits - 1 - i)
                f = jnp.sum(jnp.where(keyv == thr, jnp.where(col < cand, 1.0, 0.0), 0.0), axis=1, keepdims=True)
                return jnp.where(f < need, cand, j)

            jstar = lax.fori_loop(0, nbits, tbody, jnp.zeros((tq, 1), jnp.int32))
            chosen = jnp.where(keyv > thr, 1.0, jnp.where(keyv == thr, jnp.where(col <= jstar, 1.0, 0.0), 0.0))
            mask_ref[:, :ext] = jnp.where(chosen > 0.0, jnp.where(vis > 0.0, 0.0, NEG), NEG)

        scale = A_HDIM ** -0.5

        def hbody(h, carry):
            r0 = pl.multiple_of(h * tq, tq)
            s = _dot_nt(qs_ref[pl.ds(r0, tq), :], ak_ref[:ext, :]) * scale + bias_ref[h, :, :ext] + mask_ref[:, :ext]
            m = jnp.max(s, axis=1, keepdims=True)
            e = jnp.exp(s - m)
            l = jnp.sum(e, axis=1, keepdims=True)
            os_ref[pl.ds(r0, tq), :] = _dot(e.astype(BF16), av_ref[:ext, :]) / l
            return carry

        lax.fori_loop(0, A_HEADS, hbody, 0, unroll=2)

    _for_each_extent(qi, p_len // BLK, body)
    o = jnp.concatenate([os_ref[h * tq:(h + 1) * tq, :] for h in range(A_HEADS)], axis=1)
    o_ref[...] = _rms(o, g_ref[...]).astype(BF16)


def _attn_a(pf, pb, tabs_idx, bias, g, batch, p_len, tq, k_top, col):
    nq = p_len // tq
    t = batch * p_len
    kern = functools.partial(_attn_a_kernel, tq=tq, p_len=p_len, k_top=k_top)
    return pl.pallas_call(
        kern,
        grid=(batch, nq),
        in_specs=[
            pl.BlockSpec((tq, 1024), lambda b, q: (b * nq + q, col["i_q"] // 1024)),
            pl.BlockSpec((tq, LANES), lambda b, q: (b * nq + q, col["ikw"] // LANES)),
            pl.BlockSpec((p_len, LANES), lambda b, q: (b, col["ikw"] // LANES)),
            pl.BlockSpec((3, tq, LANES), lambda b, q: (0, q, 0)),
            pl.BlockSpec((3, p_len, LANES), lambda b, q: (0, 0, 0)),
            pl.BlockSpec((tq, 1024), lambda b, q: (b * nq + q, col["a_q"] // 1024)),
            pl.BlockSpec((p_len, LANES), lambda b, q: (b, col["a_k"] // LANES)),
            pl.BlockSpec((p_len, LANES), lambda b, q: (b, col["a_v"] // LANES)),
            pl.BlockSpec((A_HEADS, 1, p_len), lambda b, q: (0, 0, 0)),
            pl.BlockSpec((1, GROUP_WIDTH), lambda b, q: (0, 0)),
        ],
        out_specs=pl.BlockSpec((tq, GROUP_WIDTH), lambda b, q: (b * nq + q, 0)),
        out_shape=jax.ShapeDtypeStruct((t, GROUP_WIDTH), BF16),
        scratch_shapes=[pltpu.VMEM((tq, p_len), jnp.int32),
                        pltpu.VMEM((tq, p_len), F32),
                        pltpu.VMEM((A_HEADS * tq, A_HDIM), BF16),
                        pltpu.VMEM((A_HEADS * tq, A_HDIM), F32),
                        pltpu.VMEM((tq, LANES), jnp.int32),
                        pltpu.VMEM((tq // 2, LANES), jnp.int32),
                        pltpu.VMEM((tq // 2, LANES), F32)],
        compiler_params=_cparams(2),
        name="attn_a",
    )(pf, pf, pf, tabs_idx, tabs_idx, pb, pb, pb, bias, g.reshape(1, GROUP_WIDTH))


def _attn_b_kernel(q_ref, kp_ref, kc_ref, vp_ref, vc_ref, bias_ref, sink_ref, g_ref, o_ref):
    n = pl.program_id(1)
    grp = B_HEADS // B_KV_HEADS
    first_col = PAD_FRONT - (n - 1) * BLK
    col = lax.broadcasted_iota(jnp.int32, (grp * BLK, 2 * BLK), 1)
    pad_mask = jnp.where(col >= first_col, 0.0, NEG)
    outs = []
    for gi in range(B_KV_HEADS):
        q = jnp.concatenate([q_ref[:, (gi * grp + j) * B_HDIM:(gi * grp + j + 1) * B_HDIM] for j in range(grp)], axis=0)
        k = jnp.concatenate([kp_ref[:, gi * B_HDIM:(gi + 1) * B_HDIM], kc_ref[:, gi * B_HDIM:(gi + 1) * B_HDIM]], axis=0)
        v = jnp.concatenate([vp_ref[:, gi * B_HDIM:(gi + 1) * B_HDIM], vc_ref[:, gi * B_HDIM:(gi + 1) * B_HDIM]], axis=0)
        s = _dot_nt(q, k) + bias_ref[gi] + pad_mask
        sink = sink_ref[gi]
        m = jnp.maximum(jnp.max(s, axis=1, keepdims=True), sink)
        e = jnp.exp(s - m)
        l = jnp.sum(e, axis=1, keepdims=True) + jnp.exp(sink - m)
        o = _dot(e.astype(BF16), v) / l
        outs.extend(o[j * BLK:(j + 1) * BLK, :] for j in range(grp))
    o = jnp.concatenate(outs, axis=1)
    o_ref[...] = _rms(o, g_ref[...]).astype(BF16)


def _attn_b(pb, bias, sink_rows, g, batch, p_len, col):
    nb = p_len // BLK
    t = batch * p_len
    grp = B_HEADS // B_KV_HEADS
    return pl.pallas_call(
        _attn_b_kernel,
        grid=(batch, nb),
        in_specs=[
            pl.BlockSpec((BLK, 1024), lambda b, n: (b * nb + n, col["b_q"] // 1024)),
            pl.BlockSpec((BLK, LANES), lambda b, n: (b * nb + jnp.maximum(n - 1, 0), col["b_k"] // LANES)),
            pl.BlockSpec((BLK, LANES), lambda b, n: (b * nb + n, col["b_k"] // LANES)),
            pl.BlockSpec((BLK, LANES), lambda b, n: (b * nb + jnp.maximum(n - 1, 0), col["b_v"] // LANES)),
            pl.BlockSpec((BLK, LANES), lambda b, n: (b * nb + n, col["b_v"] // LANES)),
            pl.BlockSpec((B_KV_HEADS, grp * BLK, 2 * BLK), lambda b, n: (0, 0, 0)),
            pl.BlockSpec((B_KV_HEADS, grp * BLK, 1), lambda b, n: (0, 0, 0)),
            pl.BlockSpec((1, GROUP_WIDTH), lambda b, n: (0, 0)),
        ],
        out_specs=pl.BlockSpec((BLK, GROUP_WIDTH), lambda b, n: (b * nb + n, 0)),
        out_shape=jax.ShapeDtypeStruct((t, GROUP_WIDTH), BF16),
        compiler_params=_cparams(2),
        name="attn_b",
    )(pb, pb, pb, pb, pb, bias, sink_rows, g.reshape(1, GROUP_WIDTH))


def _c_prep_kernel(cq_ref, ckv_ref, kr_ref, tab_ref, gq_ref, gkv_ref, wq_ref, wkv_ref, q_ref, k_ref, v_ref):
    half = C_ROPE // 2
    tc, ts1, ts2 = tab_ref[0], tab_ref[1], tab_ref[2]
    slot = C_NOPE + LANES
    xq = _rms(cq_ref[...], gq_ref[...]).astype(BF16)
    q = _dot(xq, wq_ref[...])
    xkv = _rms(ckv_ref[...], gkv_ref[...]).astype(BF16)
    kv = _dot(xkv, wkv_ref[...])
    kr = _rope128(kr_ref[...], tc, ts1, ts2, half).astype(BF16)
    for h in range(C_HEADS):
        lo = h * slot
        q_ref[:, lo:lo + C_NOPE] = q[:, lo:lo + C_NOPE].astype(BF16)
        q_ref[:, lo + C_NOPE:lo + slot] = _rope128(q[:, lo + C_NOPE:lo + slot], tc, ts1, ts2, half).astype(BF16)
        k_ref[:, lo:lo + C_NOPE] = kv[:, h * C_NOPE:(h + 1) * C_NOPE].astype(BF16)
        k_ref[:, lo + C_NOPE:lo + slot] = kr
    v_ref[...] = kv[:, C_HEADS * C_NOPE:].astype(BF16)


def _c_prep(pf, tabs, g_cq, g_ckv, w_uq_p, w_ukv_p, p_len, tm, col):
    t = pf.shape[0]
    tiles = p_len // tm
    wide = C_HEADS * (C_NOPE + LANES)
    return pl.pallas_call(
        _c_prep_kernel,
        grid=(t // tm,),
        in_specs=[
            pl.BlockSpec((tm, C_Q_RANK), lambda i: (i, col["c_cq"] // C_Q_RANK)),
            pl.BlockSpec((tm, C_KV_RANK), lambda i: (i, col["c_ckv"] // C_KV_RANK)),
            pl.BlockSpec((tm, LANES), lambda i: (i, col["c_kr"] // LANES)),
            pl.BlockSpec((3, tm, LANES), lambda i: (0, i % tiles, 0)),
            pl.BlockSpec((1, C_Q_RANK), lambda i: (0, 0)),
            pl.BlockSpec((1, C_KV_RANK), lambda i: (0, 0)),
            pl.BlockSpec((C_Q_RANK, wide), lambda i: (0, 0)),
            pl.BlockSpec((C_KV_RANK, C_HEADS * (C_NOPE + C_VDIM)), lambda i: (0, 0)),
        ],
        out_specs=[pl.BlockSpec((tm, wide), lambda i: (i, 0)),
                   pl.BlockSpec((tm, wide), lambda i: (i, 0)),
                   pl.BlockSpec((tm, C_HEADS * C_VDIM), lambda i: (i, 0))],
        out_shape=[jax.ShapeDtypeStruct((t, wide), BF16),
                   jax.ShapeDtypeStruct((t, wide), BF16),
                   jax.ShapeDtypeStruct((t, C_HEADS * C_VDIM), BF16)],
        compiler_params=_cparams(1),
        name="c_prep",
    )(pf, pf, pf, tabs, g_cq.reshape(1, -1), g_ckv.reshape(1, -1), w_uq_p, w_ukv_p)


def _causal_mask(q0, tq, ext):
    qpos = q0 + lax.broadcasted_iota(jnp.int32, (tq, ext), 0)
    kpos = lax.broadcasted_iota(jnp.int32, (tq, ext), 1)
    return jnp.where(kpos <= qpos, jnp.where(kpos >= PAD_FRONT, 0.0, NEG), NEG)


def _attn_c_kernel(q_ref, k_ref, v_ref, g_ref, o_ref, os_ref, *, tq, p_len):
    qi = pl.program_id(1)
    scale = (C_NOPE + C_ROPE) ** -0.5
    slot = C_NOPE + LANES

    def body(ext):
        mask = _causal_mask(qi * tq, tq, ext)
        for h in range(C_HEADS):
            s = _dot_nt(q_ref[:, h * slot:(h + 1) * slot], k_ref[:ext, h * slot:(h + 1) * slot]) * scale + mask
            m = jnp.max(s, axis=1, keepdims=True)
            e = jnp.exp(s - m)
            l = jnp.sum(e, axis=1, keepdims=True)
            os_ref[:, h * C_VDIM:(h + 1) * C_VDIM] = _dot(e.astype(BF16), v_ref[:ext, h * C_VDIM:(h + 1) * C_VDIM]) / l

    _for_each_extent(qi, p_len // BLK, body)
    o_ref[...] = _rms(os_ref[...], g_ref[...]).astype(BF16)


def _attn_c(qc, kc, vc, g, batch, p_len, tq):
    nq = p_len // tq
    t = batch * p_len
    wide = C_HEADS * (C_NOPE + LANES)
    kern = functools.partial(_attn_c_kernel, tq=tq, p_len=p_len)
    return pl.pallas_call(
        kern,
        grid=(batch, nq),
        in_specs=[pl.BlockSpec((tq, wide), lambda b, q: (b * nq + q, 0)),
                  pl.BlockSpec((p_len, wide), lambda b, q: (b, 0)),
                  pl.BlockSpec((p_len, C_HEADS * C_VDIM), lambda b, q: (b, 0)),
                  pl.BlockSpec((1, GROUP_WIDTH), lambda b, q: (0, 0))],
        out_specs=pl.BlockSpec((tq, GROUP_WIDTH), lambda b, q: (b * nq + q, 0)),
        out_shape=jax.ShapeDtypeStruct((t, GROUP_WIDTH), BF16),
        scratch_shapes=[pltpu.VMEM((tq, GROUP_WIDTH), F32)],
        compiler_params=_cparams(2),
        name="attn_c",
    )(qc, kc, vc, g.reshape(1, GROUP_WIDTH))


def _attn_d_kernel(q_ref, k_ref, v_ref, bias_ref, lam_ref, g_ref, o_ref, os_ref, *, tq, p_len, lam_init):
    qi = pl.program_id(1)
    lp = lam_ref[...]
    lam = (jnp.exp(jnp.sum(lp[0:1] * lp[1:2], axis=1, keepdims=True))
           - jnp.exp(jnp.sum(lp[2:3] * lp[3:4], axis=1, keepdims=True)) + lam_init)
    vw = 2 * D_HDIM

    def body(ext):
        mask = _causal_mask(qi * tq, tq, ext)
        for h in range(D_HEADS):
            bm = bias_ref[h, :, :ext] + mask
            es, ls = [], []
            for c in range(2):
                lo = (2 * h + c) * D_HDIM
                s = _dot_nt(q_ref[:, lo:lo + D_HDIM], k_ref[:ext, lo:lo + D_HDIM]) + bm
                m = jnp.max(s, axis=1, keepdims=True)
                e = jnp.exp(s - m)
                es.append(e)
                ls.append(jnp.sum(e, axis=1, keepdims=True))
            a = es[0] * (1.0 / ls[0]) - es[1] * (lam / ls[1])
            os_ref[:, h * vw:(h + 1) * vw] = _dot(a.astype(BF16), v_ref[:ext, h * vw:(h + 1) * vw])

    _for_each_extent(qi, p_len // BLK, body, N_EXTENT_VARIANTS_D)
    g = g_ref[...]
    outs = [_rms(os_ref[:, h * vw:(h + 1) * vw], g) * (1.0 - lam_init) for h in range(D_HEADS)]
    o_ref[...] = jnp.concatenate(outs, axis=1).astype(BF16)


def _attn_d(pb, bias, lam_p, g, batch, p_len, tq, lam_init, col):
    nq = p_len // tq
    t = batch * p_len
    kern = functools.partial(_attn_d_kernel, tq=tq, p_len=p_len, lam_init=lam_init)
    return pl.pallas_call(
        kern,
        grid=(batch, nq),
        in_specs=[pl.BlockSpec((tq, 1024), lambda b, q: (b * nq + q, col["d_q"] // 1024)),
                  pl.BlockSpec((p_len, 1024), lambda b, q: (b, col["d_k"] // 1024)),
                  pl.BlockSpec((p_len, 1024), lambda b, q: (b, col["d_v"] // 1024)),
                  pl.BlockSpec((D_HEADS, 1, p_len), lambda b, q: (0, 0, 0)),
                  pl.BlockSpec((4, D_HDIM), lambda b, q: (0, 0)),
                  pl.BlockSpec((1, 2 * D_HDIM), lambda b, q: (0, 0))],
        out_specs=pl.BlockSpec((tq, GROUP_WIDTH), lambda b, q: (b * nq + q, 0)),
        out_shape=jax.ShapeDtypeStruct((t, GROUP_WIDTH), BF16),
        scratch_shapes=[pltpu.VMEM((tq, GROUP_WIDTH), F32)],
        compiler_params=_cparams(2),
        name="attn_d",
    )(pb, pb, pb, bias, lam_p, g.reshape(1, 2 * D_HDIM))


def _ffn_up_kernel(x_ref, wg_ref, wu_ref, cwg_ref, cwu_ref, cbg_ref, cbu_ref, o_ref,
                   wgb_ref, wub_ref, ha_ref, hb_ref, carry_ref, *, tm, n_row_tiles, tiles_per_seq):
    s = pl.program_id(0)
    d = x_ref.shape[1]
    kc = min(d, MXU_DEPTH)
    nk = d // kc
    rc = next(r for r in (64, 32, 16, 8) if tm % r == 0)
    nr = tm // rc

    @pl.when(s % n_row_tiles == 0)
    def _():
        wgb_ref[...] = wg_ref[...].astype(BF16)
        wub_ref[...] = wu_ref[...].astype(BF16)

    @pl.when(s == 0)
    def _():
        hb_ref[...] = jnp.zeros_like(hb_ref)

    @pl.when((s + tiles_per_seq - 1) % tiles_per_seq == 0)
    def _():
        carry_ref[...] = jnp.zeros_like(carry_ref)

    def run(rd_ref, wr_ref):
        cws = (cwg_ref[...], cwu_ref[...])
        cbs = (cbg_ref[...], cbu_ref[...])
        for step in range(nr):
            for kk in range((step * nk) // nr, ((step + 1) * nk) // nr):
                for b, wb_ref in enumerate((wgb_ref, wub_ref)):
                    part = _dot(x_ref[:, kk * kc:(kk + 1) * kc], wb_ref[kk * kc:(kk + 1) * kc, :])
                    if kk == 0:
                        wr_ref[b] = part
                    else:
                        wr_ref[b] = wr_ref[b] + part
            r0 = step * rc
            ys = []
            for b in range(2):
                top = carry_ref[b] if step == 0 else rd_ref[b, r0 - 8:r0, :]
                cur = rd_ref[b, r0:r0 + rc, :]
                full = jnp.concatenate([top, cur], axis=0)
                cw = cws[b]
                y = cw[2:3] * cur + cw[1:2] * pltpu.roll(full, 1, 0)[8:] + cw[0:1] * pltpu.roll(full, 2, 0)[8:]
                ys.append(y + cbs[b])
            o_ref[r0:r0 + rc, :] = (ys[0] * jax.nn.sigmoid(ys[0]) * ys[1]).astype(BF16)
        for b in range(2):
            carry_ref[b] = rd_ref[b, tm - 8:, :]

    @pl.when(s % 2 == 0)
    def _():
        run(hb_ref, ha_ref)

    @pl.when(s % 2 == 1)
    def _():
        run(ha_ref, hb_ref)


def _ffn_up(hb, w_up, conv_w, conv_b, layer, p_len, tm, tn):
    t, d = hb.shape
    f = w_up.shape[2] // 2
    assert f % tn == 0 and tm % 8 == 0 and p_len % tm == 0
    nj, nt = f // tn, t // tm
    n_tiles = nj * nt
    kern = functools.partial(_ffn_up_kernel, tm=tm, n_row_tiles=nt, tiles_per_seq=p_len // tm)
    cb = conv_b.reshape(conv_b.shape[0], 1, 2 * f)
    cur = lambda s: jnp.minimum(s, n_tiles - 1)
    prv = lambda s: jnp.maximum(s - 1, 0)
    return pl.pallas_call(
        kern,
        grid=(n_tiles + 1,),
        in_specs=[pl.BlockSpec((tm, d), lambda s: (cur(s) % nt, 0)),
                  pl.BlockSpec((None, d, tn), lambda s: (layer, 0, cur(s) // nt)),
                  pl.BlockSpec((None, d, tn), lambda s: (layer, 0, nj + cur(s) // nt)),
                  pl.BlockSpec((None, CONV_W, tn), lambda s: (layer, 0, prv(s) // nt)),
                  pl.BlockSpec((None, CONV_W, tn), lambda s: (layer, 0, nj + prv(s) // nt)),
                  pl.BlockSpec((None, 1, tn), lambda s: (layer, 0, prv(s) // nt)),
                  pl.BlockSpec((None, 1, tn), lambda s: (layer, 0, nj + prv(s) // nt))],
        out_specs=pl.BlockSpec((tm, tn), lambda s: (prv(s) % nt, prv(s) // nt)),
        out_shape=jax.ShapeDtypeStruct((t, f), BF16),
        scratch_shapes=[pltpu.VMEM((d, tn), BF16), pltpu.VMEM((d, tn), BF16),
                        pltpu.VMEM((2, tm, tn), F32), pltpu.VMEM((2, tm, tn), F32), pltpu.VMEM((2, 8, tn), F32)],
        compiler_params=_cparams(1),
        name="ffn_up",
    )(hb, w_up, w_up, conv_w, conv_w, cb, cb)


def _offsets(names, sizes):
    out, o = {}, 0
    for n, s in zip(names, sizes):
        out[n] = o
        o += s
    return out, o


_BF_NAMES = ("a_q", "b_q", "d_q", "d_k", "d_v", "a_k", "a_v", "b_k", "b_v")
_BF_SIZES = (1024, 1024, 1024, 1024, 1024, 128, 128, 128, 128)
_F32_NAMES = ("i_q", "c_cq", "c_ckv", "ikw", "c_kr")
_F32_SIZES = (1024, 1024, 512, 128, 128)
COL_BF, N_BF = _offsets(_BF_NAMES, _BF_SIZES)
COL_F32, N_F32 = _offsets(_F32_NAMES, _F32_SIZES)


def _prep_w_in(w):
    d = w.shape[0]
    seg = dict(zip(("a_q", "a_k", "a_v", "i_q", "i_k", "i_w", "b_q", "b_k", "b_v", "c_cq", "c_ckv", "c_kr",
                    "d_q", "d_k", "d_v"), jnp.split(w, np.cumsum(IN_SIZES)[:-1].tolist(), axis=1)))
    seg["b_q"] = seg["b_q"] * (B_HDIM ** -0.5)
    seg["d_q"] = seg["d_q"] * (D_HDIM ** -0.5)
    wb = jnp.concatenate([seg[n] for n in _BF_NAMES], axis=1).astype(BF16)
    zeros = lambda n: jnp.zeros((d, n), w.dtype)
    ikw = jnp.concatenate([seg["i_k"], seg["i_w"], zeros(LANES - IDX_HDIM - IDX_HEADS)], axis=1)
    ckr = jnp.concatenate([seg["c_kr"], zeros(LANES - C_ROPE)], axis=1)
    wf = jnp.concatenate([seg["i_q"], seg["c_cq"], seg["c_ckv"], ikw, ckr], axis=1).astype(BF16)
    return wb, wf


def _prep_w_uq(w):
    r = w.shape[0]
    w3 = w.reshape(r, C_HEADS, C_NOPE + C_ROPE)
    return jnp.pad(w3, ((0, 0), (0, 0), (0, LANES - C_ROPE))).reshape(r, C_HEADS * (C_NOPE + LANES)).astype(BF16)


def _prep_w_ukv(w):
    r = w.shape[0]
    w3 = w.reshape(r, C_HEADS, C_NOPE + C_VDIM)
    return jnp.concatenate([w3[:, :, :C_NOPE].reshape(r, -1), w3[:, :, C_NOPE:].reshape(r, -1)], axis=1).astype(BF16)


def _alibi(n):
    return 2.0 ** (-8.0 * np.arange(1, n + 1, dtype=np.float64) / n)


def _swa_bias():
    grp = B_HEADS // B_KV_HEADS
    r = np.arange(BLK)[:, None]
    c = np.arange(2 * BLK)[None, :]
    diff = (r + BLK - c).astype(np.float64)
    ok = (diff >= 0) & (diff < WINDOW)
    slopes = _alibi(B_HEADS).reshape(B_KV_HEADS, grp)
    bias = np.where(ok[None, None], -slopes[:, :, None, None] * diff[None, None], NEG)
    return jnp.asarray(bias.reshape(B_KV_HEADS, grp * BLK, 2 * BLK), F32)


def kernel(x, meta_tokens, ln_in_g, ln_in_b, w_in, g_cq, g_ckv, w_uq, w_ukv, sinks, lam_q1, lam_k1, lam_q2, lam_k2,
           g_diff, g_grp, w_out, ln1_g, ln1_b, w_up, conv_w, conv_b, w_down, ln2_g, ln2_b):
    batch, s_len, d = x.shape
    depth = w_in.shape[0]
    p_len = s_len + BLK
    t = batch * p_len
    k_top = min(TOPK_MAX, s_len // 4)
    alpha = (2 * depth) ** 0.25
    tm = _row_tile(p_len, 544)
    tm_ln = _row_tile(p_len, 272)
    tm_ffn = _row_tile(p_len, 1088)
    tq = BLK
    grp = B_HEADS // B_KV_HEADS

    h, hb = _ln_in(x, meta_tokens, ln_in_g, ln_in_b)

    pos = jnp.arange(p_len, dtype=jnp.int32) - PAD_FRONT
    tabs_idx = _rope_tables(pos, IDX_ROPE, IDX_HDIM)
    tabs_c = _rope_tables(pos, C_ROPE, LANES)
    kidx = np.arange(p_len, dtype=np.float64)
    bias_a = jnp.asarray(_alibi(A_HEADS)[:, None, None] * kidx[None, None, :], F32)
    bias_d = jnp.asarray(_alibi(D_HEADS)[:, None, None] * kidx[None, None, :], F32)
    bias_b = _swa_bias()

    for l in range(depth):
        wb, wf = _prep_w_in(w_in[l])
        pb = _mm(hb, wb, tm, N_BF // 4, BF16, "proj_bf16")
        pf = _mm(hb, wf, tm, N_F32 // 2, F32, "proj_f32")

        o_a = _attn_a(pf, pb, tabs_idx, bias_a, g_grp[l, 0], batch, p_len, tq, k_top, {**COL_F32, **COL_BF})
        sink_rows = jnp.repeat(sinks[l].astype(F32).reshape(B_KV_HEADS, grp), BLK, axis=1)[..., None]
        o_b = _attn_b(pb, bias_b, sink_rows, g_grp[l, 1], batch, p_len, COL_BF)
        qc, kc, vc = _c_prep(pf, tabs_c, g_cq[l], g_ckv[l], _prep_w_uq(w_uq[l]), _prep_w_ukv(w_ukv[l]),
                               p_len, tm_ln, COL_F32)
        o_c = _attn_c(qc, kc, vc, g_grp[l, 2], batch, p_len, tq)
        lam_init = 0.8 - 0.6 * math.exp(-0.3 * l)
        lam_p = jnp.stack([lam_q1[l], lam_k1[l], lam_q2[l], lam_k2[l]]).astype(F32)
        o_d = _attn_d(pb, bias_d, lam_p, g_diff[l], batch, p_len, tq, lam_init, COL_BF)

        mix_in = jnp.concatenate([o_a, o_b, o_c, o_d], axis=1)
        mix = _mm(mix_in, w_out[l].astype(BF16), tm, min(1024, d), F32, "out_proj")
        h, hb_ffn = _res_ln(h, mix, ln1_g[l], ln1_b[l], alpha, tm_ln, p_len, True)

        act = _ffn_up(hb_ffn, w_up, conv_w, conv_b, l, p_len, tm_ffn, FF_TILE)
        ffn = _mm(act, w_down[l].astype(BF16), tm, min(512, d), F32, "ffn_down")
        if l + 1 < depth:
            h, hb = _res_ln(h, ffn, ln2_g[l], ln2_b[l], alpha, tm_ln, p_len, False)
        else:
            out = _res_ln_final(h, ffn, ln2_g[l], ln2_b[l], alpha, batch, p_len)

    return out.reshape(batch, s_len, d)
```

```python
import functools
import math

import jax
import jax.numpy as jnp
import numpy as np
from jax import lax
from jax.experimental import pallas as pl
from jax.experimental.pallas import tpu as pltpu

N_META = 16
BLK = 128
PAD_FRONT = BLK - N_META
A_HEADS, A_HDIM = 8, 128
IDX_HEADS, IDX_HDIM, IDX_ROPE = 16, 64, 32
TOPK_MAX = 256
B_HEADS, B_KV_HEADS, B_HDIM = 16, 2, 64
WINDOW = 128
C_HEADS, C_Q_RANK, C_KV_RANK, C_NOPE, C_ROPE, C_VDIM = 8, 1024, 512, 128, 64, 128
D_HEADS, D_HDIM = 8, 64
GROUP_WIDTH = 1024
CONV_W = 3
ROPE_BASE = 10000.0
NEG = -1e30
IN_SIZES = (1024, 128, 128, 1024, 64, 16, 1024, 128, 128, 1024, 512, 64, 1024, 1024, 1024)

LANES = 128
BF16_SUBLANES = 16
MXU_DEPTH = 256
VMEM_LIMIT = 56 * 1024 * 1024
FF_TILE = 256
N_EXTENT_VARIANTS = 6
N_EXTENT_VARIANTS_D = 3
LOG2E = 1.4426950408889634

F32 = jnp.float32
BF16 = jnp.bfloat16


def _cparams(n_axes):
    return pltpu.CompilerParams(dimension_semantics=("arbitrary",) * n_axes, vmem_limit_bytes=VMEM_LIMIT)


def _row_tile(p_len, target):
    best = BF16_SUBLANES
    for t in range(BF16_SUBLANES, target + 1, BF16_SUBLANES):
        if p_len % t == 0:
            best = t
    return best


def _dot(a, b):
    return jnp.dot(a, b, preferred_element_type=F32)


def _dot_nt(a, b):
    return lax.dot_general(a, b, (((1,), (1,)), ((), ())), preferred_element_type=F32)


def _ln_rows(x, g, b):
    mu = jnp.mean(x, axis=-1, keepdims=True)
    xc = x - mu
    var = jnp.mean(xc * xc, axis=-1, keepdims=True)
    return xc * lax.rsqrt(var + 1e-5) * g + b


def _ln_in_kernel(x_ref, meta_ref, g_ref, b_ref, h_ref, hb_ref):
    def emit(rows):
        y = _ln_rows(rows, g_ref[...], b_ref[...])
        h_ref[...] = y
        hb_ref[...] = y.astype(BF16)

    @pl.when(pl.program_id(1) == 0)
    def _():
        emit(jnp.concatenate([jnp.zeros((PAD_FRONT, meta_ref.shape[1]), F32), meta_ref[...]], axis=0))

    @pl.when(pl.program_id(1) > 0)
    def _():
        emit(x_ref[...])


def _ln_in(x, meta, g, b):
    batch, s_len, d = x.shape
    nb = s_len // BLK + 1
    t = batch * nb * BLK
    return pl.pallas_call(
        _ln_in_kernel,
        grid=(batch, nb),
        in_specs=[pl.BlockSpec((None, BLK, d), lambda bi, n: (bi, jnp.maximum(n - 1, 0), 0)),
                  pl.BlockSpec((N_META, d), lambda bi, n: (0, 0)),
                  pl.BlockSpec((1, d), lambda bi, n: (0, 0)),
                  pl.BlockSpec((1, d), lambda bi, n: (0, 0))],
        out_specs=[pl.BlockSpec((BLK, d), lambda bi, n: (bi * nb + n, 0)),
                   pl.BlockSpec((BLK, d), lambda bi, n: (bi * nb + n, 0))],
        out_shape=[jax.ShapeDtypeStruct((t, d), F32), jax.ShapeDtypeStruct((t, d), BF16)],
        compiler_params=_cparams(2),
        name="ln_in",
    )(x, meta.astype(F32), g.reshape(1, d), b.reshape(1, d))


def _res_ln_kernel(h_ref, y_ref, g_ref, b_ref, o_ref, ob_ref, *, alpha, tm, tiles_per_seq, zero_pad):
    out = _ln_rows(alpha * h_ref[...] + y_ref[...], g_ref[...], b_ref[...])
    o_ref[...] = out
    if zero_pad:
        p0 = (pl.program_id(0) % tiles_per_seq) * tm
        pos = p0 + lax.broadcasted_iota(jnp.int32, out.shape, 0)
        out = jnp.where(pos >= PAD_FRONT, out, 0.0)
    ob_ref[...] = out.astype(BF16)


def _res_ln(h, y, g, b, alpha, tm, p_len, zero_pad):
    t, d = h.shape
    kern = functools.partial(_res_ln_kernel, alpha=alpha, tm=tm, tiles_per_seq=p_len // tm, zero_pad=zero_pad)
    return pl.pallas_call(
        kern,
        grid=(t // tm,),
        in_specs=[pl.BlockSpec((tm, d), lambda i: (i, 0)),
                  pl.BlockSpec((tm, d), lambda i: (i, 0)),
                  pl.BlockSpec((1, d), lambda i: (0, 0)),
                  pl.BlockSpec((1, d), lambda i: (0, 0))],
        out_specs=[pl.BlockSpec((tm, d), lambda i: (i, 0)),
                   pl.BlockSpec((tm, d), lambda i: (i, 0))],
        out_shape=[jax.ShapeDtypeStruct((t, d), F32), jax.ShapeDtypeStruct((t, d), BF16)],
        compiler_params=_cparams(1),
        name="res_ln",
    )(h, y, g.reshape(1, d), b.reshape(1, d))


def _res_ln_final_kernel(h_ref, y_ref, g_ref, b_ref, o_ref, *, alpha):
    o_ref[...] = _ln_rows(alpha * h_ref[...] + y_ref[...], g_ref[...], b_ref[...])


def _res_ln_final(h, y, g, b, alpha, batch, p_len):
    t, d = h.shape
    nb = p_len // BLK
    out_idx = lambda i: ((i // nb) * (nb - 1) + jnp.maximum(i % nb - 1, 0), 0)
    return pl.pallas_call(
        functools.partial(_res_ln_final_kernel, alpha=alpha),
        grid=(t // BLK,),
        in_specs=[pl.BlockSpec((BLK, d), lambda i: (i, 0)),
                  pl.BlockSpec((BLK, d), lambda i: (i, 0)),
                  pl.BlockSpec((1, d), lambda i: (0, 0)),
                  pl.BlockSpec((1, d), lambda i: (0, 0))],
        out_specs=pl.BlockSpec((BLK, d), out_idx),
        out_shape=jax.ShapeDtypeStruct((batch * (p_len - BLK), d), F32),
        compiler_params=_cparams(1),
        name="res_ln_final",
    )(h, y, g.reshape(1, d), b.reshape(1, d))


def _mm_kernel(a_ref, w_ref, o_ref):
    o_ref[...] = _dot(a_ref[...], w_ref[...]).astype(o_ref.dtype)


def _mm(a, w, tm, tn, out_dtype, name):
    t, k = a.shape
    n = w.shape[1]
    return pl.pallas_call(
        _mm_kernel,
        grid=(n // tn, t // tm),
        in_specs=[pl.BlockSpec((tm, k), lambda j, i: (i, 0)),
                  pl.BlockSpec((k, tn), lambda j, i: (0, j))],
        out_specs=pl.BlockSpec((tm, tn), lambda j, i: (i, j)),
        out_shape=jax.ShapeDtypeStruct((t, n), out_dtype),
        compiler_params=_cparams(2),
        name=name,
    )(a, w)


def _rope128(x, c, s1, s2, half):
    return x * c + pltpu.roll(x, LANES - half, 1) * s1 + pltpu.roll(x, half, 1) * s2


def _rope_tables(pos, rot_dim, period):
    half = rot_dim // 2
    inv = ROPE_BASE ** (-jnp.arange(half, dtype=F32) / half)
    ang = pos.astype(F32)[:, None] * inv[None]
    cos, sin = jnp.cos(ang), jnp.sin(ang)
    n = pos.shape[0]
    lane = np.arange(LANES) % period
    li = lane % half
    cos_l, sin_l = cos[:, li], sin[:, li]
    first = jnp.asarray(lane < half)[None]
    second = jnp.asarray((lane >= half) & (lane < rot_dim))[None]
    c = jnp.where(first | second, cos_l, 1.0)
    s1 = jnp.where(first, -sin_l, 0.0)
    s2 = jnp.where(second, sin_l, 0.0)
    return jnp.stack([c, s1, s2]).astype(F32).reshape(3, n, LANES)


def _rms(x, g, eps=1e-6):
    return x * lax.rsqrt(jnp.mean(x * x, axis=-1, keepdims=True) + eps) * g


def _key_extents(nq, tq, p_len, nvar):
    step = -(-nq // nvar)
    out = []
    for lo in range(0, nq, step):
        hi = min(lo + step, nq)
        out.append((lo, hi, min(p_len, -(-(hi * tq) // BLK) * BLK)))
    return out


def _for_each_extent(qi, nq, tq, p_len, body, nvar=N_EXTENT_VARIANTS):
    for lo, hi, ext in _key_extents(nq, tq, p_len, nvar):
        @pl.when(jnp.logical_and(qi >= lo, qi < hi))
        def _():
            body(ext)


def _attn_a_kernel(iq_ref, wq_ref, ikw_ref, tq_ref, tk_ref, aq_ref, ak_ref, av_ref, bias_ref, g_ref,
                   o_ref, key_ref, mask_ref, qs_ref, os_ref, lo_ref, cand_ref, cnt_ref, *, tq, p_len, k_top):
    qi = pl.program_id(1)
    q0 = qi * tq
    half = IDX_ROPE // 2
    for h in range(A_HEADS):
        qs_ref[h * tq:(h + 1) * tq, :] = aq_ref[:, h * A_HDIM:(h + 1) * A_HDIM]

    def body(ext):
        ik = _rope128(ikw_ref[:ext, :], tk_ref[0, :ext, :], tk_ref[1, :ext, :], tk_ref[2, :ext, :],
                      half)[:, :IDX_HDIM].astype(BF16)
        w = wq_ref[...][:, IDX_HDIM:IDX_HDIM + IDX_HEADS] * ((IDX_HEADS * IDX_HDIM) ** -0.5)
        tc, ts1, ts2 = tq_ref[0], tq_ref[1], tq_ref[2]
        score = jnp.zeros((tq, ext), F32)
        for c in range(IDX_HEADS * IDX_HDIM // LANES):
            chunk = _rope128(iq_ref[:, c * LANES:(c + 1) * LANES], tc, ts1, ts2, half).astype(BF16)
            for j in range(LANES // IDX_HDIM):
                h = c * (LANES // IDX_HDIM) + j
                rel = jnp.maximum(_dot_nt(chunk[:, j * IDX_HDIM:(j + 1) * IDX_HDIM], ik), 0.0)
                score = score + w[:, h:h + 1] * rel
        qpos = q0 + lax.broadcasted_iota(jnp.int32, (tq, ext), 0)
        kpos = lax.broadcasted_iota(jnp.int32, (tq, ext), 1)
        vis = jnp.where(kpos <= qpos, jnp.where(kpos >= PAD_FRONT, 1.0, 0.0), 0.0)
        score = jnp.where(vis > 0.0, score, NEG)
        bits = lax.bitcast_convert_type(score, jnp.int32)
        key_ref[:, :ext] = jnp.where(bits < 0, bits ^ jnp.int32(0x7FFFFFFF), bits)
        kf = jnp.float32(k_top)

        def count(pred):
            return jnp.sum(jnp.where(pred, 1.0, 0.0), axis=1, keepdims=True)

        hs = tq // 2

        def lane_counts(r0, cand):
            terms = [jnp.where(key_ref[r0:r0 + hs, c * LANES:(c + 1) * LANES] >= cand, 1.0, 0.0)
                     for c in range(ext // LANES)]
            while len(terms) > 1:
                terms = [terms[i] + terms[i + 1] if i + 1 < len(terms) else terms[i] for i in range(0, len(terms), 2)]
            return terms[0]

        lo_ref[...] = jnp.full((tq, LANES), -2 ** 31, jnp.int32)
        cand_ref[...] = jnp.full((hs, LANES), -2 ** 31, jnp.int32)
        cnt_ref[...] = jnp.zeros((hs, LANES), F32)

        def finish_second_half():
            tot = jnp.sum(cnt_ref[...], axis=1, keepdims=True)
            return jnp.where(tot >= kf, cand_ref[...], lo_ref[hs:, :])

        def vbody(i, carry):
            inc = jnp.left_shift(jnp.int32(1), 31 - i)
            lo_b = finish_second_half()
            lo_a = lo_ref[:hs, :]
            cand_a = lo_a + inc
            tot_a = jnp.sum(lane_counts(0, cand_a), axis=1, keepdims=True)
            cand_b = lo_b + inc
            cnt_ref[...] = lane_counts(hs, cand_b)
            cand_ref[...] = cand_b
            lo_ref[hs:, :] = lo_b
            lo_ref[:hs, :] = jnp.where(tot_a >= kf, cand_a, lo_a)
            return carry

        lax.fori_loop(0, 32, vbody, 0)
        lo_ref[hs:, :] = finish_second_half()
        thr = lo_ref[:, 0:1]
        key = key_ref[:, :ext]
        n_gt = count(key > thr)
        n_eq = count(key == thr)
        need = kf - n_gt
        n_eq_vis = jnp.sum(jnp.where(key == thr, vis, 0.0), axis=1, keepdims=True)
        mask_ref[:, :ext] = jnp.where(key >= thr, jnp.where(vis > 0.0, 0.0, NEG), NEG)
        surplus = jnp.where(n_eq_vis > 0.0, jnp.where(n_eq > need, 1.0, 0.0), 0.0)

        @pl.when(jnp.max(surplus) > 0.0)
        def _():
            nbits = max(1, (ext - 1).bit_length())
            keyv = key_ref[:, :ext]
            col = lax.broadcasted_iota(jnp.int32, (tq, ext), 1)

            def tbody(i, j):
                cand = j + jnp.left_shift(jnp.int32(1), nbits - 1 - i)
                f = jnp.sum(jnp.where(keyv == thr, jnp.where(col < cand, 1.0, 0.0), 0.0), axis=1, keepdims=True)
                return jnp.where(f < need, cand, j)

            jstar = lax.fori_loop(0, nbits, tbody, jnp.zeros((tq, 1), jnp.int32))
            chosen = jnp.where(keyv > thr, 1.0, jnp.where(keyv == thr, jnp.where(col <= jstar, 1.0, 0.0), 0.0))
            mask_ref[:, :ext] = jnp.where(chosen > 0.0, jnp.where(vis > 0.0, 0.0, NEG), NEG)

        scale2 = A_HDIM ** -0.5 * LOG2E

        def hbody(h, carry):
            r0 = pl.multiple_of(h * tq, tq)
            s = _dot_nt(qs_ref[pl.ds(r0, tq), :], ak_ref[:ext, :]) * scale2 + bias_ref[h, :, :ext] + mask_ref[:, :ext]
            m = jnp.max(s, axis=1, keepdims=True)
            e = jnp.exp2(s - m)
            l = jnp.sum(e, axis=1, keepdims=True)
            os_ref[pl.ds(r0, tq), :] = _dot(e.astype(BF16), av_ref[:ext, :]) / l
            return carry

        lax.fori_loop(0, A_HEADS, hbody, 0, unroll=2)

    _for_each_extent(qi, p_len // tq, tq, p_len, body)
    o = jnp.concatenate([os_ref[h * tq:(h + 1) * tq, :] for h in range(A_HEADS)], axis=1)
    o_ref[...] = _rms(o, g_ref[...]).astype(BF16)


def _attn_a(pf, pb, tabs_idx, bias, g, batch, p_len, tq, k_top, col):
    nq = p_len // tq
    t = batch * p_len
    kern = functools.partial(_attn_a_kernel, tq=tq, p_len=p_len, k_top=k_top)
    return pl.pallas_call(
        kern,
        grid=(batch, nq),
        in_specs=[
            pl.BlockSpec((tq, 1024), lambda b, q: (b * nq + q, col["i_q"] // 1024)),
            pl.BlockSpec((tq, LANES), lambda b, q: (b * nq + q, col["ikw"] // LANES)),
            pl.BlockSpec((p_len, LANES), lambda b, q: (b, col["ikw"] // LANES)),
            pl.BlockSpec((3, tq, LANES), lambda b, q: (0, q, 0)),
            pl.BlockSpec((3, p_len, LANES), lambda b, q: (0, 0, 0)),
            pl.BlockSpec((tq, 1024), lambda b, q: (b * nq + q, col["a_q"] // 1024)),
            pl.BlockSpec((p_len, LANES), lambda b, q: (b, col["a_k"] // LANES)),
            pl.BlockSpec((p_len, LANES), lambda b, q: (b, col["a_v"] // LANES)),
            pl.BlockSpec((A_HEADS, 1, p_len), lambda b, q: (0, 0, 0)),
            pl.BlockSpec((1, GROUP_WIDTH), lambda b, q: (0, 0)),
        ],
        out_specs=pl.BlockSpec((tq, GROUP_WIDTH), lambda b, q: (b * nq + q, 0)),
        out_shape=jax.ShapeDtypeStruct((t, GROUP_WIDTH), BF16),
        scratch_shapes=[pltpu.VMEM((tq, p_len), jnp.int32),
                        pltpu.VMEM((tq, p_len), F32),
                        pltpu.VMEM((A_HEADS * tq, A_HDIM), BF16),
                        pltpu.VMEM((A_HEADS * tq, A_HDIM), F32),
                        pltpu.VMEM((tq, LANES), jnp.int32),
                        pltpu.VMEM((tq // 2, LANES), jnp.int32),
                        pltpu.VMEM((tq // 2, LANES), F32)],
        compiler_params=_cparams(2),
        name="attn_a",
    )(pf, pf, pf, tabs_idx, tabs_idx, pb, pb, pb, bias, g.reshape(1, GROUP_WIDTH))


def _attn_b_kernel(q_ref, kp_ref, kc_ref, vp_ref, vc_ref, bias_ref, sink_ref, g_ref, o_ref):
    n = pl.program_id(1)
    grp = B_HEADS // B_KV_HEADS
    first_col = PAD_FRONT - (n - 1) * BLK
    col = lax.broadcasted_iota(jnp.int32, (grp * BLK, 2 * BLK), 1)
    pad_mask = jnp.where(col >= first_col, 0.0, NEG)
    outs = []
    for gi in range(B_KV_HEADS):
        q = jnp.concatenate([q_ref[:, (gi * grp + j) * B_HDIM:(gi * grp + j + 1) * B_HDIM] for j in range(grp)], axis=0)
        k = jnp.concatenate([kp_ref[:, gi * B_HDIM:(gi + 1) * B_HDIM], kc_ref[:, gi * B_HDIM:(gi + 1) * B_HDIM]], axis=0)
        v = jnp.concatenate([vp_ref[:, gi * B_HDIM:(gi + 1) * B_HDIM], vc_ref[:, gi * B_HDIM:(gi + 1) * B_HDIM]], axis=0)
        s = _dot_nt(q, k) + bias_ref[gi] + pad_mask
        sink = sink_ref[gi]
        m = jnp.maximum(jnp.max(s, axis=1, keepdims=True), sink)
        e = jnp.exp(s - m)
        l = jnp.sum(e, axis=1, keepdims=True) + jnp.exp(sink - m)
        o = _dot(e.astype(BF16), v) / l
        outs.extend(o[j * BLK:(j + 1) * BLK, :] for j in range(grp))
    o = jnp.concatenate(outs, axis=1)
    o_ref[...] = _rms(o, g_ref[...]).astype(BF16)


def _attn_b(pb, bias, sink_rows, g, batch, p_len, col):
    nb = p_len // BLK
    t = batch * p_len
    grp = B_HEADS // B_KV_HEADS
    return pl.pallas_call(
        _attn_b_kernel,
        grid=(batch, nb),
        in_specs=[
            pl.BlockSpec((BLK, 1024), lambda b, n: (b * nb + n, col["b_q"] // 1024)),
            pl.BlockSpec((BLK, LANES), lambda b, n: (b * nb + jnp.maximum(n - 1, 0), col["b_k"] // LANES)),
            pl.BlockSpec((BLK, LANES), lambda b, n: (b * nb + n, col["b_k"] // LANES)),
            pl.BlockSpec((BLK, LANES), lambda b, n: (b * nb + jnp.maximum(n - 1, 0), col["b_v"] // LANES)),
            pl.BlockSpec((BLK, LANES), lambda b, n: (b * nb + n, col["b_v"] // LANES)),
            pl.BlockSpec((B_KV_HEADS, grp * BLK, 2 * BLK), lambda b, n: (0, 0, 0)),
            pl.BlockSpec((B_KV_HEADS, grp * BLK, 1), lambda b, n: (0, 0, 0)),
            pl.BlockSpec((1, GROUP_WIDTH), lambda b, n: (0, 0)),
        ],
        out_specs=pl.BlockSpec((BLK, GROUP_WIDTH), lambda b, n: (b * nb + n, 0)),
        out_shape=jax.ShapeDtypeStruct((t, GROUP_WIDTH), BF16),
        compiler_params=_cparams(2),
        name="attn_b",
    )(pb, pb, pb, pb, pb, bias, sink_rows, g.reshape(1, GROUP_WIDTH))


def _c_prep_kernel(cq_ref, ckv_ref, kr_ref, tab_ref, gq_ref, gkv_ref, wq_ref, wkv_ref, q_ref, k_ref, v_ref):
    half = C_ROPE // 2
    tc, ts1, ts2 = tab_ref[0], tab_ref[1], tab_ref[2]
    slot = C_NOPE + LANES
    xq = _rms(cq_ref[...], gq_ref[...]).astype(BF16)
    q = _dot(xq, wq_ref[...])
    xkv = _rms(ckv_ref[...], gkv_ref[...]).astype(BF16)
    kv = _dot(xkv, wkv_ref[...])
    kr = _rope128(kr_ref[...], tc, ts1, ts2, half).astype(BF16)
    for h in range(C_HEADS):
        lo = h * slot
        q_ref[h, :, :C_NOPE] = q[:, lo:lo + C_NOPE].astype(BF16)
        q_ref[h, :, C_NOPE:] = _rope128(q[:, lo + C_NOPE:lo + slot], tc, ts1, ts2, half).astype(BF16)
        k_ref[h, :, :C_NOPE] = kv[:, h * C_NOPE:(h + 1) * C_NOPE].astype(BF16)
        k_ref[h, :, C_NOPE:] = kr
        v_ref[h] = kv[:, (C_HEADS + h) * C_NOPE:(C_HEADS + h + 1) * C_NOPE].astype(BF16)


def _c_prep(pf, tabs, g_cq, g_ckv, w_uq_p, w_ukv_p, p_len, tm, col):
    t = pf.shape[0]
    tiles = p_len // tm
    slot = C_NOPE + LANES
    return pl.pallas_call(
        _c_prep_kernel,
        grid=(t // tm,),
        in_specs=[
            pl.BlockSpec((tm, C_Q_RANK), lambda i: (i, col["c_cq"] // C_Q_RANK)),
            pl.BlockSpec((tm, C_KV_RANK), lambda i: (i, col["c_ckv"] // C_KV_RANK)),
            pl.BlockSpec((tm, LANES), lambda i: (i, col["c_kr"] // LANES)),
            pl.BlockSpec((3, tm, LANES), lambda i: (0, i % tiles, 0)),
            pl.BlockSpec((1, C_Q_RANK), lambda i: (0, 0)),
            pl.BlockSpec((1, C_KV_RANK), lambda i: (0, 0)),
            pl.BlockSpec((C_Q_RANK, C_HEADS * slot), lambda i: (0, 0)),
            pl.BlockSpec((C_KV_RANK, C_HEADS * (C_NOPE + C_VDIM)), lambda i: (0, 0)),
        ],
        out_specs=[pl.BlockSpec((C_HEADS, tm, slot), lambda i: (0, i, 0)),
                   pl.BlockSpec((C_HEADS, tm, slot), lambda i: (0, i, 0)),
                   pl.BlockSpec((C_HEADS, tm, C_VDIM), lambda i: (0, i, 0))],
        out_shape=[jax.ShapeDtypeStruct((C_HEADS, t, slot), BF16),
                   jax.ShapeDtypeStruct((C_HEADS, t, slot), BF16),
                   jax.ShapeDtypeStruct((C_HEADS, t, C_VDIM), BF16)],
        compiler_params=_cparams(1),
        name="c_prep",
    )(pf, pf, pf, tabs, g_cq.reshape(1, -1), g_ckv.reshape(1, -1), w_uq_p, w_ukv_p)


def _causal_mask(q0, tq, ext):
    qpos = q0 + lax.broadcasted_iota(jnp.int32, (tq, ext), 0)
    kpos = lax.broadcasted_iota(jnp.int32, (tq, ext), 1)
    return jnp.where(kpos <= qpos, jnp.where(kpos >= PAD_FRONT, 0.0, NEG), NEG)


def _attn_c_kernel(q_ref, k_ref, v_ref, g_ref, o_ref, os_ref, *, tq, p_len):
    qi = pl.program_id(1)
    scale2 = (C_NOPE + C_ROPE) ** -0.5 * LOG2E

    def body(ext):
        mask = _causal_mask(qi * tq, tq, ext)

        def hbody(h, carry):
            s = _dot_nt(q_ref[h], k_ref[h, :ext, :]) * scale2 + mask
            m = jnp.max(s, axis=1, keepdims=True)
            e = jnp.exp2(s - m)
            l = jnp.sum(e, axis=1, keepdims=True)
            os_ref[h] = _dot(e.astype(BF16), v_ref[h, :ext, :]) / l
            return carry

        lax.fori_loop(0, C_HEADS, hbody, 0, unroll=2)

    nq = p_len // tq
    _for_each_extent(qi, nq, tq, p_len, body, nq)
    o = jnp.concatenate([os_ref[h] for h in range(C_HEADS)], axis=1)
    o_ref[...] = _rms(o, g_ref[...]).astype(BF16)


def _attn_c(qc, kc, vc, g, batch, p_len, tq):
    nq = p_len // tq
    t = batch * p_len
    slot = C_NOPE + LANES
    kern = functools.partial(_attn_c_kernel, tq=tq, p_len=p_len)
    return pl.pallas_call(
        kern,
        grid=(batch, nq),
        in_specs=[pl.BlockSpec((C_HEADS, tq, slot), lambda b, q: (0, b * nq + q, 0)),
                  pl.BlockSpec((C_HEADS, p_len, slot), lambda b, q: (0, b, 0)),
                  pl.BlockSpec((C_HEADS, p_len, C_VDIM), lambda b, q: (0, b, 0)),
                  pl.BlockSpec((1, GROUP_WIDTH), lambda b, q: (0, 0))],
        out_specs=pl.BlockSpec((tq, GROUP_WIDTH), lambda b, q: (b * nq + q, 0)),
        out_shape=jax.ShapeDtypeStruct((t, GROUP_WIDTH), BF16),
        scratch_shapes=[pltpu.VMEM((C_HEADS, tq, C_VDIM), F32)],
        compiler_params=_cparams(2),
        name="attn_c",
    )(qc, kc, vc, g.reshape(1, GROUP_WIDTH))


def _attn_d_kernel(q_ref, k_ref, v_ref, bias_ref, lam_ref, g_ref, o_ref, os_ref, *, tq, p_len, lam_init):
    qi = pl.program_id(1)
    lp = lam_ref[...]
    lam = (jnp.exp(jnp.sum(lp[0:1] * lp[1:2], axis=1, keepdims=True))
           - jnp.exp(jnp.sum(lp[2:3] * lp[3:4], axis=1, keepdims=True)) + lam_init)
    vw = 2 * D_HDIM

    def body(ext):
        mask = _causal_mask(qi * tq, tq, ext)
        for h in range(D_HEADS):
            bm = bias_ref[h, :, :ext] + mask
            es, ls = [], []
            for c in range(2):
                lo = (2 * h + c) * D_HDIM
                s = _dot_nt(q_ref[:, lo:lo + D_HDIM], k_ref[:ext, lo:lo + D_HDIM]) + bm
                m = jnp.max(s, axis=1, keepdims=True)
                e = jnp.exp(s - m)
                es.append(e)
                ls.append(jnp.sum(e, axis=1, keepdims=True))
            a = es[0] * (1.0 / ls[0]) - es[1] * (lam / ls[1])
            os_ref[:, h * vw:(h + 1) * vw] = _dot(a.astype(BF16), v_ref[:ext, h * vw:(h + 1) * vw])

    _for_each_extent(qi, p_len // tq, tq, p_len, body, N_EXTENT_VARIANTS_D)
    g = g_ref[...]
    outs = [_rms(os_ref[:, h * vw:(h + 1) * vw], g) * (1.0 - lam_init) for h in range(D_HEADS)]
    o_ref[...] = jnp.concatenate(outs, axis=1).astype(BF16)


def _attn_d(pb, bias, lam_p, g, batch, p_len, tq, lam_init, col):
    nq = p_len // tq
    t = batch * p_len
    kern = functools.partial(_attn_d_kernel, tq=tq, p_len=p_len, lam_init=lam_init)
    return pl.pallas_call(
        kern,
        grid=(batch, nq),
        in_specs=[pl.BlockSpec((tq, 1024), lambda b, q: (b * nq + q, col["d_q"] // 1024)),
                  pl.BlockSpec((p_len, 1024), lambda b, q: (b, col["d_k"] // 1024)),
                  pl.BlockSpec((p_len, 1024), lambda b, q: (b, col["d_v"] // 1024)),
                  pl.BlockSpec((D_HEADS, 1, p_len), lambda b, q: (0, 0, 0)),
                  pl.BlockSpec((4, D_HDIM), lambda b, q: (0, 0)),
                  pl.BlockSpec((1, 2 * D_HDIM), lambda b, q: (0, 0))],
        out_specs=pl.BlockSpec((tq, GROUP_WIDTH), lambda b, q: (b * nq + q, 0)),
        out_shape=jax.ShapeDtypeStruct((t, GROUP_WIDTH), BF16),
        scratch_shapes=[pltpu.VMEM((tq, GROUP_WIDTH), F32)],
        compiler_params=_cparams(2),
        name="attn_d",
    )(pb, pb, pb, bias, lam_p, g.reshape(1, 2 * D_HDIM))


def _ffn_up_kernel(x_ref, wg_ref, wu_ref, cwg_ref, cwu_ref, cbg_ref, cbu_ref, o_ref,
                   wgb_ref, wub_ref, ha_ref, hb_ref, carry_ref, *, tm, n_row_tiles, tiles_per_seq):
    s = pl.program_id(0)
    d = x_ref.shape[1]
    kc = min(d, MXU_DEPTH)
    nk = d // kc
    rc = next(r for r in (64, 32, 16, 8) if tm % r == 0)
    nr = tm // rc

    @pl.when(s % n_row_tiles == 0)
    def _():
        wgb_ref[...] = wg_ref[...].astype(BF16)
        wub_ref[...] = wu_ref[...].astype(BF16)

    @pl.when(s == 0)
    def _():
        hb_ref[...] = jnp.zeros_like(hb_ref)

    @pl.when((s + tiles_per_seq - 1) % tiles_per_seq == 0)
    def _():
        carry_ref[...] = jnp.zeros_like(carry_ref)

    def run(rd_ref, wr_ref):
        cws = (cwg_ref[...], cwu_ref[...])
        cbs = (cbg_ref[...], cbu_ref[...])
        for step in range(nr):
            for kk in range((step * nk) // nr, ((step + 1) * nk) // nr):
                for b, wb_ref in enumerate((wgb_ref, wub_ref)):
                    part = _dot(x_ref[:, kk * kc:(kk + 1) * kc], wb_ref[kk * kc:(kk + 1) * kc, :])
                    if kk == 0:
                        wr_ref[b] = part
                    else:
                        wr_ref[b] = wr_ref[b] + part
            r0 = step * rc
            ys = []
            for b in range(2):
                top = carry_ref[b] if step == 0 else rd_ref[b, r0 - 8:r0, :]
                cur = rd_ref[b, r0:r0 + rc, :]
                full = jnp.concatenate([top, cur], axis=0)
                cw = cws[b]
                y = cw[2:3] * cur + cw[1:2] * pltpu.roll(full, 1, 0)[8:] + cw[0:1] * pltpu.roll(full, 2, 0)[8:]
                ys.append(y + cbs[b])
            o_ref[r0:r0 + rc, :] = (ys[0] * jax.nn.sigmoid(ys[0]) * ys[1]).astype(BF16)
        for b in range(2):
            carry_ref[b] = rd_ref[b, tm - 8:, :]

    @pl.when(s % 2 == 0)
    def _():
        run(hb_ref, ha_ref)

    @pl.when(s % 2 == 1)
    def _():
        run(ha_ref, hb_ref)


def _ffn_up(hb, w_up, conv_w, conv_b, layer, p_len, tm, tn):
    t, d = hb.shape
    f = w_up.shape[2] // 2
    assert f % tn == 0 and tm % 8 == 0 and p_len % tm == 0
    nj, nt = f // tn, t // tm
    n_tiles = nj * nt
    kern = functools.partial(_ffn_up_kernel, tm=tm, n_row_tiles=nt, tiles_per_seq=p_len // tm)
    cb = conv_b.reshape(conv_b.shape[0], 1, 2 * f)
    cur = lambda s: jnp.minimum(s, n_tiles - 1)
    prv = lambda s: jnp.maximum(s - 1, 0)
    return pl.pallas_call(
        kern,
        grid=(n_tiles + 1,),
        in_specs=[pl.BlockSpec((tm, d), lambda s: (cur(s) % nt, 0)),
                  pl.BlockSpec((None, d, tn), lambda s: (layer, 0, cur(s) // nt)),
                  pl.BlockSpec((None, d, tn), lambda s: (layer, 0, nj + cur(s) // nt)),
                  pl.BlockSpec((None, CONV_W, tn), lambda s: (layer, 0, prv(s) // nt)),
                  pl.BlockSpec((None, CONV_W, tn), lambda s: (layer, 0, nj + prv(s) // nt)),
                  pl.BlockSpec((None, 1, tn), lambda s: (layer, 0, prv(s) // nt)),
                  pl.BlockSpec((None, 1, tn), lambda s: (layer, 0, nj + prv(s) // nt))],
        out_specs=pl.BlockSpec((tm, tn), lambda s: (prv(s) % nt, prv(s) // nt)),
        out_shape=jax.ShapeDtypeStruct((t, f), BF16),
        scratch_shapes=[pltpu.VMEM((d, tn), BF16), pltpu.VMEM((d, tn), BF16),
                        pltpu.VMEM((2, tm, tn), F32), pltpu.VMEM((2, tm, tn), F32), pltpu.VMEM((2, 8, tn), F32)],
        compiler_params=_cparams(1),
        name="ffn_up",
    )(hb, w_up, w_up, conv_w, conv_w, cb, cb)


def _offsets(names, sizes):
    out, o = {}, 0
    for n, s in zip(names, sizes):
        out[n] = o
        o += s
    return out, o


_BF_NAMES = ("a_q", "b_q", "d_q", "d_k", "d_v", "a_k", "a_v", "b_k", "b_v")
_BF_SIZES = (1024, 1024, 1024, 1024, 1024, 128, 128, 128, 128)
_F32_NAMES = ("i_q", "c_cq", "c_ckv", "ikw", "c_kr")
_F32_SIZES = (1024, 1024, 512, 128, 128)
COL_BF, N_BF = _offsets(_BF_NAMES, _BF_SIZES)
COL_F32, N_F32 = _offsets(_F32_NAMES, _F32_SIZES)


def _prep_w_in(w):
    d = w.shape[0]
    seg = dict(zip(("a_q", "a_k", "a_v", "i_q", "i_k", "i_w", "b_q", "b_k", "b_v", "c_cq", "c_ckv", "c_kr",
                    "d_q", "d_k", "d_v"), jnp.split(w, np.cumsum(IN_SIZES)[:-1].tolist(), axis=1)))
    seg["b_q"] = seg["b_q"] * (B_HDIM ** -0.5)
    seg["d_q"] = seg["d_q"] * (D_HDIM ** -0.5)
    wb = jnp.concatenate([seg[n] for n in _BF_NAMES], axis=1).astype(BF16)
    zeros = lambda n: jnp.zeros((d, n), w.dtype)
    ikw = jnp.concatenate([seg["i_k"], seg["i_w"], zeros(LANES - IDX_HDIM - IDX_HEADS)], axis=1)
    ckr = jnp.concatenate([seg["c_kr"], zeros(LANES - C_ROPE)], axis=1)
    wf = jnp.concatenate([seg["i_q"], seg["c_cq"], seg["c_ckv"], ikw, ckr], axis=1).astype(BF16)
    return wb, wf


def _prep_w_uq(w):
    r = w.shape[0]
    w3 = w.reshape(r, C_HEADS, C_NOPE + C_ROPE)
    return jnp.pad(w3, ((0, 0), (0, 0), (0, LANES - C_ROPE))).reshape(r, C_HEADS * (C_NOPE + LANES)).astype(BF16)


def _prep_w_ukv(w):
    r = w.shape[0]
    w3 = w.reshape(r, C_HEADS, C_NOPE + C_VDIM)
    return jnp.concatenate([w3[:, :, :C_NOPE].reshape(r, -1), w3[:, :, C_NOPE:].reshape(r, -1)], axis=1).astype(BF16)


def _alibi(n):
    return 2.0 ** (-8.0 * np.arange(1, n + 1, dtype=np.float64) / n)


def _swa_bias():
    grp = B_HEADS // B_KV_HEADS
    r = np.arange(BLK)[:, None]
    c = np.arange(2 * BLK)[None, :]
    diff = (r + BLK - c).astype(np.float64)
    ok = (diff >= 0) & (diff < WINDOW)
    slopes = _alibi(B_HEADS).reshape(B_KV_HEADS, grp)
    bias = np.where(ok[None, None], -slopes[:, :, None, None] * diff[None, None], NEG)
    return jnp.asarray(bias.reshape(B_KV_HEADS, grp * BLK, 2 * BLK), F32)


def kernel(x, meta_tokens, ln_in_g, ln_in_b, w_in, g_cq, g_ckv, w_uq, w_ukv, sinks, lam_q1, lam_k1, lam_q2, lam_k2,
           g_diff, g_grp, w_out, ln1_g, ln1_b, w_up, conv_w, conv_b, w_down, ln2_g, ln2_b):
    batch, s_len, d = x.shape
    depth = w_in.shape[0]
    p_len = s_len + BLK
    t = batch * p_len
    k_top = min(TOPK_MAX, s_len // 4)
    alpha = (2 * depth) ** 0.25
    tm = _row_tile(p_len, 544)
    tm_ln = _row_tile(p_len, 272)
    tm_ffn = _row_tile(p_len, 1088)
    tq = BLK
    grp = B_HEADS // B_KV_HEADS

    h, hb = _ln_in(x, meta_tokens, ln_in_g, ln_in_b)

    pos = jnp.arange(p_len, dtype=jnp.int32) - PAD_FRONT
    tabs_idx = _rope_tables(pos, IDX_ROPE, IDX_HDIM)
    tabs_c = _rope_tables(pos, C_ROPE, LANES)
    kidx = np.arange(p_len, dtype=np.float64)
    bias_a = jnp.asarray(_alibi(A_HEADS)[:, None, None] * kidx[None, None, :] * LOG2E, F32)
    bias_d = jnp.asarray(_alibi(D_HEADS)[:, None, None] * kidx[None, None, :], F32)
    bias_b = _swa_bias()

    for l in range(depth):
        wb, wf = _prep_w_in(w_in[l])
        pb = _mm(hb, wb, tm_ffn, 2 * MXU_DEPTH, BF16, "proj_bf16")
        pf = _mm(hb, wf, tm_ffn, MXU_DEPTH, F32, "proj_f32")

        o_a = _attn_a(pf, pb, tabs_idx, bias_a, g_grp[l, 0], batch, p_len, tq, k_top, {**COL_F32, **COL_BF})
        sink_rows = jnp.repeat(sinks[l].astype(F32).reshape(B_KV_HEADS, grp), BLK, axis=1)[..., None]
        o_b = _attn_b(pb, bias_b, sink_rows, g_grp[l, 1], batch, p_len, COL_BF)
        qc, kc, vc = _c_prep(pf, tabs_c, g_cq[l], g_ckv[l], _prep_w_uq(w_uq[l]), _prep_w_ukv(w_ukv[l]),
                               p_len, tm_ln, COL_F32)
        o_c = _attn_c(qc, kc, vc, g_grp[l, 2], batch, p_len, tm_ln)
        lam_init = 0.8 - 0.6 * math.exp(-0.3 * l)
        lam_p = jnp.stack([lam_q1[l], lam_k1[l], lam_q2[l], lam_k2[l]]).astype(F32)
        o_d = _attn_d(pb, bias_d, lam_p, g_diff[l], batch, p_len, tq, lam_init, COL_BF)

        mix_in = jnp.concatenate([o_a, o_b, o_c, o_d], axis=1)
        mix = _mm(mix_in, w_out[l].astype(BF16), tm, min(1024, d), F32, "out_proj")
        h, hb_ffn = _res_ln(h, mix, ln1_g[l], ln1_b[l], alpha, tm_ln, p_len, True)

        act = _ffn_up(hb_ffn, w_up, conv_w, conv_b, l, p_len, tm_ffn, FF_TILE)
        ffn = _mm(act, w_down[l].astype(BF16), tm, min(512, d), F32, "ffn_down")
        if l + 1 < depth:
            h, hb = _res_ln(h, ffn, ln2_g[l], ln2_b[l], alpha, tm_ln, p_len, False)
        else:
            out = _res_ln_final(h, ffn, ln2_g[l], ln2_b[l], alpha, batch, p_len)

    return out.reshape(batch, s_len, d)
```

```python
import functools
import math

import jax
import jax.numpy as jnp
import numpy as np
from jax import lax
from jax.experimental import pallas as pl
from jax.experimental.pallas import tpu as pltpu

N_META = 16
BLK = 128
PAD_FRONT = BLK - N_META
A_HEADS, A_HDIM = 8, 128
IDX_HEADS, IDX_HDIM, IDX_ROPE = 16, 64, 32
TOPK_MAX = 256
B_HEADS, B_KV_HEADS, B_HDIM = 16, 2, 64
WINDOW = 128
C_HEADS, C_Q_RANK, C_KV_RANK, C_NOPE, C_ROPE, C_VDIM = 8, 1024, 512, 128, 64, 128
D_HEADS, D_HDIM = 8, 64
GROUP_WIDTH = 1024
CONV_W = 3
ROPE_BASE = 10000.0
NEG = -1e30
IN_SIZES = (1024, 128, 128, 1024, 64, 16, 1024, 128, 128, 1024, 512, 64, 1024, 1024, 1024)

LANES = 128
BF16_SUBLANES = 16
MXU_DEPTH = 256
VMEM_LIMIT = 56 * 1024 * 1024
FF_TILE = 256
N_EXTENT_VARIANTS = 6
LOG2E = 1.4426950408889634

F32 = jnp.float32
BF16 = jnp.bfloat16


def _cparams(n_axes):
    return pltpu.CompilerParams(dimension_semantics=("arbitrary",) * n_axes, vmem_limit_bytes=VMEM_LIMIT)


def _row_tile(p_len, target):
    best = BF16_SUBLANES
    for t in range(BF16_SUBLANES, target + 1, BF16_SUBLANES):
        if p_len % t == 0:
            best = t
    return best


def _dot(a, b):
    return jnp.dot(a, b, preferred_element_type=F32)


def _dot_nt(a, b):
    return lax.dot_general(a, b, (((1,), (1,)), ((), ())), preferred_element_type=F32)


def _ln_rows(x, g, b):
    mu = jnp.mean(x, axis=-1, keepdims=True)
    xc = x - mu
    var = jnp.mean(xc * xc, axis=-1, keepdims=True)
    return xc * lax.rsqrt(var + 1e-5) * g + b


def _ln_in_kernel(x_ref, meta_ref, g_ref, b_ref, h_ref, hb_ref):
    def emit(rows):
        y = _ln_rows(rows, g_ref[...], b_ref[...])
        h_ref[...] = y
        hb_ref[...] = y.astype(BF16)

    @pl.when(pl.program_id(1) == 0)
    def _():
        emit(jnp.concatenate([jnp.zeros((PAD_FRONT, meta_ref.shape[1]), F32), meta_ref[...]], axis=0))

    @pl.when(pl.program_id(1) > 0)
    def _():
        emit(x_ref[...])


def _ln_in(x, meta, g, b):
    batch, s_len, d = x.shape
    nb = s_len // BLK + 1
    t = batch * nb * BLK
    return pl.pallas_call(
        _ln_in_kernel,
        grid=(batch, nb),
        in_specs=[pl.BlockSpec((None, BLK, d), lambda bi, n: (bi, jnp.maximum(n - 1, 0), 0)),
                  pl.BlockSpec((N_META, d), lambda bi, n: (0, 0)),
                  pl.BlockSpec((1, d), lambda bi, n: (0, 0)),
                  pl.BlockSpec((1, d), lambda bi, n: (0, 0))],
        out_specs=[pl.BlockSpec((BLK, d), lambda bi, n: (bi * nb + n, 0)),
                   pl.BlockSpec((BLK, d), lambda bi, n: (bi * nb + n, 0))],
        out_shape=[jax.ShapeDtypeStruct((t, d), F32), jax.ShapeDtypeStruct((t, d), BF16)],
        compiler_params=_cparams(2),
        name="ln_in",
    )(x, meta.astype(F32), g.reshape(1, d), b.reshape(1, d))


def _res_ln_kernel(h_ref, y_ref, g_ref, b_ref, o_ref, ob_ref, *, alpha, tm, tiles_per_seq, zero_pad):
    out = _ln_rows(alpha * h_ref[...] + y_ref[...], g_ref[...], b_ref[...])
    o_ref[...] = out
    if zero_pad:
        p0 = (pl.program_id(0) % tiles_per_seq) * tm
        pos = p0 + lax.broadcasted_iota(jnp.int32, out.shape, 0)
        out = jnp.where(pos >= PAD_FRONT, out, 0.0)
    ob_ref[...] = out.astype(BF16)


def _res_ln(h, y, g, b, alpha, tm, p_len, zero_pad):
    t, d = h.shape
    kern = functools.partial(_res_ln_kernel, alpha=alpha, tm=tm, tiles_per_seq=p_len // tm, zero_pad=zero_pad)
    return pl.pallas_call(
        kern,
        grid=(t // tm,),
        in_specs=[pl.BlockSpec((tm, d), lambda i: (i, 0)),
                  pl.BlockSpec((tm, d), lambda i: (i, 0)),
                  pl.BlockSpec((1, d), lambda i: (0, 0)),
                  pl.BlockSpec((1, d), lambda i: (0, 0))],
        out_specs=[pl.BlockSpec((tm, d), lambda i: (i, 0)),
                   pl.BlockSpec((tm, d), lambda i: (i, 0))],
        out_shape=[jax.ShapeDtypeStruct((t, d), F32), jax.ShapeDtypeStruct((t, d), BF16)],
        compiler_params=_cparams(1),
        name="res_ln",
    )(h, y, g.reshape(1, d), b.reshape(1, d))


def _res_ln_final_kernel(h_ref, y_ref, g_ref, b_ref, o_ref, *, alpha):
    o_ref[...] = _ln_rows(alpha * h_ref[...] + y_ref[...], g_ref[...], b_ref[...])


def _res_ln_final(h, y, g, b, alpha, batch, p_len):
    t, d = h.shape
    nb = p_len // BLK
    out_idx = lambda i: ((i // nb) * (nb - 1) + jnp.maximum(i % nb - 1, 0), 0)
    return pl.pallas_call(
        functools.partial(_res_ln_final_kernel, alpha=alpha),
        grid=(t // BLK,),
        in_specs=[pl.BlockSpec((BLK, d), lambda i: (i, 0)),
                  pl.BlockSpec((BLK, d), lambda i: (i, 0)),
                  pl.BlockSpec((1, d), lambda i: (0, 0)),
                  pl.BlockSpec((1, d), lambda i: (0, 0))],
        out_specs=pl.BlockSpec((BLK, d), out_idx),
        out_shape=jax.ShapeDtypeStruct((batch * (p_len - BLK), d), F32),
        compiler_params=_cparams(1),
        name="res_ln_final",
    )(h, y, g.reshape(1, d), b.reshape(1, d))


def _mm_kernel(a_ref, w_ref, o_ref):
    o_ref[...] = _dot(a_ref[...], w_ref[...]).astype(o_ref.dtype)


def _mm(a, w, tm, tn, out_dtype, name):
    t, k = a.shape
    n = w.shape[1]
    return pl.pallas_call(
        _mm_kernel,
        grid=(n // tn, t // tm),
        in_specs=[pl.BlockSpec((tm, k), lambda j, i: (i, 0)),
                  pl.BlockSpec((k, tn), lambda j, i: (0, j))],
        out_specs=pl.BlockSpec((tm, tn), lambda j, i: (i, j)),
        out_shape=jax.ShapeDtypeStruct((t, n), out_dtype),
        compiler_params=_cparams(2),
        name=name,
    )(a, w)


def _mm_slab_kernel(a_ref, w_ref, o_ref):
    res = _dot(a_ref[...], w_ref[...])
    for c in range(o_ref.shape[0]):
        o_ref[c] = res[:, c * LANES:(c + 1) * LANES].astype(o_ref.dtype)


def _mm_slab(a, w, tm, tn, out_dtype, name):
    t, k = a.shape
    n = w.shape[1]
    return pl.pallas_call(
        _mm_slab_kernel,
        grid=(n // tn, t // tm),
        in_specs=[pl.BlockSpec((tm, k), lambda j, i: (i, 0)),
                  pl.BlockSpec((k, tn), lambda j, i: (0, j))],
        out_specs=pl.BlockSpec((tn // LANES, tm, LANES), lambda j, i: (j, i, 0)),
        out_shape=jax.ShapeDtypeStruct((n // LANES, t, LANES), out_dtype),
        compiler_params=_cparams(2),
        name=name,
    )(a, w)


def _rope128(x, c, s1, s2, half):
    return x * c + pltpu.roll(x, LANES - half, 1) * s1 + pltpu.roll(x, half, 1) * s2


def _rope_tables(pos, rot_dim, period):
    half = rot_dim // 2
    inv = ROPE_BASE ** (-jnp.arange(half, dtype=F32) / half)
    ang = pos.astype(F32)[:, None] * inv[None]
    cos, sin = jnp.cos(ang), jnp.sin(ang)
    n = pos.shape[0]
    lane = np.arange(LANES) % period
    li = lane % half
    cos_l, sin_l = cos[:, li], sin[:, li]
    first = jnp.asarray(lane < half)[None]
    second = jnp.asarray((lane >= half) & (lane < rot_dim))[None]
    c = jnp.where(first | second, cos_l, 1.0)
    s1 = jnp.where(first, -sin_l, 0.0)
    s2 = jnp.where(second, sin_l, 0.0)
    return jnp.stack([c, s1, s2]).astype(F32).reshape(3, n, LANES)


def _rms(x, g, eps=1e-6):
    return x * lax.rsqrt(jnp.mean(x * x, axis=-1, keepdims=True) + eps) * g


def _key_extents(nq, tq, p_len, nvar):
    step = -(-nq // nvar)
    out = []
    for lo in range(0, nq, step):
        hi = min(lo + step, nq)
        out.append((lo, hi, min(p_len, -(-(hi * tq) // BLK) * BLK)))
    return out


def _for_each_extent(qi, nq, tq, p_len, body, nvar=N_EXTENT_VARIANTS):
    for lo, hi, ext in _key_extents(nq, tq, p_len, nvar):
        @pl.when(jnp.logical_and(qi >= lo, qi < hi))
        def _():
            body(ext)


def _attn_a_kernel(iq_ref, wq_ref, ikw_ref, tq_ref, tk_ref, aq_ref, ak_ref, av_ref, bias_ref, g_ref,
                   o_ref, key_ref, mask_ref, qs_ref, os_ref, lo_ref, cand_ref, cnt_ref, *, tq, p_len, k_top):
    qi = pl.program_id(1)
    q0 = qi * tq
    half = IDX_ROPE // 2
    for h in range(A_HEADS):
        qs_ref[h * tq:(h + 1) * tq, :] = aq_ref[:, h * A_HDIM:(h + 1) * A_HDIM]

    def body(ext):
        ik = _rope128(ikw_ref[:ext, :], tk_ref[0, :ext, :], tk_ref[1, :ext, :], tk_ref[2, :ext, :],
                      half)[:, :IDX_HDIM].astype(BF16)
        w = wq_ref[...][:, IDX_HDIM:IDX_HDIM + IDX_HEADS] * ((IDX_HEADS * IDX_HDIM) ** -0.5)
        tc, ts1, ts2 = tq_ref[0], tq_ref[1], tq_ref[2]
        score = jnp.zeros((tq, ext), F32)
        for c in range(IDX_HEADS * IDX_HDIM // LANES):
            chunk = _rope128(iq_ref[:, c * LANES:(c + 1) * LANES], tc, ts1, ts2, half).astype(BF16)
            for j in range(LANES // IDX_HDIM):
                h = c * (LANES // IDX_HDIM) + j
                rel = jnp.maximum(_dot_nt(chunk[:, j * IDX_HDIM:(j + 1) * IDX_HDIM], ik), 0.0)
                score = score + w[:, h:h + 1] * rel
        qpos = q0 + lax.broadcasted_iota(jnp.int32, (tq, ext), 0)
        kpos = lax.broadcasted_iota(jnp.int32, (tq, ext), 1)
        vis = jnp.where(kpos <= qpos, jnp.where(kpos >= PAD_FRONT, 1.0, 0.0), 0.0)
        score = jnp.where(vis > 0.0, score, NEG)
        bits = lax.bitcast_convert_type(score, jnp.int32)
        key_ref[:, :ext] = jnp.where(bits < 0, bits ^ jnp.int32(0x7FFFFFFF), bits)
        kf = jnp.float32(k_top)

        def count(pred):
            return jnp.sum(jnp.where(pred, 1.0, 0.0), axis=1, keepdims=True)

        hs = tq // 2

        def lane_counts(r0, cand):
            terms = [jnp.where(key_ref[r0:r0 + hs, c * LANES:(c + 1) * LANES] >= cand, 1.0, 0.0)
                     for c in range(ext // LANES)]
            while len(terms) > 1:
                terms = [terms[i] + terms[i + 1] if i + 1 < len(terms) else terms[i] for i in range(0, len(terms), 2)]
            return terms[0]

        lo_ref[...] = jnp.full((tq, LANES), -2 ** 31, jnp.int32)
        cand_ref[...] = jnp.full((hs, LANES), -2 ** 31, jnp.int32)
        cnt_ref[...] = jnp.zeros((hs, LANES), F32)

        def finish_second_half():
            tot = jnp.sum(cnt_ref[...], axis=1, keepdims=True)
            return jnp.where(tot >= kf, cand_ref[...], lo_ref[hs:, :])

        def vbody(i, carry):
            inc = jnp.left_shift(jnp.int32(1), 31 - i)
            lo_b = finish_second_half()
            lo_a = lo_ref[:hs, :]
            cand_a = lo_a + inc
            tot_a = jnp.sum(lane_counts(0, cand_a), axis=1, keepdims=True)
            cand_b = lo_b + inc
            cnt_ref[...] = lane_counts(hs, cand_b)
            cand_ref[...] = cand_b
            lo_ref[hs:, :] = lo_b
            lo_ref[:hs, :] = jnp.where(tot_a >= kf, cand_a, lo_a)
            return carry

        lax.fori_loop(0, 32, vbody, 0)
        lo_ref[hs:, :] = finish_second_half()
        thr = lo_ref[:, 0:1]
        key = key_ref[:, :ext]
        n_gt = count(key > thr)
        n_eq = count(key == thr)
        need = kf - n_gt
        n_eq_vis = jnp.sum(jnp.where(key == thr, vis, 0.0), axis=1, keepdims=True)
        mask_ref[:, :ext] = jnp.where(key >= thr, jnp.where(vis > 0.0, 0.0, NEG), NEG)
        surplus = jnp.where(n_eq_vis > 0.0, jnp.where(n_eq > need, 1.0, 0.0), 0.0)

        @pl.when(jnp.max(surplus) > 0.0)
        def _():
            nbits = max(1, (ext - 1).bit_length())
            keyv = key_ref[:, :ext]
            col = lax.broadcasted_iota(jnp.int32, (tq, ext), 1)

            def tbody(i, j):
                cand = j + jnp.left_shift(jnp.int32(1), nbits - 1 - i)
                f = jnp.sum(jnp.where(keyv == thr, jnp.where(col < cand, 1.0, 0.0), 0.0), axis=1, keepdims=True)
                return jnp.where(f < need, cand, j)

            jstar = lax.fori_loop(0, nbits, tbody, jnp.zeros((tq, 1), jnp.int32))
            chosen = jnp.where(keyv > thr, 1.0, jnp.where(keyv == thr, jnp.where(col <= jstar, 1.0, 0.0), 0.0))
            mask_ref[:, :ext] = jnp.where(chosen > 0.0, jnp.where(vis > 0.0, 0.0, NEG), NEG)

        scale2 = A_HDIM ** -0.5 * LOG2E

        def hbody(h, carry):
            r0 = pl.multiple_of(h * tq, tq)
            s = _dot_nt(qs_ref[pl.ds(r0, tq), :], ak_ref[:ext, :]) * scale2 + bias_ref[h, :, :ext] + mask_ref[:, :ext]
            m = jnp.max(s, axis=1, keepdims=True)
            e = jnp.exp2(s - m)
            l = jnp.sum(e, axis=1, keepdims=True)
            os_ref[pl.ds(r0, tq), :] = _dot(e.astype(BF16), av_ref[:ext, :]) / l
            return carry

        lax.fori_loop(0, A_HEADS, hbody, 0, unroll=2)

    _for_each_extent(qi, p_len // tq, tq, p_len, body)
    o = jnp.concatenate([os_ref[h * tq:(h + 1) * tq, :] for h in range(A_HEADS)], axis=1)
    o_ref[...] = _rms(o, g_ref[...]).astype(BF16)


def _attn_a(pf, pb, tabs_idx, bias, g, batch, p_len, tq, k_top, col):
    nq = p_len // tq
    t = batch * p_len
    kern = functools.partial(_attn_a_kernel, tq=tq, p_len=p_len, k_top=k_top)
    return pl.pallas_call(
        kern,
        grid=(batch, nq),
        in_specs=[
            pl.BlockSpec((tq, 1024), lambda b, q: (b * nq + q, col["i_q"] // 1024)),
            pl.BlockSpec((tq, LANES), lambda b, q: (b * nq + q, col["ikw"] // LANES)),
            pl.BlockSpec((p_len, LANES), lambda b, q: (b, col["ikw"] // LANES)),
            pl.BlockSpec((3, tq, LANES), lambda b, q: (0, q, 0)),
            pl.BlockSpec((3, p_len, LANES), lambda b, q: (0, 0, 0)),
            pl.BlockSpec((tq, 1024), lambda b, q: (b * nq + q, col["a_q"] // 1024)),
            pl.BlockSpec((p_len, LANES), lambda b, q: (b, col["a_k"] // LANES)),
            pl.BlockSpec((p_len, LANES), lambda b, q: (b, col["a_v"] // LANES)),
            pl.BlockSpec((A_HEADS, 1, p_len), lambda b, q: (0, 0, 0)),
            pl.BlockSpec((1, GROUP_WIDTH), lambda b, q: (0, 0)),
        ],
        out_specs=pl.BlockSpec((tq, GROUP_WIDTH), lambda b, q: (b * nq + q, 0)),
        out_shape=jax.ShapeDtypeStruct((t, GROUP_WIDTH), BF16),
        scratch_shapes=[pltpu.VMEM((tq, p_len), jnp.int32),
                        pltpu.VMEM((tq, p_len), F32),
                        pltpu.VMEM((A_HEADS * tq, A_HDIM), BF16),
                        pltpu.VMEM((A_HEADS * tq, A_HDIM), F32),
                        pltpu.VMEM((tq, LANES), jnp.int32),
                        pltpu.VMEM((tq // 2, LANES), jnp.int32),
                        pltpu.VMEM((tq // 2, LANES), F32)],
        compiler_params=_cparams(2),
        name="attn_a",
    )(pf, pf, pf, tabs_idx, tabs_idx, pb, pb, pb, bias, g.reshape(1, GROUP_WIDTH))


def _attn_b_kernel(q_ref, kp_ref, kc_ref, vp_ref, vc_ref, bias_ref, sink_ref, g_ref, o_ref):
    n = pl.program_id(1)
    grp = B_HEADS // B_KV_HEADS
    first_col = PAD_FRONT - (n - 1) * BLK
    col = lax.broadcasted_iota(jnp.int32, (grp * BLK, 2 * BLK), 1)
    pad_mask = jnp.where(col >= first_col, 0.0, NEG)
    outs = []
    for gi in range(B_KV_HEADS):
        q = jnp.concatenate([q_ref[:, (gi * grp + j) * B_HDIM:(gi * grp + j + 1) * B_HDIM] for j in range(grp)], axis=0)
        k = jnp.concatenate([kp_ref[:, gi * B_HDIM:(gi + 1) * B_HDIM], kc_ref[:, gi * B_HDIM:(gi + 1) * B_HDIM]], axis=0)
        v = jnp.concatenate([vp_ref[:, gi * B_HDIM:(gi + 1) * B_HDIM], vc_ref[:, gi * B_HDIM:(gi + 1) * B_HDIM]], axis=0)
        s = _dot_nt(q, k) + bias_ref[gi] + pad_mask
        sink = sink_ref[gi]
        m = jnp.maximum(jnp.max(s, axis=1, keepdims=True), sink)
        e = jnp.exp(s - m)
        l = jnp.sum(e, axis=1, keepdims=True) + jnp.exp(sink - m)
        o = _dot(e.astype(BF16), v) / l
        outs.extend(o[j * BLK:(j + 1) * BLK, :] for j in range(grp))
    o = jnp.concatenate(outs, axis=1)
    o_ref[...] = _rms(o, g_ref[...]).astype(BF16)


def _attn_b(pb, bias, sink_rows, g, batch, p_len, col):
    nb = p_len // BLK
    t = batch * p_len
    grp = B_HEADS // B_KV_HEADS
    return pl.pallas_call(
        _attn_b_kernel,
        grid=(batch, nb),
        in_specs=[
            pl.BlockSpec((BLK, 1024), lambda b, n: (b * nb + n, col["b_q"] // 1024)),
            pl.BlockSpec((BLK, LANES), lambda b, n: (b * nb + jnp.maximum(n - 1, 0), col["b_k"] // LANES)),
            pl.BlockSpec((BLK, LANES), lambda b, n: (b * nb + n, col["b_k"] // LANES)),
            pl.BlockSpec((BLK, LANES), lambda b, n: (b * nb + jnp.maximum(n - 1, 0), col["b_v"] // LANES)),
            pl.BlockSpec((BLK, LANES), lambda b, n: (b * nb + n, col["b_v"] // LANES)),
            pl.BlockSpec((B_KV_HEADS, grp * BLK, 2 * BLK), lambda b, n: (0, 0, 0)),
            pl.BlockSpec((B_KV_HEADS, grp * BLK, 1), lambda b, n: (0, 0, 0)),
            pl.BlockSpec((1, GROUP_WIDTH), lambda b, n: (0, 0)),
        ],
        out_specs=pl.BlockSpec((BLK, GROUP_WIDTH), lambda b, n: (b * nb + n, 0)),
        out_shape=jax.ShapeDtypeStruct((t, GROUP_WIDTH), BF16),
        compiler_params=_cparams(2),
        name="attn_b",
    )(pb, pb, pb, pb, pb, bias, sink_rows, g.reshape(1, GROUP_WIDTH))


def _c_prep_kernel(cq_ref, ckv_ref, kr_ref, tab_ref, gq_ref, gkv_ref, wq_ref, wkv_ref, q_ref, k_ref, v_ref):
    half = C_ROPE // 2
    tc, ts1, ts2 = tab_ref[0], tab_ref[1], tab_ref[2]
    slot = C_NOPE + LANES
    xq = _rms(cq_ref[...], gq_ref[...]).astype(BF16)
    q = _dot(xq, wq_ref[...])
    xkv = _rms(ckv_ref[...], gkv_ref[...]).astype(BF16)
    kv = _dot(xkv, wkv_ref[...])
    kr = _rope128(kr_ref[...], tc, ts1, ts2, half).astype(BF16)
    for h in range(C_HEADS):
        lo = h * slot
        q_ref[h, :, :C_NOPE] = q[:, lo:lo + C_NOPE].astype(BF16)
        q_ref[h, :, C_NOPE:] = _rope128(q[:, lo + C_NOPE:lo + slot], tc, ts1, ts2, half).astype(BF16)
        k_ref[h, :, :C_NOPE] = kv[:, h * C_NOPE:(h + 1) * C_NOPE].astype(BF16)
        k_ref[h, :, C_NOPE:] = kr
        v_ref[h] = kv[:, (C_HEADS + h) * C_NOPE:(C_HEADS + h + 1) * C_NOPE].astype(BF16)


def _c_prep(pf, tabs, g_cq, g_ckv, w_uq_p, w_ukv_p, p_len, tm, col):
    t = pf.shape[0]
    tiles = p_len // tm
    slot = C_NOPE + LANES
    return pl.pallas_call(
        _c_prep_kernel,
        grid=(t // tm,),
        in_specs=[
            pl.BlockSpec((tm, C_Q_RANK), lambda i: (i, col["c_cq"] // C_Q_RANK)),
            pl.BlockSpec((tm, C_KV_RANK), lambda i: (i, col["c_ckv"] // C_KV_RANK)),
            pl.BlockSpec((tm, LANES), lambda i: (i, col["c_kr"] // LANES)),
            pl.BlockSpec((3, tm, LANES), lambda i: (0, i % tiles, 0)),
            pl.BlockSpec((1, C_Q_RANK), lambda i: (0, 0)),
            pl.BlockSpec((1, C_KV_RANK), lambda i: (0, 0)),
            pl.BlockSpec((C_Q_RANK, C_HEADS * slot), lambda i: (0, 0)),
            pl.BlockSpec((C_KV_RANK, C_HEADS * (C_NOPE + C_VDIM)), lambda i: (0, 0)),
        ],
        out_specs=[pl.BlockSpec((C_HEADS, tm, slot), lambda i: (0, i, 0)),
                   pl.BlockSpec((C_HEADS, tm, slot), lambda i: (0, i, 0)),
                   pl.BlockSpec((C_HEADS, tm, C_VDIM), lambda i: (0, i, 0))],
        out_shape=[jax.ShapeDtypeStruct((C_HEADS, t, slot), BF16),
                   jax.ShapeDtypeStruct((C_HEADS, t, slot), BF16),
                   jax.ShapeDtypeStruct((C_HEADS, t, C_VDIM), BF16)],
        compiler_params=_cparams(1),
        name="c_prep",
    )(pf, pf, pf, tabs, g_cq.reshape(1, -1), g_ckv.reshape(1, -1), w_uq_p, w_ukv_p)


def _causal_mask(q0, tq, ext):
    qpos = q0 + lax.broadcasted_iota(jnp.int32, (tq, ext), 0)
    kpos = lax.broadcasted_iota(jnp.int32, (tq, ext), 1)
    return jnp.where(kpos <= qpos, jnp.where(kpos >= PAD_FRONT, 0.0, NEG), NEG)


def _attn_c_kernel(q_ref, k_ref, v_ref, g_ref, o_ref, os_ref, *, tq, p_len):
    qi = pl.program_id(1)
    scale2 = (C_NOPE + C_ROPE) ** -0.5 * LOG2E

    def body(ext):
        mask = _causal_mask(qi * tq, tq, ext)

        def hbody(h, carry):
            s = _dot_nt(q_ref[h], k_ref[h, :ext, :]) * scale2 + mask
            m = jnp.max(s, axis=1, keepdims=True)
            e = jnp.exp2(s - m)
            l = jnp.sum(e, axis=1, keepdims=True)
            os_ref[h] = _dot(e.astype(BF16), v_ref[h, :ext, :]) / l
            return carry

        lax.fori_loop(0, C_HEADS, hbody, 0, unroll=2)

    nq = p_len // tq
    _for_each_extent(qi, nq, tq, p_len, body, nq)
    o = jnp.concatenate([os_ref[h] for h in range(C_HEADS)], axis=1)
    o_ref[...] = _rms(o, g_ref[...]).astype(BF16)


def _attn_c(qc, kc, vc, g, batch, p_len, tq):
    nq = p_len // tq
    t = batch * p_len
    slot = C_NOPE + LANES
    kern = functools.partial(_attn_c_kernel, tq=tq, p_len=p_len)
    return pl.pallas_call(
        kern,
        grid=(batch, nq),
        in_specs=[pl.BlockSpec((C_HEADS, tq, slot), lambda b, q: (0, b * nq + q, 0)),
                  pl.BlockSpec((C_HEADS, p_len, slot), lambda b, q: (0, b, 0)),
                  pl.BlockSpec((C_HEADS, p_len, C_VDIM), lambda b, q: (0, b, 0)),
                  pl.BlockSpec((1, GROUP_WIDTH), lambda b, q: (0, 0))],
        out_specs=pl.BlockSpec((tq, GROUP_WIDTH), lambda b, q: (b * nq + q, 0)),
        out_shape=jax.ShapeDtypeStruct((t, GROUP_WIDTH), BF16),
        scratch_shapes=[pltpu.VMEM((C_HEADS, tq, C_VDIM), F32)],
        compiler_params=_cparams(2),
        name="attn_c",
    )(qc, kc, vc, g.reshape(1, GROUP_WIDTH))


def _attn_d_kernel(q_ref, k_ref, v_ref, bias_ref, lam_ref, g_ref, o_ref, os_ref, *, tq, p_len, lam_init):
    qi = pl.program_id(1)
    lp = lam_ref[...]
    lam = (jnp.exp(jnp.sum(lp[0:1] * lp[1:2], axis=1, keepdims=True))
           - jnp.exp(jnp.sum(lp[2:3] * lp[3:4], axis=1, keepdims=True)) + lam_init)

    def body(ext):
        mask = _causal_mask(qi * tq, tq, ext)

        def hbody(h, carry):
            bm = bias_ref[h, :, :ext] + mask
            qh = q_ref[h]
            kh = k_ref[h, :ext, :]
            es, ls = [], []
            for c in range(2):
                lo = c * D_HDIM
                s = _dot_nt(qh[:, lo:lo + D_HDIM], kh[:, lo:lo + D_HDIM]) + bm
                m = jnp.max(s, axis=1, keepdims=True)
                e = jnp.exp(s - m)
                es.append(e)
                ls.append(jnp.sum(e, axis=1, keepdims=True))
            a = es[0] * (1.0 / ls[0]) - es[1] * (lam / ls[1])
            os_ref[h] = _dot(a.astype(BF16), v_ref[h, :ext, :])
            return carry

        lax.fori_loop(0, D_HEADS, hbody, 0)

    nq = p_len // tq
    _for_each_extent(qi, nq, tq, p_len, body, nq)
    g = g_ref[...]
    outs = [_rms(os_ref[h], g) * (1.0 - lam_init) for h in range(D_HEADS)]
    o_ref[...] = jnp.concatenate(outs, axis=1).astype(BF16)


def _attn_d(pd, bias, lam_p, g, batch, p_len, tq, lam_init):
    nq = p_len // tq
    t = batch * p_len
    kern = functools.partial(_attn_d_kernel, tq=tq, p_len=p_len, lam_init=lam_init)
    return pl.pallas_call(
        kern,
        grid=(batch, nq),
        in_specs=[pl.BlockSpec((D_HEADS, tq, LANES), lambda b, q: (0, b * nq + q, 0)),
                  pl.BlockSpec((D_HEADS, p_len, LANES), lambda b, q: (1, b, 0)),
                  pl.BlockSpec((D_HEADS, p_len, LANES), lambda b, q: (2, b, 0)),
                  pl.BlockSpec((D_HEADS, 1, p_len), lambda b, q: (0, 0, 0)),
                  pl.BlockSpec((4, D_HDIM), lambda b, q: (0, 0)),
                  pl.BlockSpec((1, 2 * D_HDIM), lambda b, q: (0, 0))],
        out_specs=pl.BlockSpec((tq, GROUP_WIDTH), lambda b, q: (b * nq + q, 0)),
        out_shape=jax.ShapeDtypeStruct((t, GROUP_WIDTH), BF16),
        scratch_shapes=[pltpu.VMEM((D_HEADS, tq, 2 * D_HDIM), F32)],
        compiler_params=_cparams(2),
        name="attn_d",
    )(pd, pd, pd, bias, lam_p, g.reshape(1, 2 * D_HDIM))


def _ffn_up_kernel(x_ref, wg_ref, wu_ref, cwg_ref, cwu_ref, cbg_ref, cbu_ref, o_ref,
                   wgb_ref, wub_ref, ha_ref, hb_ref, carry_ref, *, tm, n_row_tiles, tiles_per_seq):
    s = pl.program_id(0)
    d = x_ref.shape[1]
    kc = min(d, MXU_DEPTH)
    nk = d // kc
    rc = next(r for r in (64, 32, 16, 8) if tm % r == 0)
    nr = tm // rc

    @pl.when(s % n_row_tiles == 0)
    def _():
        wgb_ref[...] = wg_ref[...].astype(BF16)
        wub_ref[...] = wu_ref[...].astype(BF16)

    @pl.when(s == 0)
    def _():
        hb_ref[...] = jnp.zeros_like(hb_ref)

    @pl.when((s + tiles_per_seq - 1) % tiles_per_seq == 0)
    def _():
        carry_ref[...] = jnp.zeros_like(carry_ref)

    def run(rd_ref, wr_ref):
        cws = (cwg_ref[...], cwu_ref[...])
        cbs = (cbg_ref[...], cbu_ref[...])
        for step in range(nr):
            for kk in range((step * nk) // nr, ((step + 1) * nk) // nr):
                for b, wb_ref in enumerate((wgb_ref, wub_ref)):
                    part = _dot(x_ref[:, kk * kc:(kk + 1) * kc], wb_ref[kk * kc:(kk + 1) * kc, :])
                    if kk == 0:
                        wr_ref[b] = part
                    else:
                        wr_ref[b] = wr_ref[b] + part
            r0 = step * rc
            ys = []
            for b in range(2):
                top = carry_ref[b] if step == 0 else rd_ref[b, r0 - 8:r0, :]
                cur = rd_ref[b, r0:r0 + rc, :]
                full = jnp.concatenate([top, cur], axis=0)
                cw = cws[b]
                y = cw[2:3] * cur + cw[1:2] * pltpu.roll(full, 1, 0)[8:] + cw[0:1] * pltpu.roll(full, 2, 0)[8:]
                ys.append(y + cbs[b])
            o_ref[r0:r0 + rc, :] = (ys[0] * jax.nn.sigmoid(ys[0]) * ys[1]).astype(BF16)
        for b in range(2):
            carry_ref[b] = rd_ref[b, tm - 8:, :]

    @pl.when(s % 2 == 0)
    def _():
        run(hb_ref, ha_ref)

    @pl.when(s % 2 == 1)
    def _():
        run(ha_ref, hb_ref)


def _ffn_up(hb, w_up, conv_w, conv_b, layer, p_len, tm, tn):
    t, d = hb.shape
    f = w_up.shape[2] // 2
    assert f % tn == 0 and tm % 8 == 0 and p_len % tm == 0
    nj, nt = f // tn, t // tm
    n_tiles = nj * nt
    kern = functools.partial(_ffn_up_kernel, tm=tm, n_row_tiles=nt, tiles_per_seq=p_len // tm)
    cb = conv_b.reshape(conv_b.shape[0], 1, 2 * f)
    cur = lambda s: jnp.minimum(s, n_tiles - 1)
    prv = lambda s: jnp.maximum(s - 1, 0)
    return pl.pallas_call(
        kern,
        grid=(n_tiles + 1,),
        in_specs=[pl.BlockSpec((tm, d), lambda s: (cur(s) % nt, 0)),
                  pl.BlockSpec((None, d, tn), lambda s: (layer, 0, cur(s) // nt)),
                  pl.BlockSpec((None, d, tn), lambda s: (layer, 0, nj + cur(s) // nt)),
                  pl.BlockSpec((None, CONV_W, tn), lambda s: (layer, 0, prv(s) // nt)),
                  pl.BlockSpec((None, CONV_W, tn), lambda s: (layer, 0, nj + prv(s) // nt)),
                  pl.BlockSpec((None, 1, tn), lambda s: (layer, 0, prv(s) // nt)),
                  pl.BlockSpec((None, 1, tn), lambda s: (layer, 0, nj + prv(s) // nt))],
        out_specs=pl.BlockSpec((tm, tn), lambda s: (prv(s) % nt, prv(s) // nt)),
        out_shape=jax.ShapeDtypeStruct((t, f), BF16),
        scratch_shapes=[pltpu.VMEM((d, tn), BF16), pltpu.VMEM((d, tn), BF16),
                        pltpu.VMEM((2, tm, tn), F32), pltpu.VMEM((2, tm, tn), F32), pltpu.VMEM((2, 8, tn), F32)],
        compiler_params=_cparams(1),
        name="ffn_up",
    )(hb, w_up, w_up, conv_w, conv_w, cb, cb)


def _offsets(names, sizes):
    out, o = {}, 0
    for n, s in zip(names, sizes):
        out[n] = o
        o += s
    return out, o


_BF_NAMES = ("a_q", "b_q", "a_k", "a_v", "b_k", "b_v")
_BF_SIZES = (1024, 1024, 128, 128, 128, 128)
_D_NAMES = ("d_q", "d_k", "d_v")
_F32_NAMES = ("i_q", "c_cq", "c_ckv", "ikw", "c_kr")
_F32_SIZES = (1024, 1024, 512, 128, 128)
COL_BF, N_BF = _offsets(_BF_NAMES, _BF_SIZES)
COL_F32, N_F32 = _offsets(_F32_NAMES, _F32_SIZES)


def _prep_w_in(w):
    d = w.shape[0]
    seg = dict(zip(("a_q", "a_k", "a_v", "i_q", "i_k", "i_w", "b_q", "b_k", "b_v", "c_cq", "c_ckv", "c_kr",
                    "d_q", "d_k", "d_v"), jnp.split(w, np.cumsum(IN_SIZES)[:-1].tolist(), axis=1)))
    seg["b_q"] = seg["b_q"] * (B_HDIM ** -0.5)
    seg["d_q"] = seg["d_q"] * (D_HDIM ** -0.5)
    wb = jnp.concatenate([seg[n] for n in _BF_NAMES], axis=1).astype(BF16)
    wd = jnp.concatenate([seg[n] for n in _D_NAMES], axis=1).astype(BF16)
    zeros = lambda n: jnp.zeros((d, n), w.dtype)
    ikw = jnp.concatenate([seg["i_k"], seg["i_w"], zeros(LANES - IDX_HDIM - IDX_HEADS)], axis=1)
    ckr = jnp.concatenate([seg["c_kr"], zeros(LANES - C_ROPE)], axis=1)
    wf = jnp.concatenate([seg["i_q"], seg["c_cq"], seg["c_ckv"], ikw, ckr], axis=1).astype(BF16)
    return wb, wd, wf


def _prep_w_uq(w):
    r = w.shape[0]
    w3 = w.reshape(r, C_HEADS, C_NOPE + C_ROPE)
    return jnp.pad(w3, ((0, 0), (0, 0), (0, LANES - C_ROPE))).reshape(r, C_HEADS * (C_NOPE + LANES)).astype(BF16)


def _prep_w_ukv(w):
    r = w.shape[0]
    w3 = w.reshape(r, C_HEADS, C_NOPE + C_VDIM)
    return jnp.concatenate([w3[:, :, :C_NOPE].reshape(r, -1), w3[:, :, C_NOPE:].reshape(r, -1)], axis=1).astype(BF16)


def _alibi(n):
    return 2.0 ** (-8.0 * np.arange(1, n + 1, dtype=np.float64) / n)


def _swa_bias():
    grp = B_HEADS // B_KV_HEADS
    r = np.arange(BLK)[:, None]
    c = np.arange(2 * BLK)[None, :]
    diff = (r + BLK - c).astype(np.float64)
    ok = (diff >= 0) & (diff < WINDOW)
    slopes = _alibi(B_HEADS).reshape(B_KV_HEADS, grp)
    bias = np.where(ok[None, None], -slopes[:, :, None, None] * diff[None, None], NEG)
    return jnp.asarray(bias.reshape(B_KV_HEADS, grp * BLK, 2 * BLK), F32)


def kernel(x, meta_tokens, ln_in_g, ln_in_b, w_in, g_cq, g_ckv, w_uq, w_ukv, sinks, lam_q1, lam_k1, lam_q2, lam_k2,
           g_diff, g_grp, w_out, ln1_g, ln1_b, w_up, conv_w, conv_b, w_down, ln2_g, ln2_b):
    batch, s_len, d = x.shape
    depth = w_in.shape[0]
    p_len = s_len + BLK
    t = batch * p_len
    k_top = min(TOPK_MAX, s_len // 4)
    alpha = (2 * depth) ** 0.25
    tm = _row_tile(p_len, 544)
    tm_ln = _row_tile(p_len, 272)
    tm_ffn = _row_tile(p_len, 1088)
    tq = BLK
    grp = B_HEADS // B_KV_HEADS

    h, hb = _ln_in(x, meta_tokens, ln_in_g, ln_in_b)

    pos = jnp.arange(p_len, dtype=jnp.int32) - PAD_FRONT
    tabs_idx = _rope_tables(pos, IDX_ROPE, IDX_HDIM)
    tabs_c = _rope_tables(pos, C_ROPE, LANES)
    kidx = np.arange(p_len, dtype=np.float64)
    bias_a = jnp.asarray(_alibi(A_HEADS)[:, None, None] * kidx[None, None, :] * LOG2E, F32)
    bias_d = jnp.asarray(_alibi(D_HEADS)[:, None, None] * kidx[None, None, :], F32)
    bias_b = _swa_bias()

    for l in range(depth):
        wb, wd, wf = _prep_w_in(w_in[l])
        pb = _mm(hb, wb, tm_ffn, 2 * MXU_DEPTH, BF16, "proj_bf16")
        pd = _mm_slab(hb, wd, tm_ffn, 2 * MXU_DEPTH, BF16, "proj_d")
        pf = _mm(hb, wf, tm, N_F32 // 2, F32, "proj_f32")

        o_a = _attn_a(pf, pb, tabs_idx, bias_a, g_grp[l, 0], batch, p_len, tq, k_top, {**COL_F32, **COL_BF})
        sink_rows = jnp.repeat(sinks[l].astype(F32).reshape(B_KV_HEADS, grp), BLK, axis=1)[..., None]
        o_b = _attn_b(pb, bias_b, sink_rows, g_grp[l, 1], batch, p_len, COL_BF)
        qc, kc, vc = _c_prep(pf, tabs_c, g_cq[l], g_ckv[l], _prep_w_uq(w_uq[l]), _prep_w_ukv(w_ukv[l]),
                               p_len, tm_ln, COL_F32)
        o_c = _attn_c(qc, kc, vc, g_grp[l, 2], batch, p_len, tm_ln)
        lam_init = 0.8 - 0.6 * math.exp(-0.3 * l)
        lam_p = jnp.stack([lam_q1[l], lam_k1[l], lam_q2[l], lam_k2[l]]).astype(F32)
        o_d = _attn_d(pd, bias_d, lam_p, g_diff[l], batch, p_len, tm_ln, lam_init)

        mix_in = jnp.concatenate([o_a, o_b, o_c, o_d], axis=1)
        mix = _mm(mix_in, w_out[l].astype(BF16), tm, min(1024, d), F32, "out_proj")
        h, hb_ffn = _res_ln(h, mix, ln1_g[l], ln1_b[l], alpha, tm_ln, p_len, True)

        act = _ffn_up(hb_ffn, w_up, conv_w, conv_b, l, p_len, tm_ffn, FF_TILE)
        ffn = _mm(act, w_down[l].astype(BF16), tm, min(512, d), F32, "ffn_down")
        if l + 1 < depth:
            h, hb = _res_ln(h, ffn, ln2_g[l], ln2_b[l], alpha, tm_ln, p_len, False)
        else:
            out = _res_ln_final(h, ffn, ln2_g[l], ln2_b[l], alpha, batch, p_len)

    return out.reshape(batch, s_len, d)
```

```python
import functools
import math

import jax
import jax.numpy as jnp
import numpy as np
from jax import lax
from jax.experimental import pallas as pl
from jax.experimental.pallas import tpu as pltpu

N_META = 16
BLK = 128
PAD_FRONT = BLK - N_META
A_HEADS, A_HDIM = 8, 128
IDX_HEADS, IDX_HDIM, IDX_ROPE = 16, 64, 32
TOPK_MAX = 256
B_HEADS, B_KV_HEADS, B_HDIM = 16, 2, 64
WINDOW = 128
C_HEADS, C_Q_RANK, C_KV_RANK, C_NOPE, C_ROPE, C_VDIM = 8, 1024, 512, 128, 64, 128
D_HEADS, D_HDIM = 8, 64
GROUP_WIDTH = 1024
CONV_W = 3
ROPE_BASE = 10000.0
NEG = -1e30
IN_SIZES = (1024, 128, 128, 1024, 64, 16, 1024, 128, 128, 1024, 512, 64, 1024, 1024, 1024)

LANES = 128
BF16_SUBLANES = 16
MXU_DEPTH = 256
VMEM_LIMIT = 56 * 1024 * 1024
FF_TILE = 256
N_EXTENT_VARIANTS = 6
LOG2E = 1.4426950408889634

F32 = jnp.float32
BF16 = jnp.bfloat16


def _cparams(n_axes):
    return pltpu.CompilerParams(dimension_semantics=("arbitrary",) * n_axes, vmem_limit_bytes=VMEM_LIMIT)


def _row_tile(p_len, target):
    best = BF16_SUBLANES
    for t in range(BF16_SUBLANES, target + 1, BF16_SUBLANES):
        if p_len % t == 0:
            best = t
    return best


def _dot(a, b):
    return jnp.dot(a, b, preferred_element_type=F32)


def _dot_nt(a, b):
    return lax.dot_general(a, b, (((1,), (1,)), ((), ())), preferred_element_type=F32)


def _ln_rows(x, g, b):
    mu = jnp.mean(x, axis=-1, keepdims=True)
    xc = x - mu
    var = jnp.mean(xc * xc, axis=-1, keepdims=True)
    return xc * lax.rsqrt(var + 1e-5) * g + b


def _ln_in_kernel(x_ref, meta_ref, g_ref, b_ref, h_ref, hb_ref):
    def emit(rows):
        y = _ln_rows(rows, g_ref[...], b_ref[...])
        h_ref[...] = y
        hb_ref[...] = y.astype(BF16)

    @pl.when(pl.program_id(1) == 0)
    def _():
        emit(jnp.concatenate([jnp.zeros((PAD_FRONT, meta_ref.shape[1]), F32), meta_ref[...]], axis=0))

    @pl.when(pl.program_id(1) > 0)
    def _():
        emit(x_ref[...])


def _ln_in(x, meta, g, b):
    batch, s_len, d = x.shape
    nb = s_len // BLK + 1
    t = batch * nb * BLK
    return pl.pallas_call(
        _ln_in_kernel,
        grid=(batch, nb),
        in_specs=[pl.BlockSpec((None, BLK, d), lambda bi, n: (bi, jnp.maximum(n - 1, 0), 0)),
                  pl.BlockSpec((N_META, d), lambda bi, n: (0, 0)),
                  pl.BlockSpec((1, d), lambda bi, n: (0, 0)),
                  pl.BlockSpec((1, d), lambda bi, n: (0, 0))],
        out_specs=[pl.BlockSpec((BLK, d), lambda bi, n: (bi * nb + n, 0)),
                   pl.BlockSpec((BLK, d), lambda bi, n: (bi * nb + n, 0))],
        out_shape=[jax.ShapeDtypeStruct((t, d), F32), jax.ShapeDtypeStruct((t, d), BF16)],
        compiler_params=_cparams(2),
        name="ln_in",
    )(x, meta.astype(F32), g.reshape(1, d), b.reshape(1, d))


def _res_ln_kernel(h_ref, y_ref, g_ref, b_ref, o_ref, ob_ref, *, alpha, tm, tiles_per_seq, zero_pad):
    out = _ln_rows(alpha * h_ref[...] + y_ref[...], g_ref[...], b_ref[...])
    o_ref[...] = out
    if zero_pad:
        p0 = (pl.program_id(0) % tiles_per_seq) * tm
        pos = p0 + lax.broadcasted_iota(jnp.int32, out.shape, 0)
        out = jnp.where(pos >= PAD_FRONT, out, 0.0)
    ob_ref[...] = out.astype(BF16)


def _res_ln(h, y, g, b, alpha, tm, p_len, zero_pad):
    t, d = h.shape
    kern = functools.partial(_res_ln_kernel, alpha=alpha, tm=tm, tiles_per_seq=p_len // tm, zero_pad=zero_pad)
    return pl.pallas_call(
        kern,
        grid=(t // tm,),
        in_specs=[pl.BlockSpec((tm, d), lambda i: (i, 0)),
                  pl.BlockSpec((tm, d), lambda i: (i, 0)),
                  pl.BlockSpec((1, d), lambda i: (0, 0)),
                  pl.BlockSpec((1, d), lambda i: (0, 0))],
        out_specs=[pl.BlockSpec((tm, d), lambda i: (i, 0)),
                   pl.BlockSpec((tm, d), lambda i: (i, 0))],
        out_shape=[jax.ShapeDtypeStruct((t, d), F32), jax.ShapeDtypeStruct((t, d), BF16)],
        compiler_params=_cparams(1),
        name="res_ln",
    )(h, y, g.reshape(1, d), b.reshape(1, d))


def _res_ln_final_kernel(h_ref, y_ref, g_ref, b_ref, o_ref, *, alpha):
    o_ref[...] = _ln_rows(alpha * h_ref[...] + y_ref[...], g_ref[...], b_ref[...])


def _res_ln_final(h, y, g, b, alpha, batch, p_len):
    t, d = h.shape
    nb = p_len // BLK
    out_idx = lambda i: ((i // nb) * (nb - 1) + jnp.maximum(i % nb - 1, 0), 0)
    return pl.pallas_call(
        functools.partial(_res_ln_final_kernel, alpha=alpha),
        grid=(t // BLK,),
        in_specs=[pl.BlockSpec((BLK, d), lambda i: (i, 0)),
                  pl.BlockSpec((BLK, d), lambda i: (i, 0)),
                  pl.BlockSpec((1, d), lambda i: (0, 0)),
                  pl.BlockSpec((1, d), lambda i: (0, 0))],
        out_specs=pl.BlockSpec((BLK, d), out_idx),
        out_shape=jax.ShapeDtypeStruct((batch * (p_len - BLK), d), F32),
        compiler_params=_cparams(1),
        name="res_ln_final",
    )(h, y, g.reshape(1, d), b.reshape(1, d))


def _mm_kernel(a_ref, w_ref, o_ref):
    o_ref[...] = _dot(a_ref[...], w_ref[...]).astype(o_ref.dtype)


def _mm(a, w, tm, tn, out_dtype, name):
    t, k = a.shape
    n = w.shape[1]
    return pl.pallas_call(
        _mm_kernel,
        grid=(n // tn, t // tm),
        in_specs=[pl.BlockSpec((tm, k), lambda j, i: (i, 0)),
                  pl.BlockSpec((k, tn), lambda j, i: (0, j))],
        out_specs=pl.BlockSpec((tm, tn), lambda j, i: (i, j)),
        out_shape=jax.ShapeDtypeStruct((t, n), out_dtype),
        compiler_params=_cparams(2),
        name=name,
    )(a, w)


def _mm_groups_kernel(*refs):
    *a_refs, w_ref, o_ref = refs
    k0 = 0
    acc = None
    for a_ref in a_refs:
        kw = a_ref.shape[1]
        part = _dot(a_ref[...], w_ref[k0:k0 + kw, :])
        acc = part if acc is None else acc + part
        k0 += kw
    o_ref[...] = acc.astype(o_ref.dtype)


def _mm_groups(a_list, w, tm, tn, out_dtype, name):
    t = a_list[0].shape[0]
    k, n = w.shape
    assert sum(a.shape[1] for a in a_list) == k
    return pl.pallas_call(
        _mm_groups_kernel,
        grid=(n // tn, t // tm),
        in_specs=[pl.BlockSpec((tm, a.shape[1]), lambda j, i: (i, 0)) for a in a_list]
        + [pl.BlockSpec((k, tn), lambda j, i: (0, j))],
        out_specs=pl.BlockSpec((tm, tn), lambda j, i: (i, j)),
        out_shape=jax.ShapeDtypeStruct((t, n), out_dtype),
        compiler_params=_cparams(2),
        name=name,
    )(*a_list, w)


def _mm_slab_kernel(a_ref, w_ref, o_ref):
    res = _dot(a_ref[...], w_ref[...])
    for c in range(o_ref.shape[0]):
        o_ref[c] = res[:, c * LANES:(c + 1) * LANES].astype(o_ref.dtype)


def _mm_slab(a, w, tm, tn, out_dtype, name):
    t, k = a.shape
    n = w.shape[1]
    return pl.pallas_call(
        _mm_slab_kernel,
        grid=(n // tn, t // tm),
        in_specs=[pl.BlockSpec((tm, k), lambda j, i: (i, 0)),
                  pl.BlockSpec((k, tn), lambda j, i: (0, j))],
        out_specs=pl.BlockSpec((tn // LANES, tm, LANES), lambda j, i: (j, i, 0)),
        out_shape=jax.ShapeDtypeStruct((n // LANES, t, LANES), out_dtype),
        compiler_params=_cparams(2),
        name=name,
    )(a, w)


def _rope128(x, c, s1, s2, half):
    return x * c + pltpu.roll(x, LANES - half, 1) * s1 + pltpu.roll(x, half, 1) * s2


def _rope_tables(pos, rot_dim, period):
    half = rot_dim // 2
    inv = ROPE_BASE ** (-jnp.arange(half, dtype=F32) / half)
    ang = pos.astype(F32)[:, None] * inv[None]
    cos, sin = jnp.cos(ang), jnp.sin(ang)
    n = pos.shape[0]
    lane = np.arange(LANES) % period
    li = lane % half
    cos_l, sin_l = cos[:, li], sin[:, li]
    first = jnp.asarray(lane < half)[None]
    second = jnp.asarray((lane >= half) & (lane < rot_dim))[None]
    c = jnp.where(first | second, cos_l, 1.0)
    s1 = jnp.where(first, -sin_l, 0.0)
    s2 = jnp.where(second, sin_l, 0.0)
    return jnp.stack([c, s1, s2]).astype(F32).reshape(3, n, LANES)


def _rms(x, g, eps=1e-6):
    return x * lax.rsqrt(jnp.mean(x * x, axis=-1, keepdims=True) + eps) * g


def _key_extents(nq, tq, p_len, nvar):
    step = -(-nq // nvar)
    out = []
    for lo in range(0, nq, step):
        hi = min(lo + step, nq)
        out.append((lo, hi, min(p_len, -(-(hi * tq) // BLK) * BLK)))
    return out


def _for_each_extent(qi, nq, tq, p_len, body, nvar=N_EXTENT_VARIANTS):
    for lo, hi, ext in _key_extents(nq, tq, p_len, nvar):
        @pl.when(jnp.logical_and(qi >= lo, qi < hi))
        def _():
            body(ext)


def _attn_a_kernel(iq_ref, wq_ref, ikw_ref, tq_ref, tk_ref, aq_ref, ak_ref, av_ref, bias_ref, g_ref,
                   o_ref, key_ref, mask_ref, qs_ref, os_ref, lo_ref, cand_ref, cnt_ref, *, tq, p_len, k_top):
    qi = pl.program_id(1)
    q0 = qi * tq
    half = IDX_ROPE // 2
    for h in range(A_HEADS):
        qs_ref[h * tq:(h + 1) * tq, :] = aq_ref[:, h * A_HDIM:(h + 1) * A_HDIM]

    def body(ext):
        ik = _rope128(ikw_ref[:ext, :], tk_ref[0, :ext, :], tk_ref[1, :ext, :], tk_ref[2, :ext, :],
                      half)[:, :IDX_HDIM].astype(BF16)
        w = wq_ref[...][:, IDX_HDIM:IDX_HDIM + IDX_HEADS] * ((IDX_HEADS * IDX_HDIM) ** -0.5)
        tc, ts1, ts2 = tq_ref[0], tq_ref[1], tq_ref[2]
        score = jnp.zeros((tq, ext), F32)
        for c in range(IDX_HEADS * IDX_HDIM // LANES):
            chunk = _rope128(iq_ref[:, c * LANES:(c + 1) * LANES], tc, ts1, ts2, half).astype(BF16)
            for j in range(LANES // IDX_HDIM):
                h = c * (LANES // IDX_HDIM) + j
                rel = jnp.maximum(_dot_nt(chunk[:, j * IDX_HDIM:(j + 1) * IDX_HDIM], ik), 0.0)
                score = score + w[:, h:h + 1] * rel
        qpos = q0 + lax.broadcasted_iota(jnp.int32, (tq, ext), 0)
        kpos = lax.broadcasted_iota(jnp.int32, (tq, ext), 1)
        vis = jnp.where(kpos <= qpos, jnp.where(kpos >= PAD_FRONT, 1.0, 0.0), 0.0)
        score = jnp.where(vis > 0.0, score, NEG)
        bits = lax.bitcast_convert_type(score, jnp.int32)
        key_ref[:, :ext] = jnp.where(bits < 0, bits ^ jnp.int32(0x7FFFFFFF), bits)
        kf = jnp.float32(k_top)

        def count(pred):
            return jnp.sum(jnp.where(pred, 1.0, 0.0), axis=1, keepdims=True)

        hs = tq // 2

        def lane_counts(r0, cand):
            terms = [jnp.where(key_ref[r0:r0 + hs, c * LANES:(c + 1) * LANES] >= cand, 1.0, 0.0)
                     for c in range(ext // LANES)]
            while len(terms) > 1:
                terms = [terms[i] + terms[i + 1] if i + 1 < len(terms) else terms[i] for i in range(0, len(terms), 2)]
            return terms[0]

        lo_ref[...] = jnp.full((tq, LANES), -2 ** 31, jnp.int32)
        cand_ref[...] = jnp.full((hs, LANES), -2 ** 31, jnp.int32)
        cnt_ref[...] = jnp.zeros((hs, LANES), F32)

        def finish_second_half():
            tot = jnp.sum(cnt_ref[...], axis=1, keepdims=True)
            return jnp.where(tot >= kf, cand_ref[...], lo_ref[hs:, :])

        def vbody(i, carry):
            inc = jnp.left_shift(jnp.int32(1), 31 - i)
            lo_b = finish_second_half()
            lo_a = lo_ref[:hs, :]
            cand_a = lo_a + inc
            tot_a = jnp.sum(lane_counts(0, cand_a), axis=1, keepdims=True)
            cand_b = lo_b + inc
            cnt_ref[...] = lane_counts(hs, cand_b)
            cand_ref[...] = cand_b
            lo_ref[hs:, :] = lo_b
            lo_ref[:hs, :] = jnp.where(tot_a >= kf, cand_a, lo_a)
            return carry

        lax.fori_loop(0, 32, vbody, 0)
        lo_ref[hs:, :] = finish_second_half()
        thr = lo_ref[:, 0:1]
        key = key_ref[:, :ext]
        n_gt = count(key > thr)
        n_eq = count(key == thr)
        need = kf - n_gt
        n_eq_vis = jnp.sum(jnp.where(key == thr, vis, 0.0), axis=1, keepdims=True)
        mask_ref[:, :ext] = jnp.where(key >= thr, jnp.where(vis > 0.0, 0.0, NEG), NEG)
        surplus = jnp.where(n_eq_vis > 0.0, jnp.where(n_eq > need, 1.0, 0.0), 0.0)

        @pl.when(jnp.max(surplus) > 0.0)
        def _():
            nbits = max(1, (ext - 1).bit_length())
            keyv = key_ref[:, :ext]
            col = lax.broadcasted_iota(jnp.int32, (tq, ext), 1)

            def tbody(i, j):
                cand = j + jnp.left_shift(jnp.int32(1), nbits - 1 - i)
                f = jnp.sum(jnp.where(keyv == thr, jnp.where(col < cand, 1.0, 0.0), 0.0), axis=1, keepdims=True)
                return jnp.where(f < need, cand, j)

            jstar = lax.fori_loop(0, nbits, tbody, jnp.zeros((tq, 1), jnp.int32))
            chosen = jnp.where(keyv > thr, 1.0, jnp.where(keyv == thr, jnp.where(col <= jstar, 1.0, 0.0), 0.0))
            mask_ref[:, :ext] = jnp.where(chosen > 0.0, jnp.where(vis > 0.0, 0.0, NEG), NEG)

        scale2 = A_HDIM ** -0.5 * LOG2E

        def hbody(h, carry):
            r0 = pl.multiple_of(h * tq, tq)
            s = _dot_nt(qs_ref[pl.ds(r0, tq), :], ak_ref[:ext, :]) * scale2 + bias_ref[h, :, :ext] + mask_ref[:, :ext]
            m = jnp.max(s, axis=1, keepdims=True)
            e = jnp.exp2(s - m)
            l = jnp.sum(e, axis=1, keepdims=True)
            os_ref[pl.ds(r0, tq), :] = _dot(e.astype(BF16), av_ref[:ext, :]) / l
            return carry

        lax.fori_loop(0, A_HEADS, hbody, 0, unroll=2)

    _for_each_extent(qi, p_len // tq, tq, p_len, body)
    o = jnp.concatenate([os_ref[h * tq:(h + 1) * tq, :] for h in range(A_HEADS)], axis=1)
    o_ref[...] = _rms(o, g_ref[...]).astype(BF16)


def _attn_a(pf, pb, tabs_idx, bias, g, batch, p_len, tq, k_top, col):
    nq = p_len // tq
    t = batch * p_len
    kern = functools.partial(_attn_a_kernel, tq=tq, p_len=p_len, k_top=k_top)
    return pl.pallas_call(
        kern,
        grid=(batch, nq),
        in_specs=[
            pl.BlockSpec((tq, 1024), lambda b, q: (b * nq + q, col["i_q"] // 1024)),
            pl.BlockSpec((tq, LANES), lambda b, q: (b * nq + q, col["ikw"] // LANES)),
            pl.BlockSpec((p_len, LANES), lambda b, q: (b, col["ikw"] // LANES)),
            pl.BlockSpec((3, tq, LANES), lambda b, q: (0, q, 0)),
            pl.BlockSpec((3, p_len, LANES), lambda b, q: (0, 0, 0)),
            pl.BlockSpec((tq, 1024), lambda b, q: (b * nq + q, col["a_q"] // 1024)),
            pl.BlockSpec((p_len, LANES), lambda b, q: (b, col["a_k"] // LANES)),
            pl.BlockSpec((p_len, LANES), lambda b, q: (b, col["a_v"] // LANES)),
            pl.BlockSpec((A_HEADS, 1, p_len), lambda b, q: (0, 0, 0)),
            pl.BlockSpec((1, GROUP_WIDTH), lambda b, q: (0, 0)),
        ],
        out_specs=pl.BlockSpec((tq, GROUP_WIDTH), lambda b, q: (b * nq + q, 0)),
        out_shape=jax.ShapeDtypeStruct((t, GROUP_WIDTH), BF16),
        scratch_shapes=[pltpu.VMEM((tq, p_len), jnp.int32),
                        pltpu.VMEM((tq, p_len), F32),
                        pltpu.VMEM((A_HEADS * tq, A_HDIM), BF16),
                        pltpu.VMEM((A_HEADS * tq, A_HDIM), F32),
                        pltpu.VMEM((tq, LANES), jnp.int32),
                        pltpu.VMEM((tq // 2, LANES), jnp.int32),
                        pltpu.VMEM((tq // 2, LANES), F32)],
        compiler_params=_cparams(2),
        name="attn_a",
    )(pf, pf, pf, tabs_idx, tabs_idx, pb, pb, pb, bias, g.reshape(1, GROUP_WIDTH))


def _attn_b_kernel(q_ref, kp_ref, kc_ref, vp_ref, vc_ref, bias_ref, sink_ref, g_ref, o_ref):
    n = pl.program_id(1)
    grp = B_HEADS // B_KV_HEADS
    first_col = PAD_FRONT - (n - 1) * BLK
    col = lax.broadcasted_iota(jnp.int32, (grp * BLK, 2 * BLK), 1)
    pad_mask = jnp.where(col >= first_col, 0.0, NEG)
    outs = []
    for gi in range(B_KV_HEADS):
        q = jnp.concatenate([q_ref[:, (gi * grp + j) * B_HDIM:(gi * grp + j + 1) * B_HDIM] for j in range(grp)], axis=0)
        k = jnp.concatenate([kp_ref[:, gi * B_HDIM:(gi + 1) * B_HDIM], kc_ref[:, gi * B_HDIM:(gi + 1) * B_HDIM]], axis=0)
        v = jnp.concatenate([vp_ref[:, gi * B_HDIM:(gi + 1) * B_HDIM], vc_ref[:, gi * B_HDIM:(gi + 1) * B_HDIM]], axis=0)
        s = _dot_nt(q, k) + bias_ref[gi] + pad_mask
        sink = sink_ref[gi]
        m = jnp.maximum(jnp.max(s, axis=1, keepdims=True), sink)
        e = jnp.exp(s - m)
        l = jnp.sum(e, axis=1, keepdims=True) + jnp.exp(sink - m)
        o = _dot(e.astype(BF16), v) / l
        outs.extend(o[j * BLK:(j + 1) * BLK, :] for j in range(grp))
    o = jnp.concatenate(outs, axis=1)
    o_ref[...] = _rms(o, g_ref[...]).astype(BF16)


def _attn_b(pb, bias, sink_rows, g, batch, p_len, col):
    nb = p_len // BLK
    t = batch * p_len
    grp = B_HEADS // B_KV_HEADS
    return pl.pallas_call(
        _attn_b_kernel,
        grid=(batch, nb),
        in_specs=[
            pl.BlockSpec((BLK, 1024), lambda b, n: (b * nb + n, col["b_q"] // 1024)),
            pl.BlockSpec((BLK, LANES), lambda b, n: (b * nb + jnp.maximum(n - 1, 0), col["b_k"] // LANES)),
            pl.BlockSpec((BLK, LANES), lambda b, n: (b * nb + n, col["b_k"] // LANES)),
            pl.BlockSpec((BLK, LANES), lambda b, n: (b * nb + jnp.maximum(n - 1, 0), col["b_v"] // LANES)),
            pl.BlockSpec((BLK, LANES), lambda b, n: (b * nb + n, col["b_v"] // LANES)),
            pl.BlockSpec((B_KV_HEADS, grp * BLK, 2 * BLK), lambda b, n: (0, 0, 0)),
            pl.BlockSpec((B_KV_HEADS, grp * BLK, 1), lambda b, n: (0, 0, 0)),
            pl.BlockSpec((1, GROUP_WIDTH), lambda b, n: (0, 0)),
        ],
        out_specs=pl.BlockSpec((BLK, GROUP_WIDTH), lambda b, n: (b * nb + n, 0)),
        out_shape=jax.ShapeDtypeStruct((t, GROUP_WIDTH), BF16),
        compiler_params=_cparams(2),
        name="attn_b",
    )(pb, pb, pb, pb, pb, bias, sink_rows, g.reshape(1, GROUP_WIDTH))


def _c_prep_kernel(cq_ref, ckv_ref, kr_ref, tab_ref, gq_ref, gkv_ref, wq_ref, wkv_ref, q_ref, k_ref, v_ref):
    half = C_ROPE // 2
    tc, ts1, ts2 = tab_ref[0], tab_ref[1], tab_ref[2]
    slot = C_NOPE + LANES
    xq = _rms(cq_ref[...], gq_ref[...]).astype(BF16)
    q = _dot(xq, wq_ref[...])
    xkv = _rms(ckv_ref[...], gkv_ref[...]).astype(BF16)
    kv = _dot(xkv, wkv_ref[...])
    kr = _rope128(kr_ref[...], tc, ts1, ts2, half).astype(BF16)
    for h in range(C_HEADS):
        lo = h * slot
        q_ref[h, :, :C_NOPE] = q[:, lo:lo + C_NOPE].astype(BF16)
        q_ref[h, :, C_NOPE:] = _rope128(q[:, lo + C_NOPE:lo + slot], tc, ts1, ts2, half).astype(BF16)
        k_ref[h, :, :C_NOPE] = kv[:, h * C_NOPE:(h + 1) * C_NOPE].astype(BF16)
        k_ref[h, :, C_NOPE:] = kr
        v_ref[h] = kv[:, (C_HEADS + h) * C_NOPE:(C_HEADS + h + 1) * C_NOPE].astype(BF16)


def _c_prep(pf, tabs, g_cq, g_ckv, w_uq_p, w_ukv_p, p_len, tm, col):
    t = pf.shape[0]
    tiles = p_len // tm
    slot = C_NOPE + LANES
    return pl.pallas_call(
        _c_prep_kernel,
        grid=(t // tm,),
        in_specs=[
            pl.BlockSpec((tm, C_Q_RANK), lambda i: (i, col["c_cq"] // C_Q_RANK)),
            pl.BlockSpec((tm, C_KV_RANK), lambda i: (i, col["c_ckv"] // C_KV_RANK)),
            pl.BlockSpec((tm, LANES), lambda i: (i, col["c_kr"] // LANES)),
            pl.BlockSpec((3, tm, LANES), lambda i: (0, i % tiles, 0)),
            pl.BlockSpec((1, C_Q_RANK), lambda i: (0, 0)),
            pl.BlockSpec((1, C_KV_RANK), lambda i: (0, 0)),
            pl.BlockSpec((C_Q_RANK, C_HEADS * slot), lambda i: (0, 0)),
            pl.BlockSpec((C_KV_RANK, C_HEADS * (C_NOPE + C_VDIM)), lambda i: (0, 0)),
        ],
        out_specs=[pl.BlockSpec((C_HEADS, tm, slot), lambda i: (0, i, 0)),
                   pl.BlockSpec((C_HEADS, tm, slot), lambda i: (0, i, 0)),
                   pl.BlockSpec((C_HEADS, tm, C_VDIM), lambda i: (0, i, 0))],
        out_shape=[jax.ShapeDtypeStruct((C_HEADS, t, slot), BF16),
                   jax.ShapeDtypeStruct((C_HEADS, t, slot), BF16),
                   jax.ShapeDtypeStruct((C_HEADS, t, C_VDIM), BF16)],
        compiler_params=_cparams(1),
        name="c_prep",
    )(pf, pf, pf, tabs, g_cq.reshape(1, -1), g_ckv.reshape(1, -1), w_uq_p, w_ukv_p)


def _causal_mask(q0, tq, ext):
    qpos = q0 + lax.broadcasted_iota(jnp.int32, (tq, ext), 0)
    kpos = lax.broadcasted_iota(jnp.int32, (tq, ext), 1)
    return jnp.where(kpos <= qpos, jnp.where(kpos >= PAD_FRONT, 0.0, NEG), NEG)


def _attn_c_kernel(q_ref, k_ref, v_ref, g_ref, o_ref, os_ref, *, tq, p_len):
    qi = pl.program_id(1)
    scale2 = (C_NOPE + C_ROPE) ** -0.5 * LOG2E

    def body(ext):
        mask = _causal_mask(qi * tq, tq, ext)

        def hbody(h, carry):
            s = _dot_nt(q_ref[h], k_ref[h, :ext, :]) * scale2 + mask
            m = jnp.max(s, axis=1, keepdims=True)
            e = jnp.exp2(s - m)
            l = jnp.sum(e, axis=1, keepdims=True)
            os_ref[h] = _dot(e.astype(BF16), v_ref[h, :ext, :]) / l
            return carry

        lax.fori_loop(0, C_HEADS, hbody, 0, unroll=2)

    nq = p_len // tq
    _for_each_extent(qi, nq, tq, p_len, body, nq)
    o = jnp.concatenate([os_ref[h] for h in range(C_HEADS)], axis=1)
    o_ref[...] = _rms(o, g_ref[...]).astype(BF16)


def _attn_c(qc, kc, vc, g, batch, p_len, tq):
    nq = p_len // tq
    t = batch * p_len
    slot = C_NOPE + LANES
    kern = functools.partial(_attn_c_kernel, tq=tq, p_len=p_len)
    return pl.pallas_call(
        kern,
        grid=(batch, nq),
        in_specs=[pl.BlockSpec((C_HEADS, tq, slot), lambda b, q: (0, b * nq + q, 0)),
                  pl.BlockSpec((C_HEADS, p_len, slot), lambda b, q: (0, b, 0)),
                  pl.BlockSpec((C_HEADS, p_len, C_VDIM), lambda b, q: (0, b, 0)),
                  pl.BlockSpec((1, GROUP_WIDTH), lambda b, q: (0, 0))],
        out_specs=pl.BlockSpec((tq, GROUP_WIDTH), lambda b, q: (b * nq + q, 0)),
        out_shape=jax.ShapeDtypeStruct((t, GROUP_WIDTH), BF16),
        scratch_shapes=[pltpu.VMEM((C_HEADS, tq, C_VDIM), F32)],
        compiler_params=_cparams(2),
        name="attn_c",
    )(qc, kc, vc, g.reshape(1, GROUP_WIDTH))


def _attn_d_kernel(q_ref, k_ref, v_ref, bias_ref, lam_ref, g_ref, o_ref, os_ref, *, tq, p_len, lam_init):
    qi = pl.program_id(1)
    lp = lam_ref[...]
    lam = (jnp.exp(jnp.sum(lp[0:1] * lp[1:2], axis=1, keepdims=True))
           - jnp.exp(jnp.sum(lp[2:3] * lp[3:4], axis=1, keepdims=True)) + lam_init)

    def body(ext):
        mask = _causal_mask(qi * tq, tq, ext)

        def hbody(h, carry):
            bm = bias_ref[h, :, :ext] + mask
            qh = q_ref[h]
            kh = k_ref[h, :ext, :]
            es, ls = [], []
            for c in range(2):
                lo = c * D_HDIM
                s = _dot_nt(qh[:, lo:lo + D_HDIM], kh[:, lo:lo + D_HDIM]) + bm
                m = jnp.max(s, axis=1, keepdims=True)
                e = jnp.exp(s - m)
                es.append(e)
                ls.append(jnp.sum(e, axis=1, keepdims=True))
            a = es[0] * (1.0 / ls[0]) - es[1] * (lam / ls[1])
            os_ref[h] = _dot(a.astype(BF16), v_ref[h, :ext, :])
            return carry

        lax.fori_loop(0, D_HEADS, hbody, 0)

    nq = p_len // tq
    _for_each_extent(qi, nq, tq, p_len, body, nq)
    g = g_ref[...]
    outs = [_rms(os_ref[h], g) * (1.0 - lam_init) for h in range(D_HEADS)]
    o_ref[...] = jnp.concatenate(outs, axis=1).astype(BF16)


def _attn_d(pd, bias, lam_p, g, batch, p_len, tq, lam_init):
    nq = p_len // tq
    t = batch * p_len
    kern = functools.partial(_attn_d_kernel, tq=tq, p_len=p_len, lam_init=lam_init)
    return pl.pallas_call(
        kern,
        grid=(batch, nq),
        in_specs=[pl.BlockSpec((D_HEADS, tq, LANES), lambda b, q: (0, b * nq + q, 0)),
                  pl.BlockSpec((D_HEADS, p_len, LANES), lambda b, q: (1, b, 0)),
                  pl.BlockSpec((D_HEADS, p_len, LANES), lambda b, q: (2, b, 0)),
                  pl.BlockSpec((D_HEADS, 1, p_len), lambda b, q: (0, 0, 0)),
                  pl.BlockSpec((4, D_HDIM), lambda b, q: (0, 0)),
                  pl.BlockSpec((1, 2 * D_HDIM), lambda b, q: (0, 0))],
        out_specs=pl.BlockSpec((tq, GROUP_WIDTH), lambda b, q: (b * nq + q, 0)),
        out_shape=jax.ShapeDtypeStruct((t, GROUP_WIDTH), BF16),
        scratch_shapes=[pltpu.VMEM((D_HEADS, tq, 2 * D_HDIM), F32)],
        compiler_params=_cparams(2),
        name="attn_d",
    )(pd, pd, pd, bias, lam_p, g.reshape(1, 2 * D_HDIM))


def _ffn_up_kernel(x_ref, wg_ref, wu_ref, cwg_ref, cwu_ref, cbg_ref, cbu_ref, o_ref,
                   wgb_ref, wub_ref, ha_ref, hb_ref, carry_ref, *, tm, n_row_tiles, tiles_per_seq):
    s = pl.program_id(0)
    d = x_ref.shape[1]
    kc = min(d, MXU_DEPTH)
    nk = d // kc
    rc = next(r for r in (64, 32, 16, 8) if tm % r == 0)
    nr = tm // rc

    @pl.when(s % n_row_tiles == 0)
    def _():
        wgb_ref[...] = wg_ref[...].astype(BF16)
        wub_ref[...] = wu_ref[...].astype(BF16)

    @pl.when(s == 0)
    def _():
        hb_ref[...] = jnp.zeros_like(hb_ref)

    @pl.when((s + tiles_per_seq - 1) % tiles_per_seq == 0)
    def _():
        carry_ref[...] = jnp.zeros_like(carry_ref)

    def run(rd_ref, wr_ref):
        cws = (cwg_ref[...], cwu_ref[...])
        cbs = (cbg_ref[...], cbu_ref[...])
        for step in range(nr):
            for kk in range((step * nk) // nr, ((step + 1) * nk) // nr):
                for b, wb_ref in enumerate((wgb_ref, wub_ref)):
                    part = _dot(x_ref[:, kk * kc:(kk + 1) * kc], wb_ref[kk * kc:(kk + 1) * kc, :])
                    if kk == 0:
                        wr_ref[b] = part
                    else:
                        wr_ref[b] = wr_ref[b] + part
            r0 = step * rc
            ys = []
            for b in range(2):
                top = carry_ref[b] if step == 0 else rd_ref[b, r0 - 8:r0, :]
                cur = rd_ref[b, r0:r0 + rc, :]
                full = jnp.concatenate([top, cur], axis=0)
                cw = cws[b]
                y = cw[2:3] * cur + cw[1:2] * pltpu.roll(full, 1, 0)[8:] + cw[0:1] * pltpu.roll(full, 2, 0)[8:]
                ys.append(y + cbs[b])
            o_ref[r0:r0 + rc, :] = (ys[0] * jax.nn.sigmoid(ys[0]) * ys[1]).astype(BF16)
        for b in range(2):
            carry_ref[b] = rd_ref[b, tm - 8:, :]

    @pl.when(s % 2 == 0)
    def _():
        run(hb_ref, ha_ref)

    @pl.when(s % 2 == 1)
    def _():
        run(ha_ref, hb_ref)


def _ffn_up(hb, w_up, conv_w, conv_b, layer, p_len, tm, tn):
    t, d = hb.shape
    f = w_up.shape[2] // 2
    assert f % tn == 0 and tm % 8 == 0 and p_len % tm == 0
    nj, nt = f // tn, t // tm
    n_tiles = nj * nt
    kern = functools.partial(_ffn_up_kernel, tm=tm, n_row_tiles=nt, tiles_per_seq=p_len // tm)
    cb = conv_b.reshape(conv_b.shape[0], 1, 2 * f)
    cur = lambda s: jnp.minimum(s, n_tiles - 1)
    prv = lambda s: jnp.maximum(s - 1, 0)
    return pl.pallas_call(
        kern,
        grid=(n_tiles + 1,),
        in_specs=[pl.BlockSpec((tm, d), lambda s: (cur(s) % nt, 0)),
                  pl.BlockSpec((None, d, tn), lambda s: (layer, 0, cur(s) // nt)),
                  pl.BlockSpec((None, d, tn), lambda s: (layer, 0, nj + cur(s) // nt)),
                  pl.BlockSpec((None, CONV_W, tn), lambda s: (layer, 0, prv(s) // nt)),
                  pl.BlockSpec((None, CONV_W, tn), lambda s: (layer, 0, nj + prv(s) // nt)),
                  pl.BlockSpec((None, 1, tn), lambda s: (layer, 0, prv(s) // nt)),
                  pl.BlockSpec((None, 1, tn), lambda s: (layer, 0, nj + prv(s) // nt))],
        out_specs=pl.BlockSpec((tm, tn), lambda s: (prv(s) % nt, prv(s) // nt)),
        out_shape=jax.ShapeDtypeStruct((t, f), BF16),
        scratch_shapes=[pltpu.VMEM((d, tn), BF16), pltpu.VMEM((d, tn), BF16),
                        pltpu.VMEM((2, tm, tn), F32), pltpu.VMEM((2, tm, tn), F32), pltpu.VMEM((2, 8, tn), F32)],
        compiler_params=_cparams(1),
        name="ffn_up",
    )(hb, w_up, w_up, conv_w, conv_w, cb, cb)


def _offsets(names, sizes):
    out, o = {}, 0
    for n, s in zip(names, sizes):
        out[n] = o
        o += s
    return out, o


_BF_NAMES = ("a_q", "b_q", "a_k", "a_v", "b_k", "b_v")
_BF_SIZES = (1024, 1024, 128, 128, 128, 128)
_D_NAMES = ("d_q", "d_k", "d_v")
_F32_NAMES = ("i_q", "c_cq", "c_ckv", "ikw", "c_kr")
_F32_SIZES = (1024, 1024, 512, 128, 128)
COL_BF, N_BF = _offsets(_BF_NAMES, _BF_SIZES)
COL_F32, N_F32 = _offsets(_F32_NAMES, _F32_SIZES)


def _prep_w_in(w):
    d = w.shape[0]
    seg = dict(zip(("a_q", "a_k", "a_v", "i_q", "i_k", "i_w", "b_q", "b_k", "b_v", "c_cq", "c_ckv", "c_kr",
                    "d_q", "d_k", "d_v"), jnp.split(w, np.cumsum(IN_SIZES)[:-1].tolist(), axis=1)))
    seg["b_q"] = seg["b_q"] * (B_HDIM ** -0.5)
    seg["d_q"] = seg["d_q"] * (D_HDIM ** -0.5)
    wb = jnp.concatenate([seg[n] for n in _BF_NAMES], axis=1).astype(BF16)
    wd = jnp.concatenate([seg[n] for n in _D_NAMES], axis=1).astype(BF16)
    zeros = lambda n: jnp.zeros((d, n), w.dtype)
    ikw = jnp.concatenate([seg["i_k"], seg["i_w"], zeros(LANES - IDX_HDIM - IDX_HEADS)], axis=1)
    ckr = jnp.concatenate([seg["c_kr"], zeros(LANES - C_ROPE)], axis=1)
    wf = jnp.concatenate([seg["i_q"], seg["c_cq"], seg["c_ckv"], ikw, ckr], axis=1).astype(BF16)
    return wb, wd, wf


def _prep_w_uq(w):
    r = w.shape[0]
    w3 = w.reshape(r, C_HEADS, C_NOPE + C_ROPE)
    return jnp.pad(w3, ((0, 0), (0, 0), (0, LANES - C_ROPE))).reshape(r, C_HEADS * (C_NOPE + LANES)).astype(BF16)


def _prep_w_ukv(w):
    r = w.shape[0]
    w3 = w.reshape(r, C_HEADS, C_NOPE + C_VDIM)
    return jnp.concatenate([w3[:, :, :C_NOPE].reshape(r, -1), w3[:, :, C_NOPE:].reshape(r, -1)], axis=1).astype(BF16)


def _alibi(n):
    return 2.0 ** (-8.0 * np.arange(1, n + 1, dtype=np.float64) / n)


def _swa_bias():
    grp = B_HEADS // B_KV_HEADS
    r = np.arange(BLK)[:, None]
    c = np.arange(2 * BLK)[None, :]
    diff = (r + BLK - c).astype(np.float64)
    ok = (diff >= 0) & (diff < WINDOW)
    slopes = _alibi(B_HEADS).reshape(B_KV_HEADS, grp)
    bias = np.where(ok[None, None], -slopes[:, :, None, None] * diff[None, None], NEG)
    return jnp.asarray(bias.reshape(B_KV_HEADS, grp * BLK, 2 * BLK), F32)


def kernel(x, meta_tokens, ln_in_g, ln_in_b, w_in, g_cq, g_ckv, w_uq, w_ukv, sinks, lam_q1, lam_k1, lam_q2, lam_k2,
           g_diff, g_grp, w_out, ln1_g, ln1_b, w_up, conv_w, conv_b, w_down, ln2_g, ln2_b):
    batch, s_len, d = x.shape
    depth = w_in.shape[0]
    p_len = s_len + BLK
    t = batch * p_len
    k_top = min(TOPK_MAX, s_len // 4)
    alpha = (2 * depth) ** 0.25
    tm = _row_tile(p_len, 544)
    tm_ln = _row_tile(p_len, 272)
    tm_ffn = _row_tile(p_len, 1088)
    tq = BLK
    grp = B_HEADS // B_KV_HEADS

    h, hb = _ln_in(x, meta_tokens, ln_in_g, ln_in_b)

    pos = jnp.arange(p_len, dtype=jnp.int32) - PAD_FRONT
    tabs_idx = _rope_tables(pos, IDX_ROPE, IDX_HDIM)
    tabs_c = _rope_tables(pos, C_ROPE, LANES)
    kidx = np.arange(p_len, dtype=np.float64)
    bias_a = jnp.asarray(_alibi(A_HEADS)[:, None, None] * kidx[None, None, :] * LOG2E, F32)
    bias_d = jnp.asarray(_alibi(D_HEADS)[:, None, None] * kidx[None, None, :], F32)
    bias_b = _swa_bias()

    for l in range(depth):
        wb, wd, wf = _prep_w_in(w_in[l])
        pb = _mm(hb, wb, tm_ffn, 2 * MXU_DEPTH, BF16, "proj_bf16")
        pd = _mm_slab(hb, wd, tm_ffn, 2 * MXU_DEPTH, BF16, "proj_d")
        pf = _mm(hb, wf, tm, N_F32 // 2, F32, "proj_f32")

        o_a = _attn_a(pf, pb, tabs_idx, bias_a, g_grp[l, 0], batch, p_len, tq, k_top, {**COL_F32, **COL_BF})
        sink_rows = jnp.repeat(sinks[l].astype(F32).reshape(B_KV_HEADS, grp), BLK, axis=1)[..., None]
        o_b = _attn_b(pb, bias_b, sink_rows, g_grp[l, 1], batch, p_len, COL_BF)
        qc, kc, vc = _c_prep(pf, tabs_c, g_cq[l], g_ckv[l], _prep_w_uq(w_uq[l]), _prep_w_ukv(w_ukv[l]),
                               p_len, tm_ln, COL_F32)
        o_c = _attn_c(qc, kc, vc, g_grp[l, 2], batch, p_len, tm_ln)
        lam_init = 0.8 - 0.6 * math.exp(-0.3 * l)
        lam_p = jnp.stack([lam_q1[l], lam_k1[l], lam_q2[l], lam_k2[l]]).astype(F32)
        o_d = _attn_d(pd, bias_d, lam_p, g_diff[l], batch, p_len, tm_ln, lam_init)

        mix = _mm_groups([o_a, o_b, o_c, o_d], w_out[l].astype(BF16), tm, min(1024, d), F32, "out_proj")
        h, hb_ffn = _res_ln(h, mix, ln1_g[l], ln1_b[l], alpha, tm_ln, p_len, True)

        act = _ffn_up(hb_ffn, w_up, conv_w, conv_b, l, p_len, tm_ffn, FF_TILE)
        ffn = _mm(act, w_down[l].astype(BF16), tm, min(512, d), F32, "ffn_down")
        if l + 1 < depth:
            h, hb = _res_ln(h, ffn, ln2_g[l], ln2_b[l], alpha, tm_ln, p_len, False)
        else:
            out = _res_ln_final(h, ffn, ln2_g[l], ln2_b[l], alpha, batch, p_len)

    return out.reshape(batch, s_len, d)
```

```python
import functools
import math

import jax
import jax.numpy as jnp
import numpy as np
from jax import lax
from jax.experimental import pallas as pl
from jax.experimental.pallas import tpu as pltpu

N_META = 16
BLK = 128
PAD_FRONT = BLK - N_META
A_HEADS, A_HDIM = 8, 128
IDX_HEADS, IDX_HDIM, IDX_ROPE = 16, 64, 32
TOPK_MAX = 256
B_HEADS, B_KV_HEADS, B_HDIM = 16, 2, 64
WINDOW = 128
C_HEADS, C_Q_RANK, C_KV_RANK, C_NOPE, C_ROPE, C_VDIM = 8, 1024, 512, 128, 64, 128
D_HEADS, D_HDIM = 8, 64
GROUP_WIDTH = 1024
CONV_W = 3
ROPE_BASE = 10000.0
NEG = -1e30
IN_SIZES = (1024, 128, 128, 1024, 64, 16, 1024, 128, 128, 1024, 512, 64, 1024, 1024, 1024)

LANES = 128
BF16_SUBLANES = 16
MXU_DEPTH = 256
VMEM_LIMIT = 56 * 1024 * 1024
FF_TILE = 256
N_EXTENT_VARIANTS = 6
LOG2E = 1.4426950408889634

F32 = jnp.float32
BF16 = jnp.bfloat16


def _cparams(n_axes):
    return pltpu.CompilerParams(dimension_semantics=("arbitrary",) * n_axes, vmem_limit_bytes=VMEM_LIMIT)


def _row_tile(p_len, target):
    best = BF16_SUBLANES
    for t in range(BF16_SUBLANES, target + 1, BF16_SUBLANES):
        if p_len % t == 0:
            best = t
    return best


def _dot(a, b):
    return jnp.dot(a, b, preferred_element_type=F32)


def _dot_nt(a, b):
    return lax.dot_general(a, b, (((1,), (1,)), ((), ())), preferred_element_type=F32)


def _ln_rows(x, g, b):
    mu = jnp.mean(x, axis=-1, keepdims=True)
    xc = x - mu
    var = jnp.mean(xc * xc, axis=-1, keepdims=True)
    return xc * lax.rsqrt(var + 1e-5) * g + b


def _ln_in_kernel(x_ref, meta_ref, g_ref, b_ref, h_ref, hb_ref):
    def emit(rows):
        y = _ln_rows(rows, g_ref[...], b_ref[...])
        h_ref[...] = y
        hb_ref[...] = y.astype(BF16)

    @pl.when(pl.program_id(1) == 0)
    def _():
        emit(jnp.concatenate([jnp.zeros((PAD_FRONT, meta_ref.shape[1]), F32), meta_ref[...]], axis=0))

    @pl.when(pl.program_id(1) > 0)
    def _():
        emit(x_ref[...])


def _ln_in(x, meta, g, b):
    batch, s_len, d = x.shape
    nb = s_len // BLK + 1
    t = batch * nb * BLK
    return pl.pallas_call(
        _ln_in_kernel,
        grid=(batch, nb),
        in_specs=[pl.BlockSpec((None, BLK, d), lambda bi, n: (bi, jnp.maximum(n - 1, 0), 0)),
                  pl.BlockSpec((N_META, d), lambda bi, n: (0, 0)),
                  pl.BlockSpec((1, d), lambda bi, n: (0, 0)),
                  pl.BlockSpec((1, d), lambda bi, n: (0, 0))],
        out_specs=[pl.BlockSpec((BLK, d), lambda bi, n: (bi * nb + n, 0)),
                   pl.BlockSpec((BLK, d), lambda bi, n: (bi * nb + n, 0))],
        out_shape=[jax.ShapeDtypeStruct((t, d), F32), jax.ShapeDtypeStruct((t, d), BF16)],
        compiler_params=_cparams(2),
        name="ln_in",
    )(x, meta.astype(F32), g.reshape(1, d), b.reshape(1, d))


def _res_ln_kernel(h_ref, y_ref, g_ref, b_ref, o_ref, ob_ref, *, alpha, tm, tiles_per_seq, zero_pad):
    out = _ln_rows(alpha * h_ref[...] + y_ref[...], g_ref[...], b_ref[...])
    o_ref[...] = out
    if zero_pad:
        p0 = (pl.program_id(0) % tiles_per_seq) * tm
        pos = p0 + lax.broadcasted_iota(jnp.int32, out.shape, 0)
        out = jnp.where(pos >= PAD_FRONT, out, 0.0)
    ob_ref[...] = out.astype(BF16)


def _res_ln(h, y, g, b, alpha, tm, p_len, zero_pad):
    t, d = h.shape
    kern = functools.partial(_res_ln_kernel, alpha=alpha, tm=tm, tiles_per_seq=p_len // tm, zero_pad=zero_pad)
    return pl.pallas_call(
        kern,
        grid=(t // tm,),
        in_specs=[pl.BlockSpec((tm, d), lambda i: (i, 0)),
                  pl.BlockSpec((tm, d), lambda i: (i, 0)),
                  pl.BlockSpec((1, d), lambda i: (0, 0)),
                  pl.BlockSpec((1, d), lambda i: (0, 0))],
        out_specs=[pl.BlockSpec((tm, d), lambda i: (i, 0)),
                   pl.BlockSpec((tm, d), lambda i: (i, 0))],
        out_shape=[jax.ShapeDtypeStruct((t, d), F32), jax.ShapeDtypeStruct((t, d), BF16)],
        compiler_params=_cparams(1),
        name="res_ln",
    )(h, y, g.reshape(1, d), b.reshape(1, d))


def _res_ln_final_kernel(h_ref, y_ref, g_ref, b_ref, o_ref, *, alpha):
    o_ref[...] = _ln_rows(alpha * h_ref[...] + y_ref[...], g_ref[...], b_ref[...])


def _res_ln_final(h, y, g, b, alpha, batch, p_len):
    t, d = h.shape
    nb = p_len // BLK
    out_idx = lambda i: ((i // nb) * (nb - 1) + jnp.maximum(i % nb - 1, 0), 0)
    return pl.pallas_call(
        functools.partial(_res_ln_final_kernel, alpha=alpha),
        grid=(t // BLK,),
        in_specs=[pl.BlockSpec((BLK, d), lambda i: (i, 0)),
                  pl.BlockSpec((BLK, d), lambda i: (i, 0)),
                  pl.BlockSpec((1, d), lambda i: (0, 0)),
                  pl.BlockSpec((1, d), lambda i: (0, 0))],
        out_specs=pl.BlockSpec((BLK, d), out_idx),
        out_shape=jax.ShapeDtypeStruct((batch * (p_len - BLK), d), F32),
        compiler_params=_cparams(1),
        name="res_ln_final",
    )(h, y, g.reshape(1, d), b.reshape(1, d))


def _mm_kernel(a_ref, w_ref, o_ref):
    o_ref[...] = _dot(a_ref[...], w_ref[...]).astype(o_ref.dtype)


def _mm(a, w, tm, tn, out_dtype, name):
    t, k = a.shape
    n = w.shape[1]
    return pl.pallas_call(
        _mm_kernel,
        grid=(n // tn, t // tm),
        in_specs=[pl.BlockSpec((tm, k), lambda j, i: (i, 0)),
                  pl.BlockSpec((k, tn), lambda j, i: (0, j))],
        out_specs=pl.BlockSpec((tm, tn), lambda j, i: (i, j)),
        out_shape=jax.ShapeDtypeStruct((t, n), out_dtype),
        compiler_params=_cparams(2),
        name=name,
    )(a, w)


def _cast_kernel(w_ref, o_ref):
    o_ref[...] = w_ref[...].astype(o_ref.dtype)


def _cast_layer(w, layer, dtype):
    _, r, c = w.shape
    rb = _row_tile(r, 688)
    return pl.pallas_call(
        _cast_kernel,
        grid=(r // rb,),
        in_specs=[pl.BlockSpec((None, rb, c), lambda i: (layer, i, 0))],
        out_specs=pl.BlockSpec((rb, c), lambda i: (i, 0)),
        out_shape=jax.ShapeDtypeStruct((r, c), dtype),
        compiler_params=_cparams(1),
        name="cast_w",
    )(w)


def _mm_groups_kernel(*refs):
    *a_refs, w_ref, o_ref = refs
    k0 = 0
    acc = None
    for a_ref in a_refs:
        kw = a_ref.shape[1]
        part = _dot(a_ref[...], w_ref[k0:k0 + kw, :])
        acc = part if acc is None else acc + part
        k0 += kw
    o_ref[...] = acc.astype(o_ref.dtype)


def _mm_groups(a_list, w, tm, tn, out_dtype, name):
    t = a_list[0].shape[0]
    k, n = w.shape
    assert sum(a.shape[1] for a in a_list) == k
    return pl.pallas_call(
        _mm_groups_kernel,
        grid=(n // tn, t // tm),
        in_specs=[pl.BlockSpec((tm, a.shape[1]), lambda j, i: (i, 0)) for a in a_list]
        + [pl.BlockSpec((k, tn), lambda j, i: (0, j))],
        out_specs=pl.BlockSpec((tm, tn), lambda j, i: (i, j)),
        out_shape=jax.ShapeDtypeStruct((t, n), out_dtype),
        compiler_params=_cparams(2),
        name=name,
    )(*a_list, w)


def _mm_slab_kernel(a_ref, w_ref, o_ref):
    res = _dot(a_ref[...], w_ref[...])
    for c in range(o_ref.shape[0]):
        o_ref[c] = res[:, c * LANES:(c + 1) * LANES].astype(o_ref.dtype)


def _mm_slab(a, w, tm, tn, out_dtype, name):
    t, k = a.shape
    n = w.shape[1]
    return pl.pallas_call(
        _mm_slab_kernel,
        grid=(n // tn, t // tm),
        in_specs=[pl.BlockSpec((tm, k), lambda j, i: (i, 0)),
                  pl.BlockSpec((k, tn), lambda j, i: (0, j))],
        out_specs=pl.BlockSpec((tn // LANES, tm, LANES), lambda j, i: (j, i, 0)),
        out_shape=jax.ShapeDtypeStruct((n // LANES, t, LANES), out_dtype),
        compiler_params=_cparams(2),
        name=name,
    )(a, w)


def _rope128(x, c, s1, s2, half):
    return x * c + pltpu.roll(x, LANES - half, 1) * s1 + pltpu.roll(x, half, 1) * s2


def _rope_tables(pos, rot_dim, period):
    half = rot_dim // 2
    inv = ROPE_BASE ** (-jnp.arange(half, dtype=F32) / half)
    ang = pos.astype(F32)[:, None] * inv[None]
    cos, sin = jnp.cos(ang), jnp.sin(ang)
    n = pos.shape[0]
    lane = np.arange(LANES) % period
    li = lane % half
    cos_l, sin_l = cos[:, li], sin[:, li]
    first = jnp.asarray(lane < half)[None]
    second = jnp.asarray((lane >= half) & (lane < rot_dim))[None]
    c = jnp.where(first | second, cos_l, 1.0)
    s1 = jnp.where(first, -sin_l, 0.0)
    s2 = jnp.where(second, sin_l, 0.0)
    return jnp.stack([c, s1, s2]).astype(F32).reshape(3, n, LANES)


def _rms(x, g, eps=1e-6):
    return x * lax.rsqrt(jnp.mean(x * x, axis=-1, keepdims=True) + eps) * g


def _key_extents(nq, tq, p_len, nvar):
    step = -(-nq // nvar)
    out = []
    for lo in range(0, nq, step):
        hi = min(lo + step, nq)
        out.append((lo, hi, min(p_len, -(-(hi * tq) // BLK) * BLK)))
    return out


def _for_each_extent(qi, nq, tq, p_len, body, nvar=N_EXTENT_VARIANTS):
    for lo, hi, ext in _key_extents(nq, tq, p_len, nvar):
        @pl.when(jnp.logical_and(qi >= lo, qi < hi))
        def _():
            body(ext)


def _attn_a_kernel(iq_ref, wq_ref, ikw_ref, tq_ref, tk_ref, aq_ref, ak_ref, av_ref, bias_ref, g_ref,
                   o_ref, key_ref, mask_ref, qs_ref, os_ref, lo_ref, cand_ref, cnt_ref, *, tq, p_len, k_top):
    qi = pl.program_id(1)
    q0 = qi * tq
    half = IDX_ROPE // 2
    for h in range(A_HEADS):
        qs_ref[h * tq:(h + 1) * tq, :] = aq_ref[:, h * A_HDIM:(h + 1) * A_HDIM]

    def body(ext):
        ik = _rope128(ikw_ref[:ext, :], tk_ref[0, :ext, :], tk_ref[1, :ext, :], tk_ref[2, :ext, :],
                      half)[:, :IDX_HDIM].astype(BF16)
        w = wq_ref[...][:, IDX_HDIM:IDX_HDIM + IDX_HEADS] * ((IDX_HEADS * IDX_HDIM) ** -0.5)
        tc, ts1, ts2 = tq_ref[0], tq_ref[1], tq_ref[2]
        score = jnp.zeros((tq, ext), F32)
        for c in range(IDX_HEADS * IDX_HDIM // LANES):
            chunk = _rope128(iq_ref[:, c * LANES:(c + 1) * LANES], tc, ts1, ts2, half).astype(BF16)
            for j in range(LANES // IDX_HDIM):
                h = c * (LANES // IDX_HDIM) + j
                rel = jnp.maximum(_dot_nt(chunk[:, j * IDX_HDIM:(j + 1) * IDX_HDIM], ik), 0.0)
                score = score + w[:, h:h + 1] * rel
        qpos = q0 + lax.broadcasted_iota(jnp.int32, (tq, ext), 0)
        kpos = lax.broadcasted_iota(jnp.int32, (tq, ext), 1)
        vis = jnp.where(kpos <= qpos, jnp.where(kpos >= PAD_FRONT, 1.0, 0.0), 0.0)
        score = jnp.where(vis > 0.0, score, NEG)
        bits = lax.bitcast_convert_type(score, jnp.int32)
        key_ref[:, :ext] = jnp.where(bits < 0, bits ^ jnp.int32(0x7FFFFFFF), bits)
        kf = jnp.float32(k_top)

        def count(pred):
            return jnp.sum(jnp.where(pred, 1.0, 0.0), axis=1, keepdims=True)

        hs = tq // 2

        def lane_counts(r0, cand):
            terms = [jnp.where(key_ref[r0:r0 + hs, c * LANES:(c + 1) * LANES] >= cand, 1.0, 0.0)
                     for c in range(ext // LANES)]
            while len(terms) > 1:
                terms = [terms[i] + terms[i + 1] if i + 1 < len(terms) else terms[i] for i in range(0, len(terms), 2)]
            return terms[0]

        lo_ref[...] = jnp.full((tq, LANES), -2 ** 31, jnp.int32)
        cand_ref[...] = jnp.full((hs, LANES), -2 ** 31, jnp.int32)
        cnt_ref[...] = jnp.zeros((hs, LANES), F32)

        def finish_second_half():
            tot = jnp.sum(cnt_ref[...], axis=1, keepdims=True)
            return jnp.where(tot >= kf, cand_ref[...], lo_ref[hs:, :])

        def vbody(i, carry):
            inc = jnp.left_shift(jnp.int32(1), 31 - i)
            lo_b = finish_second_half()
            lo_a = lo_ref[:hs, :]
            cand_a = lo_a + inc
            tot_a = jnp.sum(lane_counts(0, cand_a), axis=1, keepdims=True)
            cand_b = lo_b + inc
            cnt_ref[...] = lane_counts(hs, cand_b)
            cand_ref[...] = cand_b
            lo_ref[hs:, :] = lo_b
            lo_ref[:hs, :] = jnp.where(tot_a >= kf, cand_a, lo_a)
            return carry

        lax.fori_loop(0, 32, vbody, 0)
        lo_ref[hs:, :] = finish_second_half()
        thr = lo_ref[:, 0:1]
        key = key_ref[:, :ext]
        n_gt = count(key > thr)
        n_eq = count(key == thr)
        need = kf - n_gt
        n_eq_vis = jnp.sum(jnp.where(key == thr, vis, 0.0), axis=1, keepdims=True)
        mask_ref[:, :ext] = jnp.where(key >= thr, jnp.where(vis > 0.0, 0.0, NEG), NEG)
        surplus = jnp.where(n_eq_vis > 0.0, jnp.where(n_eq > need, 1.0, 0.0), 0.0)

        @pl.when(jnp.max(surplus) > 0.0)
        def _():
            nbits = max(1, (ext - 1).bit_length())
            keyv = key_ref[:, :ext]
            col = lax.broadcasted_iota(jnp.int32, (tq, ext), 1)

            def tbody(i, j):
                cand = j + jnp.left_shift(jnp.int32(1), nbits - 1 - i)
                f = jnp.sum(jnp.where(keyv == thr, jnp.where(col < cand, 1.0, 0.0), 0.0), axis=1, keepdims=True)
                return jnp.where(f < need, cand, j)

            jstar = lax.fori_loop(0, nbits, tbody, jnp.zeros((tq, 1), jnp.int32))
            chosen = jnp.where(keyv > thr, 1.0, jnp.where(keyv == thr, jnp.where(col <= jstar, 1.0, 0.0), 0.0))
            mask_ref[:, :ext] = jnp.where(chosen > 0.0, jnp.where(vis > 0.0, 0.0, NEG), NEG)

        scale2 = A_HDIM ** -0.5 * LOG2E

        def hbody(h, carry):
            r0 = pl.multiple_of(h * tq, tq)
            s = _dot_nt(qs_ref[pl.ds(r0, tq), :], ak_ref[:ext, :]) * scale2 + bias_ref[h, :, :ext] + mask_ref[:, :ext]
            m = jnp.max(s, axis=1, keepdims=True)
            e = jnp.exp2(s - m)
            l = jnp.sum(e, axis=1, keepdims=True)
            os_ref[pl.ds(r0, tq), :] = _dot(e.astype(BF16), av_ref[:ext, :]) / l
            return carry

        lax.fori_loop(0, A_HEADS, hbody, 0, unroll=2)

    _for_each_extent(qi, p_len // tq, tq, p_len, body)
    o = jnp.concatenate([os_ref[h * tq:(h + 1) * tq, :] for h in range(A_HEADS)], axis=1)
    o_ref[...] = _rms(o, g_ref[...]).astype(BF16)


def _attn_a(pf, pb, tabs_idx, bias, g, batch, p_len, tq, k_top, col):
    nq = p_len // tq
    t = batch * p_len
    kern = functools.partial(_attn_a_kernel, tq=tq, p_len=p_len, k_top=k_top)
    return pl.pallas_call(
        kern,
        grid=(batch, nq),
        in_specs=[
            pl.BlockSpec((tq, 1024), lambda b, q: (b * nq + q, col["i_q"] // 1024)),
            pl.BlockSpec((tq, LANES), lambda b, q: (b * nq + q, col["ikw"] // LANES)),
            pl.BlockSpec((p_len, LANES), lambda b, q: (b, col["ikw"] // LANES)),
            pl.BlockSpec((3, tq, LANES), lambda b, q: (0, q, 0)),
            pl.BlockSpec((3, p_len, LANES), lambda b, q: (0, 0, 0)),
            pl.BlockSpec((tq, 1024), lambda b, q: (b * nq + q, col["a_q"] // 1024)),
            pl.BlockSpec((p_len, LANES), lambda b, q: (b, col["a_k"] // LANES)),
            pl.BlockSpec((p_len, LANES), lambda b, q: (b, col["a_v"] // LANES)),
            pl.BlockSpec((A_HEADS, 1, p_len), lambda b, q: (0, 0, 0)),
            pl.BlockSpec((1, GROUP_WIDTH), lambda b, q: (0, 0)),
        ],
        out_specs=pl.BlockSpec((tq, GROUP_WIDTH), lambda b, q: (b * nq + q, 0)),
        out_shape=jax.ShapeDtypeStruct((t, GROUP_WIDTH), BF16),
        scratch_shapes=[pltpu.VMEM((tq, p_len), jnp.int32),
                        pltpu.VMEM((tq, p_len), F32),
                        pltpu.VMEM((A_HEADS * tq, A_HDIM), BF16),
                        pltpu.VMEM((A_HEADS * tq, A_HDIM), F32),
                        pltpu.VMEM((tq, LANES), jnp.int32),
                        pltpu.VMEM((tq // 2, LANES), jnp.int32),
                        pltpu.VMEM((tq // 2, LANES), F32)],
        compiler_params=_cparams(2),
        name="attn_a",
    )(pf, pf, pf, tabs_idx, tabs_idx, pb, pb, pb, bias, g.reshape(1, GROUP_WIDTH))


def _attn_b_kernel(q_ref, kp_ref, kc_ref, ksp_ref, ksc_ref, vp_ref, vc_ref, vsp_ref, vsc_ref, bias_ref, sink_ref, g_ref, o_ref):
    n = pl.program_id(1)
    grp = B_HEADS // B_KV_HEADS
    slabs = grp // 2
    rows = slabs * BLK
    first_col = PAD_FRONT - (n - 1) * BLK
    col = lax.broadcasted_iota(jnp.int32, (rows, 2 * BLK), 1)
    pad_mask = jnp.where(col >= first_col, 0.0, NEG)
    lane = lax.broadcasted_iota(jnp.int32, (2 * BLK, LANES), 1)
    low = lane < B_HDIM
    zero = jnp.zeros((2 * BLK, LANES), BF16)
    out_slabs = []
    for gi in range(B_KV_HEADS):
        q = jnp.concatenate([q_ref[:, (gi * slabs + sl) * LANES:(gi * slabs + sl + 1) * LANES] for sl in range(slabs)], axis=0)
        kplain = jnp.concatenate([kp_ref[...], kc_ref[...]], axis=0)
        kswap = jnp.concatenate([ksp_ref[...], ksc_ref[...]], axis=0)
        vplain = jnp.concatenate([vp_ref[...], vc_ref[...]], axis=0)
        vswap = jnp.concatenate([vsp_ref[...], vsc_ref[...]], axis=0)
        lo_src_k, hi_src_k = (kplain, kswap) if gi == 0 else (kswap, kplain)
        lo_src_v, hi_src_v = (vplain, vswap) if gi == 0 else (vswap, vplain)
        acc = None
        for par in range(2):
            keep = low if par == 0 else jnp.logical_not(low)
            k = jnp.where(keep, lo_src_k if par == 0 else hi_src_k, zero)
            v = jnp.where(keep, lo_src_v if par == 0 else hi_src_v, zero)
            s = _dot_nt(q, k) + bias_ref[gi, par] + pad_mask
            sink = sink_ref[gi, par]
            m = jnp.maximum(jnp.max(s, axis=1, keepdims=True), sink)
            e = jnp.exp(s - m)
            l = jnp.sum(e, axis=1, keepdims=True) + jnp.exp(sink - m)
            part = _dot(e.astype(BF16), v) * (1.0 / l)
            acc = part if acc is None else acc + part
        out_slabs.extend(acc[sl * BLK:(sl + 1) * BLK, :] for sl in range(slabs))
    o = jnp.concatenate(out_slabs, axis=1)
    o_ref[...] = _rms(o, g_ref[...]).astype(BF16)


def _attn_b(pb, bias, sink_rows, g, batch, p_len, col):
    nb = p_len // BLK
    t = batch * p_len
    grp = B_HEADS // B_KV_HEADS
    rows = grp // 2 * BLK
    prev = lambda b, n: b * nb + jnp.maximum(n - 1, 0)
    cur = lambda b, n: b * nb + n
    kv_specs = []
    for name in ("b_k", "b_ks", "b_v", "b_vs"):
        c = col[name] // LANES
        kv_specs.append(pl.BlockSpec((BLK, LANES), lambda b, n, c=c: (prev(b, n), c)))
        kv_specs.append(pl.BlockSpec((BLK, LANES), lambda b, n, c=c: (cur(b, n), c)))
    return pl.pallas_call(
        _attn_b_kernel,
        grid=(batch, nb),
        in_specs=[pl.BlockSpec((BLK, 1024), lambda b, n: (cur(b, n), col["b_q"] // 1024))] + kv_specs + [
            pl.BlockSpec((B_KV_HEADS, 2, rows, 2 * BLK), lambda b, n: (0, 0, 0, 0)),
            pl.BlockSpec((B_KV_HEADS, 2, rows, 1), lambda b, n: (0, 0, 0, 0)),
            pl.BlockSpec((1, GROUP_WIDTH), lambda b, n: (0, 0)),
        ],
        out_specs=pl.BlockSpec((BLK, GROUP_WIDTH), lambda b, n: (cur(b, n), 0)),
        out_shape=jax.ShapeDtypeStruct((t, GROUP_WIDTH), BF16),
        compiler_params=_cparams(2),
        name="attn_b",
    )(pb, *([pb] * 8), bias, sink_rows, g.reshape(1, GROUP_WIDTH))


def _c_prep_kernel(cq_ref, ckv_ref, kr_ref, tab_ref, gq_ref, gkv_ref, wq_ref, wkv_ref, q_ref, k_ref, v_ref):
    half = C_ROPE // 2
    tc, ts1, ts2 = tab_ref[0], tab_ref[1], tab_ref[2]
    slot = C_NOPE + LANES
    xq = _rms(cq_ref[...], gq_ref[...]).astype(BF16)
    q = _dot(xq, wq_ref[...])
    xkv = _rms(ckv_ref[...], gkv_ref[...]).astype(BF16)
    kv = _dot(xkv, wkv_ref[...])
    kr = _rope128(kr_ref[...], tc, ts1, ts2, half).astype(BF16)
    for h in range(C_HEADS):
        lo = h * slot
        q_ref[h, :, :C_NOPE] = q[:, lo:lo + C_NOPE].astype(BF16)
        q_ref[h, :, C_NOPE:] = _rope128(q[:, lo + C_NOPE:lo + slot], tc, ts1, ts2, half).astype(BF16)
        k_ref[h, :, :C_NOPE] = kv[:, h * C_NOPE:(h + 1) * C_NOPE].astype(BF16)
        k_ref[h, :, C_NOPE:] = kr
        v_ref[h] = kv[:, (C_HEADS + h) * C_NOPE:(C_HEADS + h + 1) * C_NOPE].astype(BF16)


def _c_prep(pf, tabs, g_cq, g_ckv, w_uq_p, w_ukv_p, p_len, tm, col):
    t = pf.shape[0]
    tiles = p_len // tm
    slot = C_NOPE + LANES
    return pl.pallas_call(
        _c_prep_kernel,
        grid=(t // tm,),
        in_specs=[
            pl.BlockSpec((tm, C_Q_RANK), lambda i: (i, col["c_cq"] // C_Q_RANK)),
            pl.BlockSpec((tm, C_KV_RANK), lambda i: (i, col["c_ckv"] // C_KV_RANK)),
            pl.BlockSpec((tm, LANES), lambda i: (i, col["c_kr"] // LANES)),
            pl.BlockSpec((3, tm, LANES), lambda i: (0, i % tiles, 0)),
            pl.BlockSpec((1, C_Q_RANK), lambda i: (0, 0)),
            pl.BlockSpec((1, C_KV_RANK), lambda i: (0, 0)),
            pl.BlockSpec((C_Q_RANK, C_HEADS * slot), lambda i: (0, 0)),
            pl.BlockSpec((C_KV_RANK, C_HEADS * (C_NOPE + C_VDIM)), lambda i: (0, 0)),
        ],
        out_specs=[pl.BlockSpec((C_HEADS, tm, slot), lambda i: (0, i, 0)),
                   pl.BlockSpec((C_HEADS, tm, slot), lambda i: (0, i, 0)),
                   pl.BlockSpec((C_HEADS, tm, C_VDIM), lambda i: (0, i, 0))],
        out_shape=[jax.ShapeDtypeStruct((C_HEADS, t, slot), BF16),
                   jax.ShapeDtypeStruct((C_HEADS, t, slot), BF16),
                   jax.ShapeDtypeStruct((C_HEADS, t, C_VDIM), BF16)],
        compiler_params=_cparams(1),
        name="c_prep",
    )(pf, pf, pf, tabs, g_cq.reshape(1, -1), g_ckv.reshape(1, -1), w_uq_p, w_ukv_p)


def _causal_mask(q0, tq, ext):
    qpos = q0 + lax.broadcasted_iota(jnp.int32, (tq, ext), 0)
    kpos = lax.broadcasted_iota(jnp.int32, (tq, ext), 1)
    return jnp.where(kpos <= qpos, jnp.where(kpos >= PAD_FRONT, 0.0, NEG), NEG)


def _attn_c_kernel(q_ref, k_ref, v_ref, g_ref, o_ref, os_ref, *, tq, p_len):
    qi = pl.program_id(1)
    scale2 = (C_NOPE + C_ROPE) ** -0.5 * LOG2E

    def body(ext):
        mask = _causal_mask(qi * tq, tq, ext)

        def hbody(h, carry):
            s = _dot_nt(q_ref[h], k_ref[h, :ext, :]) * scale2 + mask
            m = jnp.max(s, axis=1, keepdims=True)
            e = jnp.exp2(s - m)
            l = jnp.sum(e, axis=1, keepdims=True)
            os_ref[h] = _dot(e.astype(BF16), v_ref[h, :ext, :]) / l
            return carry

        lax.fori_loop(0, C_HEADS, hbody, 0, unroll=2)

    nq = p_len // tq
    _for_each_extent(qi, nq, tq, p_len, body, nq)
    o = jnp.concatenate([os_ref[h] for h in range(C_HEADS)], axis=1)
    o_ref[...] = _rms(o, g_ref[...]).astype(BF16)


def _attn_c(qc, kc, vc, g, batch, p_len, tq):
    nq = p_len // tq
    t = batch * p_len
    slot = C_NOPE + LANES
    kern = functools.partial(_attn_c_kernel, tq=tq, p_len=p_len)
    return pl.pallas_call(
        kern,
        grid=(batch, nq),
        in_specs=[pl.BlockSpec((C_HEADS, tq, slot), lambda b, q: (0, b * nq + q, 0)),
                  pl.BlockSpec((C_HEADS, p_len, slot), lambda b, q: (0, b, 0)),
                  pl.BlockSpec((C_HEADS, p_len, C_VDIM), lambda b, q: (0, b, 0)),
                  pl.BlockSpec((1, GROUP_WIDTH), lambda b, q: (0, 0))],
        out_specs=pl.BlockSpec((tq, GROUP_WIDTH), lambda b, q: (b * nq + q, 0)),
        out_shape=jax.ShapeDtypeStruct((t, GROUP_WIDTH), BF16),
        scratch_shapes=[pltpu.VMEM((C_HEADS, tq, C_VDIM), F32)],
        compiler_params=_cparams(2),
        name="attn_c",
    )(qc, kc, vc, g.reshape(1, GROUP_WIDTH))


def _attn_d_kernel(q_ref, k_ref, v_ref, bias_ref, lam_ref, g_ref, o_ref, os_ref, *, tq, p_len, lam_init):
    qi = pl.program_id(1)
    lp = lam_ref[...]
    lam = (jnp.exp(jnp.sum(lp[0:1] * lp[1:2], axis=1, keepdims=True))
           - jnp.exp(jnp.sum(lp[2:3] * lp[3:4], axis=1, keepdims=True)) + lam_init)

    def body(ext):
        mask = _causal_mask(qi * tq, tq, ext)

        def hbody(h, carry):
            bm = bias_ref[h, :, :ext] + mask
            qh = q_ref[h]
            kh = k_ref[h, :ext, :]
            es, ls = [], []
            for c in range(2):
                lo = c * D_HDIM
                s = _dot_nt(qh[:, lo:lo + D_HDIM], kh[:, lo:lo + D_HDIM]) + bm
                m = jnp.max(s, axis=1, keepdims=True)
                e = jnp.exp(s - m)
                es.append(e)
                ls.append(jnp.sum(e, axis=1, keepdims=True))
            a = es[0] * (1.0 / ls[0]) - es[1] * (lam / ls[1])
            os_ref[h] = _dot(a.astype(BF16), v_ref[h, :ext, :])
            return carry

        lax.fori_loop(0, D_HEADS, hbody, 0)

    nq = p_len // tq
    _for_each_extent(qi, nq, tq, p_len, body, nq)
    g = g_ref[...]
    outs = [_rms(os_ref[h], g) * (1.0 - lam_init) for h in range(D_HEADS)]
    o_ref[...] = jnp.concatenate(outs, axis=1).astype(BF16)


def _attn_d(pd, bias, lam_p, g, batch, p_len, tq, lam_init):
    nq = p_len // tq
    t = batch * p_len
    kern = functools.partial(_attn_d_kernel, tq=tq, p_len=p_len, lam_init=lam_init)
    return pl.pallas_call(
        kern,
        grid=(batch, nq),
        in_specs=[pl.BlockSpec((D_HEADS, tq, LANES), lambda b, q: (0, b * nq + q, 0)),
                  pl.BlockSpec((D_HEADS, p_len, LANES), lambda b, q: (1, b, 0)),
                  pl.BlockSpec((D_HEADS, p_len, LANES), lambda b, q: (2, b, 0)),
                  pl.BlockSpec((D_HEADS, 1, p_len), lambda b, q: (0, 0, 0)),
                  pl.BlockSpec((4, D_HDIM), lambda b, q: (0, 0)),
                  pl.BlockSpec((1, 2 * D_HDIM), lambda b, q: (0, 0))],
        out_specs=pl.BlockSpec((tq, GROUP_WIDTH), lambda b, q: (b * nq + q, 0)),
        out_shape=jax.ShapeDtypeStruct((t, GROUP_WIDTH), BF16),
        scratch_shapes=[pltpu.VMEM((D_HEADS, tq, 2 * D_HDIM), F32)],
        compiler_params=_cparams(2),
        name="attn_d",
    )(pd, pd, pd, bias, lam_p, g.reshape(1, 2 * D_HDIM))


def _ffn_up_kernel(x_ref, wg_ref, wu_ref, cwg_ref, cwu_ref, cbg_ref, cbu_ref, o_ref,
                   wgb_ref, wub_ref, ha_ref, hb_ref, carry_ref, *, tm, n_row_tiles, tiles_per_seq):
    s = pl.program_id(0)
    d = x_ref.shape[1]
    kc = min(d, MXU_DEPTH)
    nk = d // kc
    rc = next(r for r in (64, 32, 16, 8) if tm % r == 0)
    nr = tm // rc

    @pl.when(s % n_row_tiles == 0)
    def _():
        wgb_ref[...] = wg_ref[...].astype(BF16)
        wub_ref[...] = wu_ref[...].astype(BF16)

    @pl.when(s == 0)
    def _():
        hb_ref[...] = jnp.zeros_like(hb_ref)

    @pl.when((s + tiles_per_seq - 1) % tiles_per_seq == 0)
    def _():
        carry_ref[...] = jnp.zeros_like(carry_ref)

    def run(rd_ref, wr_ref):
        cws = (cwg_ref[...], cwu_ref[...])
        cbs = (cbg_ref[...], cbu_ref[...])
        for step in range(nr):
            for kk in range((step * nk) // nr, ((step + 1) * nk) // nr):
                for b, wb_ref in enumerate((wgb_ref, wub_ref)):
                    part = _dot(x_ref[:, kk * kc:(kk + 1) * kc], wb_ref[kk * kc:(kk + 1) * kc, :])
                    if kk == 0:
                        wr_ref[b] = part
                    else:
                        wr_ref[b] = wr_ref[b] + part
            r0 = step * rc
            ys = []
            for b in range(2):
                top = carry_ref[b] if step == 0 else rd_ref[b, r0 - 8:r0, :]
                cur = rd_ref[b, r0:r0 + rc, :]
                full = jnp.concatenate([top, cur], axis=0)
                cw = cws[b]
                y = cw[2:3] * cur + cw[1:2] * pltpu.roll(full, 1, 0)[8:] + cw[0:1] * pltpu.roll(full, 2, 0)[8:]
                ys.append(y + cbs[b])
            o_ref[r0:r0 + rc, :] = (ys[0] * jax.nn.sigmoid(ys[0]) * ys[1]).astype(BF16)
        for b in range(2):
            carry_ref[b] = rd_ref[b, tm - 8:, :]

    @pl.when(s % 2 == 0)
    def _():
        run(hb_ref, ha_ref)

    @pl.when(s % 2 == 1)
    def _():
        run(ha_ref, hb_ref)


def _ffn_up(hb, w_up, conv_w, conv_b, layer, p_len, tm, tn):
    t, d = hb.shape
    f = w_up.shape[2] // 2
    assert f % tn == 0 and tm % 8 == 0 and p_len % tm == 0
    nj, nt = f // tn, t // tm
    n_tiles = nj * nt
    kern = functools.partial(_ffn_up_kernel, tm=tm, n_row_tiles=nt, tiles_per_seq=p_len // tm)
    cb = conv_b.reshape(conv_b.shape[0], 1, 2 * f)
    cur = lambda s: jnp.minimum(s, n_tiles - 1)
    prv = lambda s: jnp.maximum(s - 1, 0)
    return pl.pallas_call(
        kern,
        grid=(n_tiles + 1,),
        in_specs=[pl.BlockSpec((tm, d), lambda s: (cur(s) % nt, 0)),
                  pl.BlockSpec((None, d, tn), lambda s: (layer, 0, cur(s) // nt)),
                  pl.BlockSpec((None, d, tn), lambda s: (layer, 0, nj + cur(s) // nt)),
                  pl.BlockSpec((None, CONV_W, tn), lambda s: (layer, 0, prv(s) // nt)),
                  pl.BlockSpec((None, CONV_W, tn), lambda s: (layer, 0, nj + prv(s) // nt)),
                  pl.BlockSpec((None, 1, tn), lambda s: (layer, 0, prv(s) // nt)),
                  pl.BlockSpec((None, 1, tn), lambda s: (layer, 0, nj + prv(s) // nt))],
        out_specs=pl.BlockSpec((tm, tn), lambda s: (prv(s) % nt, prv(s) // nt)),
        out_shape=jax.ShapeDtypeStruct((t, f), BF16),
        scratch_shapes=[pltpu.VMEM((d, tn), BF16), pltpu.VMEM((d, tn), BF16),
                        pltpu.VMEM((2, tm, tn), F32), pltpu.VMEM((2, tm, tn), F32), pltpu.VMEM((2, 8, tn), F32)],
        compiler_params=_cparams(1),
        name="ffn_up",
    )(hb, w_up, w_up, conv_w, conv_w, cb, cb)


def _offsets(names, sizes):
    out, o = {}, 0
    for n, s in zip(names, sizes):
        out[n] = o
        o += s
    return out, o


_BF_NAMES = ("a_q", "b_q", "a_k", "a_v", "b_k", "b_v", "b_ks", "b_vs", "pad0", "pad1")
_BF_SIZES = (1024, 1024, 128, 128, 128, 128, 128, 128, 128, 128)
_D_NAMES = ("d_q", "d_k", "d_v")
_F32_NAMES = ("i_q", "c_cq", "c_ckv", "ikw", "c_kr")
_F32_SIZES = (1024, 1024, 512, 128, 128)
COL_BF, N_BF = _offsets(_BF_NAMES, _BF_SIZES)
COL_F32, N_F32 = _offsets(_F32_NAMES, _F32_SIZES)


def _prep_w_in(w):
    d = w.shape[0]
    seg = dict(zip(("a_q", "a_k", "a_v", "i_q", "i_k", "i_w", "b_q", "b_k", "b_v", "c_cq", "c_ckv", "c_kr",
                    "d_q", "d_k", "d_v"), jnp.split(w, np.cumsum(IN_SIZES)[:-1].tolist(), axis=1)))
    seg["b_q"] = seg["b_q"] * (B_HDIM ** -0.5)
    seg["b_ks"] = jnp.concatenate([seg["b_k"][:, B_HDIM:], seg["b_k"][:, :B_HDIM]], axis=1)
    seg["b_vs"] = jnp.concatenate([seg["b_v"][:, B_HDIM:], seg["b_v"][:, :B_HDIM]], axis=1)
    seg["pad0"] = seg["pad1"] = jnp.zeros((d, LANES), w.dtype)
    seg["d_q"] = seg["d_q"] * (D_HDIM ** -0.5)
    wb = jnp.concatenate([seg[n] for n in _BF_NAMES], axis=1).astype(BF16)
    wd = jnp.concatenate([seg[n] for n in _D_NAMES], axis=1).astype(BF16)
    zeros = lambda n: jnp.zeros((d, n), w.dtype)
    ikw = jnp.concatenate([seg["i_k"], seg["i_w"], zeros(LANES - IDX_HDIM - IDX_HEADS)], axis=1)
    ckr = jnp.concatenate([seg["c_kr"], zeros(LANES - C_ROPE)], axis=1)
    wf = jnp.concatenate([seg["i_q"], seg["c_cq"], seg["c_ckv"], ikw, ckr], axis=1).astype(BF16)
    return wb, wd, wf


def _prep_w_uq(w):
    r = w.shape[0]
    w3 = w.reshape(r, C_HEADS, C_NOPE + C_ROPE)
    return jnp.pad(w3, ((0, 0), (0, 0), (0, LANES - C_ROPE))).reshape(r, C_HEADS * (C_NOPE + LANES)).astype(BF16)


def _prep_w_ukv(w):
    r = w.shape[0]
    w3 = w.reshape(r, C_HEADS, C_NOPE + C_VDIM)
    return jnp.concatenate([w3[:, :, :C_NOPE].reshape(r, -1), w3[:, :, C_NOPE:].reshape(r, -1)], axis=1).astype(BF16)


def _alibi(n):
    return 2.0 ** (-8.0 * np.arange(1, n + 1, dtype=np.float64) / n)


def _swa_bias():
    grp = B_HEADS // B_KV_HEADS
    r = np.arange(BLK)[:, None]
    c = np.arange(2 * BLK)[None, :]
    diff = (r + BLK - c).astype(np.float64)
    ok = (diff >= 0) & (diff < WINDOW)
    slopes = _alibi(B_HEADS).reshape(B_KV_HEADS, grp // 2, 2).transpose(0, 2, 1)
    bias = np.where(ok[None, None, None], -slopes[..., None, None] * diff[None, None, None], NEG)
    return jnp.asarray(bias.reshape(B_KV_HEADS, 2, grp // 2 * BLK, 2 * BLK), F32)


def _swa_sinks(sinks):
    grp = B_HEADS // B_KV_HEADS
    sk = sinks.astype(F32).reshape(B_KV_HEADS, grp // 2, 2).transpose(0, 2, 1)
    return jnp.repeat(sk, BLK, axis=2)[..., None]


def kernel(x, meta_tokens, ln_in_g, ln_in_b, w_in, g_cq, g_ckv, w_uq, w_ukv, sinks, lam_q1, lam_k1, lam_q2, lam_k2,
           g_diff, g_grp, w_out, ln1_g, ln1_b, w_up, conv_w, conv_b, w_down, ln2_g, ln2_b):
    batch, s_len, d = x.shape
    depth = w_in.shape[0]
    p_len = s_len + BLK
    t = batch * p_len
    k_top = min(TOPK_MAX, s_len // 4)
    alpha = (2 * depth) ** 0.25
    tm = _row_tile(p_len, 544)
    tm_ln = _row_tile(p_len, 272)
    tm_ffn = _row_tile(p_len, 1088)
    tq = BLK
    grp = B_HEADS // B_KV_HEADS

    h, hb = _ln_in(x, meta_tokens, ln_in_g, ln_in_b)

    pos = jnp.arange(p_len, dtype=jnp.int32) - PAD_FRONT
    tabs_idx = _rope_tables(pos, IDX_ROPE, IDX_HDIM)
    tabs_c = _rope_tables(pos, C_ROPE, LANES)
    kidx = np.arange(p_len, dtype=np.float64)
    bias_a = jnp.asarray(_alibi(A_HEADS)[:, None, None] * kidx[None, None, :] * LOG2E, F32)
    bias_d = jnp.asarray(_alibi(D_HEADS)[:, None, None] * kidx[None, None, :], F32)
    bias_b = _swa_bias()

    for l in range(depth):
        wb, wd, wf = _prep_w_in(w_in[l])
        pb = _mm(hb, wb, tm_ffn, 2 * MXU_DEPTH, BF16, "proj_bf16")
        pd = _mm_slab(hb, wd, tm_ffn, 2 * MXU_DEPTH, BF16, "proj_d")
        pf = _mm(hb, wf, tm, N_F32 // 2, F32, "proj_f32")

        o_a = _attn_a(pf, pb, tabs_idx, bias_a, g_grp[l, 0], batch, p_len, tq, k_top, {**COL_F32, **COL_BF})
        o_b = _attn_b(pb, bias_b, _swa_sinks(sinks[l]), g_grp[l, 1], batch, p_len, COL_BF)
        qc, kc, vc = _c_prep(pf, tabs_c, g_cq[l], g_ckv[l], _prep_w_uq(w_uq[l]), _prep_w_ukv(w_ukv[l]),
                               p_len, tm_ln, COL_F32)
        o_c = _attn_c(qc, kc, vc, g_grp[l, 2], batch, p_len, tm_ln)
        lam_init = 0.8 - 0.6 * math.exp(-0.3 * l)
        lam_p = jnp.stack([lam_q1[l], lam_k1[l], lam_q2[l], lam_k2[l]]).astype(F32)
        o_d = _attn_d(pd, bias_d, lam_p, g_diff[l], batch, p_len, tm_ln, lam_init)

        mix = _mm_groups([o_a, o_b, o_c, o_d], _cast_layer(w_out, l, BF16), tm, min(1024, d), F32, "out_proj")
        h, hb_ffn = _res_ln(h, mix, ln1_g[l], ln1_b[l], alpha, tm_ln, p_len, True)

        act = _ffn_up(hb_ffn, w_up, conv_w, conv_b, l, p_len, tm_ffn, FF_TILE)
        ffn = _mm(act, _cast_layer(w_down, l, BF16), tm, min(512, d), F32, "ffn_down")
        if l + 1 < depth:
            h, hb = _res_ln(h, ffn, ln2_g[l], ln2_b[l], alpha, tm_ln, p_len, False)
        else:
            out = _res_ln_final(h, ffn, ln2_g[l], ln2_b[l], alpha, batch, p_len)

    return out.reshape(batch, s_len, d)
```

```python
import functools
import math

import jax
import jax.numpy as jnp
import numpy as np
from jax import lax
from jax.experimental import pallas as pl
from jax.experimental.pallas import tpu as pltpu

N_META = 16
BLK = 128
PAD_FRONT = BLK - N_META
A_HEADS, A_HDIM = 8, 128
IDX_HEADS, IDX_HDIM, IDX_ROPE = 16, 64, 32
TOPK_MAX = 256
B_HEADS, B_KV_HEADS, B_HDIM = 16, 2, 64
WINDOW = 128
C_HEADS, C_Q_RANK, C_KV_RANK, C_NOPE, C_ROPE, C_VDIM = 8, 1024, 512, 128, 64, 128
D_HEADS, D_HDIM = 8, 64
GROUP_WIDTH = 1024
CONV_W = 3
ROPE_BASE = 10000.0
NEG = -1e30
IN_SIZES = (1024, 128, 128, 1024, 64, 16, 1024, 128, 128, 1024, 512, 64, 1024, 1024, 1024)

LANES = 128
BF16_SUBLANES = 16
MXU_DEPTH = 256
VMEM_LIMIT = 56 * 1024 * 1024
FF_TILE = 256
N_EXTENT_VARIANTS = 6
LOG2E = 1.4426950408889634

F32 = jnp.float32
BF16 = jnp.bfloat16


def _cparams(n_axes):
    return pltpu.CompilerParams(dimension_semantics=("arbitrary",) * n_axes, vmem_limit_bytes=VMEM_LIMIT)


def _row_tile(p_len, target):
    best = BF16_SUBLANES
    for t in range(BF16_SUBLANES, target + 1, BF16_SUBLANES):
        if p_len % t == 0:
            best = t
    return best


def _dot(a, b):
    return jnp.dot(a, b, preferred_element_type=F32)


def _dot_nt(a, b):
    return lax.dot_general(a, b, (((1,), (1,)), ((), ())), preferred_element_type=F32)


def _ln_rows(x, g, b):
    mu = jnp.mean(x, axis=-1, keepdims=True)
    xc = x - mu
    var = jnp.mean(xc * xc, axis=-1, keepdims=True)
    return xc * lax.rsqrt(var + 1e-5) * g + b


def _ln_in_kernel(x_ref, meta_ref, g_ref, b_ref, h_ref, hb_ref):
    def emit(rows):
        y = _ln_rows(rows, g_ref[...], b_ref[...])
        h_ref[...] = y
        hb_ref[...] = y.astype(BF16)

    @pl.when(pl.program_id(1) == 0)
    def _():
        emit(jnp.concatenate([jnp.zeros((PAD_FRONT, meta_ref.shape[1]), F32), meta_ref[...]], axis=0))

    @pl.when(pl.program_id(1) > 0)
    def _():
        emit(x_ref[...])


def _ln_in(x, meta, g, b):
    batch, s_len, d = x.shape
    nb = s_len // BLK + 1
    t = batch * nb * BLK
    return pl.pallas_call(
        _ln_in_kernel,
        grid=(batch, nb),
        in_specs=[pl.BlockSpec((None, BLK, d), lambda bi, n: (bi, jnp.maximum(n - 1, 0), 0)),
                  pl.BlockSpec((N_META, d), lambda bi, n: (0, 0)),
                  pl.BlockSpec((1, d), lambda bi, n: (0, 0)),
                  pl.BlockSpec((1, d), lambda bi, n: (0, 0))],
        out_specs=[pl.BlockSpec((BLK, d), lambda bi, n: (bi * nb + n, 0)),
                   pl.BlockSpec((BLK, d), lambda bi, n: (bi * nb + n, 0))],
        out_shape=[jax.ShapeDtypeStruct((t, d), F32), jax.ShapeDtypeStruct((t, d), BF16)],
        compiler_params=_cparams(2),
        name="ln_in",
    )(x, meta.astype(F32), g.reshape(1, d), b.reshape(1, d))


def _res_ln_kernel(h_ref, y_ref, g_ref, b_ref, o_ref, ob_ref, *, alpha, tm, tiles_per_seq, zero_pad):
    out = _ln_rows(alpha * h_ref[...] + y_ref[...], g_ref[...], b_ref[...])
    o_ref[...] = out
    if zero_pad:
        p0 = (pl.program_id(0) % tiles_per_seq) * tm
        pos = p0 + lax.broadcasted_iota(jnp.int32, out.shape, 0)
        out = jnp.where(pos >= PAD_FRONT, out, 0.0)
    ob_ref[...] = out.astype(BF16)


def _res_ln(h, y, g, b, alpha, tm, p_len, zero_pad):
    t, d = h.shape
    kern = functools.partial(_res_ln_kernel, alpha=alpha, tm=tm, tiles_per_seq=p_len // tm, zero_pad=zero_pad)
    return pl.pallas_call(
        kern,
        grid=(t // tm,),
        in_specs=[pl.BlockSpec((tm, d), lambda i: (i, 0)),
                  pl.BlockSpec((tm, d), lambda i: (i, 0)),
                  pl.BlockSpec((1, d), lambda i: (0, 0)),
                  pl.BlockSpec((1, d), lambda i: (0, 0))],
        out_specs=[pl.BlockSpec((tm, d), lambda i: (i, 0)),
                   pl.BlockSpec((tm, d), lambda i: (i, 0))],
        out_shape=[jax.ShapeDtypeStruct((t, d), F32), jax.ShapeDtypeStruct((t, d), BF16)],
        compiler_params=_cparams(1),
        name="res_ln",
    )(h, y, g.reshape(1, d), b.reshape(1, d))


def _res_ln_final_kernel(h_ref, y_ref, g_ref, b_ref, o_ref, *, alpha):
    o_ref[...] = _ln_rows(alpha * h_ref[...] + y_ref[...], g_ref[...], b_ref[...])


def _res_ln_final(h, y, g, b, alpha, batch, p_len):
    t, d = h.shape
    nb = p_len // BLK
    out_idx = lambda i: ((i // nb) * (nb - 1) + jnp.maximum(i % nb - 1, 0), 0)
    return pl.pallas_call(
        functools.partial(_res_ln_final_kernel, alpha=alpha),
        grid=(t // BLK,),
        in_specs=[pl.BlockSpec((BLK, d), lambda i: (i, 0)),
                  pl.BlockSpec((BLK, d), lambda i: (i, 0)),
                  pl.BlockSpec((1, d), lambda i: (0, 0)),
                  pl.BlockSpec((1, d), lambda i: (0, 0))],
        out_specs=pl.BlockSpec((BLK, d), out_idx),
        out_shape=jax.ShapeDtypeStruct((batch * (p_len - BLK), d), F32),
        compiler_params=_cparams(1),
        name="res_ln_final",
    )(h, y, g.reshape(1, d), b.reshape(1, d))


def _mm_kernel(a_ref, w_ref, o_ref):
    o_ref[...] = _dot(a_ref[...], w_ref[...]).astype(o_ref.dtype)


def _mm(a, w, tm, tn, out_dtype, name):
    t, k = a.shape
    n = w.shape[1]
    return pl.pallas_call(
        _mm_kernel,
        grid=(n // tn, t // tm),
        in_specs=[pl.BlockSpec((tm, k), lambda j, i: (i, 0)),
                  pl.BlockSpec((k, tn), lambda j, i: (0, j))],
        out_specs=pl.BlockSpec((tm, tn), lambda j, i: (i, j)),
        out_shape=jax.ShapeDtypeStruct((t, n), out_dtype),
        compiler_params=_cparams(2),
        name=name,
    )(a, w)


def _cast_kernel(w_ref, o_ref):
    o_ref[...] = w_ref[...].astype(o_ref.dtype)


def _cast_layer(w, layer, dtype):
    _, r, c = w.shape
    rb = _row_tile(r, 256)
    return pl.pallas_call(
        _cast_kernel,
        grid=(r // rb,),
        in_specs=[pl.BlockSpec((None, rb, c), lambda i: (layer, i, 0))],
        out_specs=pl.BlockSpec((rb, c), lambda i: (i, 0)),
        out_shape=jax.ShapeDtypeStruct((r, c), dtype),
        compiler_params=_cparams(1),
        name="cast_w",
    )(w)


def _mm_groups_kernel(*refs):
    *a_refs, w_ref, o_ref = refs
    k0 = 0
    acc = None
    for a_ref in a_refs:
        kw = a_ref.shape[1]
        part = _dot(a_ref[...], w_ref[k0:k0 + kw, :])
        acc = part if acc is None else acc + part
        k0 += kw
    o_ref[...] = acc.astype(o_ref.dtype)


def _mm_groups(a_list, w, tm, tn, out_dtype, name):
    t = a_list[0].shape[0]
    k, n = w.shape
    assert sum(a.shape[1] for a in a_list) == k
    return pl.pallas_call(
        _mm_groups_kernel,
        grid=(n // tn, t // tm),
        in_specs=[pl.BlockSpec((tm, a.shape[1]), lambda j, i: (i, 0)) for a in a_list]
        + [pl.BlockSpec((k, tn), lambda j, i: (0, j))],
        out_specs=pl.BlockSpec((tm, tn), lambda j, i: (i, j)),
        out_shape=jax.ShapeDtypeStruct((t, n), out_dtype),
        compiler_params=_cparams(2),
        name=name,
    )(*a_list, w)


def _mm_slab_kernel(a_ref, w_ref, o_ref):
    res = _dot(a_ref[...], w_ref[...])
    for c in range(o_ref.shape[0]):
        o_ref[c] = res[:, c * LANES:(c + 1) * LANES].astype(o_ref.dtype)


def _mm_slab(a, w, tm, tn, out_dtype, name):
    t, k = a.shape
    n = w.shape[1]
    return pl.pallas_call(
        _mm_slab_kernel,
        grid=(n // tn, t // tm),
        in_specs=[pl.BlockSpec((tm, k), lambda j, i: (i, 0)),
                  pl.BlockSpec((k, tn), lambda j, i: (0, j))],
        out_specs=pl.BlockSpec((tn // LANES, tm, LANES), lambda j, i: (j, i, 0)),
        out_shape=jax.ShapeDtypeStruct((n // LANES, t, LANES), out_dtype),
        compiler_params=_cparams(2),
        name=name,
    )(a, w)


def _rope128(x, c, s1, s2, half):
    return x * c + pltpu.roll(x, LANES - half, 1) * s1 + pltpu.roll(x, half, 1) * s2


def _rope_tables(pos, rot_dim, period):
    half = rot_dim // 2
    inv = ROPE_BASE ** (-jnp.arange(half, dtype=F32) / half)
    ang = pos.astype(F32)[:, None] * inv[None]
    cos, sin = jnp.cos(ang), jnp.sin(ang)
    n = pos.shape[0]
    lane = np.arange(LANES) % period
    li = lane % half
    cos_l, sin_l = cos[:, li], sin[:, li]
    first = jnp.asarray(lane < half)[None]
    second = jnp.asarray((lane >= half) & (lane < rot_dim))[None]
    c = jnp.where(first | second, cos_l, 1.0)
    s1 = jnp.where(first, -sin_l, 0.0)
    s2 = jnp.where(second, sin_l, 0.0)
    return jnp.stack([c, s1, s2]).astype(F32).reshape(3, n, LANES)


def _rms(x, g, eps=1e-6):
    return x * lax.rsqrt(jnp.mean(x * x, axis=-1, keepdims=True) + eps) * g


def _key_extents(nq, tq, p_len, nvar):
    step = -(-nq // nvar)
    out = []
    for lo in range(0, nq, step):
        hi = min(lo + step, nq)
        out.append((lo, hi, min(p_len, -(-(hi * tq) // BLK) * BLK)))
    return out


def _for_each_extent(qi, nq, tq, p_len, body, nvar=N_EXTENT_VARIANTS):
    for lo, hi, ext in _key_extents(nq, tq, p_len, nvar):
        @pl.when(jnp.logical_and(qi >= lo, qi < hi))
        def _():
            body(ext)


def _attn_a_kernel(iq_ref, wq_ref, ikw_ref, tq_ref, tk_ref, aq_ref, ak_ref, av_ref, bias_ref, g_ref,
                   o_ref, key_ref, mask_ref, qs_ref, os_ref, lo_ref, cand_ref, cnt_ref, *, tq, p_len, k_top):
    qi = pl.program_id(1)
    q0 = qi * tq
    half = IDX_ROPE // 2
    for h in range(A_HEADS):
        qs_ref[h * tq:(h + 1) * tq, :] = aq_ref[:, h * A_HDIM:(h + 1) * A_HDIM]

    def body(ext):
        ik = _rope128(ikw_ref[:ext, :], tk_ref[0, :ext, :], tk_ref[1, :ext, :], tk_ref[2, :ext, :],
                      half)[:, :IDX_HDIM].astype(BF16)
        w = wq_ref[...][:, IDX_HDIM:IDX_HDIM + IDX_HEADS] * ((IDX_HEADS * IDX_HDIM) ** -0.5)
        tc, ts1, ts2 = tq_ref[0], tq_ref[1], tq_ref[2]
        score = jnp.zeros((tq, ext), F32)
        for c in range(IDX_HEADS * IDX_HDIM // LANES):
            chunk = _rope128(iq_ref[:, c * LANES:(c + 1) * LANES], tc, ts1, ts2, half).astype(BF16)
            for j in range(LANES // IDX_HDIM):
                h = c * (LANES // IDX_HDIM) + j
                rel = jnp.maximum(_dot_nt(chunk[:, j * IDX_HDIM:(j + 1) * IDX_HDIM], ik), 0.0)
                score = score + w[:, h:h + 1] * rel
        qpos = q0 + lax.broadcasted_iota(jnp.int32, (tq, ext), 0)
        kpos = lax.broadcasted_iota(jnp.int32, (tq, ext), 1)
        vis = jnp.where(kpos <= qpos, jnp.where(kpos >= PAD_FRONT, 1.0, 0.0), 0.0)
        score = jnp.where(vis > 0.0, score, NEG)
        bits = lax.bitcast_convert_type(score, jnp.int32)
        key_ref[:, :ext] = jnp.where(bits < 0, bits ^ jnp.int32(0x7FFFFFFF), bits)
        kf = jnp.float32(k_top)

        def count(pred):
            return jnp.sum(jnp.where(pred, 1.0, 0.0), axis=1, keepdims=True)

        hs = tq // 2

        def lane_counts(r0, cand):
            terms = [jnp.where(key_ref[r0:r0 + hs, c * LANES:(c + 1) * LANES] >= cand, 1.0, 0.0)
                     for c in range(ext // LANES)]
            while len(terms) > 1:
                terms = [terms[i] + terms[i + 1] if i + 1 < len(terms) else terms[i] for i in range(0, len(terms), 2)]
            return terms[0]

        lo_ref[...] = jnp.full((tq, LANES), -2 ** 31, jnp.int32)
        cand_ref[...] = jnp.full((hs, LANES), -2 ** 31, jnp.int32)
        cnt_ref[...] = jnp.zeros((hs, LANES), F32)

        def finish_second_half():
            tot = jnp.sum(cnt_ref[...], axis=1, keepdims=True)
            return jnp.where(tot >= kf, cand_ref[...], lo_ref[hs:, :])

        def vbody(i, carry):
            inc = jnp.left_shift(jnp.int32(1), 31 - i)
            lo_b = finish_second_half()
            lo_a = lo_ref[:hs, :]
            cand_a = lo_a + inc
            tot_a = jnp.sum(lane_counts(0, cand_a), axis=1, keepdims=True)
            cand_b = lo_b + inc
            cnt_ref[...] = lane_counts(hs, cand_b)
            cand_ref[...] = cand_b
            lo_ref[hs:, :] = lo_b
            lo_ref[:hs, :] = jnp.where(tot_a >= kf, cand_a, lo_a)
            return carry

        lax.fori_loop(0, 32, vbody, 0)
        lo_ref[hs:, :] = finish_second_half()
        thr = lo_ref[:, 0:1]
        key = key_ref[:, :ext]
        n_gt = count(key > thr)
        n_eq = count(key == thr)
        need = kf - n_gt
        n_eq_vis = jnp.sum(jnp.where(key == thr, vis, 0.0), axis=1, keepdims=True)
        mask_ref[:, :ext] = jnp.where(key >= thr, jnp.where(vis > 0.0, 0.0, NEG), NEG)
        surplus = jnp.where(n_eq_vis > 0.0, jnp.where(n_eq > need, 1.0, 0.0), 0.0)

        @pl.when(jnp.max(surplus) > 0.0)
        def _():
            nbits = max(1, (ext - 1).bit_length())
            keyv = key_ref[:, :ext]
            col = lax.broadcasted_iota(jnp.int32, (tq, ext), 1)

            def tbody(i, j):
                cand = j + jnp.left_shift(jnp.int32(1), nbits - 1 - i)
                f = jnp.sum(jnp.where(keyv == thr, jnp.where(col < cand, 1.0, 0.0), 0.0), axis=1, keepdims=True)
                return jnp.where(f < need, cand, j)

            jstar = lax.fori_loop(0, nbits, tbody, jnp.zeros((tq, 1), jnp.int32))
            chosen = jnp.where(keyv > thr, 1.0, jnp.where(keyv == thr, jnp.where(col <= jstar, 1.0, 0.0), 0.0))
            mask_ref[:, :ext] = jnp.where(chosen > 0.0, jnp.where(vis > 0.0, 0.0, NEG), NEG)

        scale2 = A_HDIM ** -0.5 * LOG2E

        def hbody(h, carry):
            r0 = pl.multiple_of(h * tq, tq)
            s = _dot_nt(qs_ref[pl.ds(r0, tq), :], ak_ref[:ext, :]) * scale2 + bias_ref[h, :, :ext] + mask_ref[:, :ext]
            m = jnp.max(s, axis=1, keepdims=True)
            e = jnp.exp2(s - m)
            l = jnp.sum(e, axis=1, keepdims=True)
            os_ref[pl.ds(r0, tq), :] = _dot(e.astype(BF16), av_ref[:ext, :]) / l
            return carry

        lax.fori_loop(0, A_HEADS, hbody, 0, unroll=2)

    _for_each_extent(qi, p_len // tq, tq, p_len, body)
    o = jnp.concatenate([os_ref[h * tq:(h + 1) * tq, :] for h in range(A_HEADS)], axis=1)
    o_ref[...] = _rms(o, g_ref[...]).astype(BF16)


def _attn_a(pf, pb, tabs_idx, bias, g, batch, p_len, tq, k_top, col):
    nq = p_len // tq
    t = batch * p_len
    kern = functools.partial(_attn_a_kernel, tq=tq, p_len=p_len, k_top=k_top)
    return pl.pallas_call(
        kern,
        grid=(batch, nq),
        in_specs=[
            pl.BlockSpec((tq, 1024), lambda b, q: (b * nq + q, col["i_q"] // 1024)),
            pl.BlockSpec((tq, LANES), lambda b, q: (b * nq + q, col["ikw"] // LANES)),
            pl.BlockSpec((p_len, LANES), lambda b, q: (b, col["ikw"] // LANES)),
            pl.BlockSpec((3, tq, LANES), lambda b, q: (0, q, 0)),
            pl.BlockSpec((3, p_len, LANES), lambda b, q: (0, 0, 0)),
            pl.BlockSpec((tq, 1024), lambda b, q: (b * nq + q, col["a_q"] // 1024)),
            pl.BlockSpec((p_len, LANES), lambda b, q: (b, col["a_k"] // LANES)),
            pl.BlockSpec((p_len, LANES), lambda b, q: (b, col["a_v"] // LANES)),
            pl.BlockSpec((A_HEADS, 1, p_len), lambda b, q: (0, 0, 0)),
            pl.BlockSpec((1, GROUP_WIDTH), lambda b, q: (0, 0)),
        ],
        out_specs=pl.BlockSpec((tq, GROUP_WIDTH), lambda b, q: (b * nq + q, 0)),
        out_shape=jax.ShapeDtypeStruct((t, GROUP_WIDTH), BF16),
        scratch_shapes=[pltpu.VMEM((tq, p_len), jnp.int32),
                        pltpu.VMEM((tq, p_len), F32),
                        pltpu.VMEM((A_HEADS * tq, A_HDIM), BF16),
                        pltpu.VMEM((A_HEADS * tq, A_HDIM), F32),
                        pltpu.VMEM((tq, LANES), jnp.int32),
                        pltpu.VMEM((tq // 2, LANES), jnp.int32),
                        pltpu.VMEM((tq // 2, LANES), F32)],
        compiler_params=_cparams(2),
        name="attn_a",
    )(pf, pf, pf, tabs_idx, tabs_idx, pb, pb, pb, bias, g.reshape(1, GROUP_WIDTH))


def _attn_b_kernel(q_ref, kp_ref, kc_ref, vp_ref, vc_ref, bias_ref, sink_ref, g_ref, o_ref):
    n = pl.program_id(1)
    grp = B_HEADS // B_KV_HEADS
    slabs = grp // 2
    rows = slabs * BLK
    first_col = PAD_FRONT - (n - 1) * BLK
    col = lax.broadcasted_iota(jnp.int32, (rows, 2 * BLK), 1)
    pad_mask = jnp.where(col >= first_col, 0.0, NEG)
    lane = lax.broadcasted_iota(jnp.int32, (2 * BLK, LANES), 1)
    low = lane < B_HDIM
    zero = jnp.zeros((2 * BLK, LANES), BF16)
    out_slabs = []
    swap = lambda x: jnp.concatenate([x[:, B_HDIM:], x[:, :B_HDIM]], axis=1)
    kplain = jnp.concatenate([kp_ref[...], kc_ref[...]], axis=0)
    vplain = jnp.concatenate([vp_ref[...], vc_ref[...]], axis=0)
    kswap, vswap = swap(kplain), swap(vplain)
    for gi in range(B_KV_HEADS):
        q = jnp.concatenate([q_ref[:, (gi * slabs + sl) * LANES:(gi * slabs + sl + 1) * LANES] for sl in range(slabs)], axis=0)
        lo_src_k, hi_src_k = (kplain, kswap) if gi == 0 else (kswap, kplain)
        lo_src_v, hi_src_v = (vplain, vswap) if gi == 0 else (vswap, vplain)
        acc = None
        for par in range(2):
            keep = low if par == 0 else jnp.logical_not(low)
            k = jnp.where(keep, lo_src_k if par == 0 else hi_src_k, zero)
            v = jnp.where(keep, lo_src_v if par == 0 else hi_src_v, zero)
            s = _dot_nt(q, k) + bias_ref[gi, par] + pad_mask
            sink = sink_ref[gi, par]
            m = jnp.maximum(jnp.max(s, axis=1, keepdims=True), sink)
            e = jnp.exp(s - m)
            l = jnp.sum(e, axis=1, keepdims=True) + jnp.exp(sink - m)
            part = _dot(e.astype(BF16), v) * (1.0 / l)
            acc = part if acc is None else acc + part
        out_slabs.extend(acc[sl * BLK:(sl + 1) * BLK, :] for sl in range(slabs))
    o = jnp.concatenate(out_slabs, axis=1)
    o_ref[...] = _rms(o, g_ref[...]).astype(BF16)


def _attn_b(pb, bias, sink_rows, g, batch, p_len, col):
    nb = p_len // BLK
    t = batch * p_len
    grp = B_HEADS // B_KV_HEADS
    rows = grp // 2 * BLK
    prev = lambda b, n: b * nb + jnp.maximum(n - 1, 0)
    cur = lambda b, n: b * nb + n
    kv_specs = []
    for name in ("b_k", "b_v"):
        c = col[name] // LANES
        kv_specs.append(pl.BlockSpec((BLK, LANES), lambda b, n, c=c: (prev(b, n), c)))
        kv_specs.append(pl.BlockSpec((BLK, LANES), lambda b, n, c=c: (cur(b, n), c)))
    return pl.pallas_call(
        _attn_b_kernel,
        grid=(batch, nb),
        in_specs=[pl.BlockSpec((BLK, 1024), lambda b, n: (cur(b, n), col["b_q"] // 1024))] + kv_specs + [
            pl.BlockSpec((B_KV_HEADS, 2, rows, 2 * BLK), lambda b, n: (0, 0, 0, 0)),
            pl.BlockSpec((B_KV_HEADS, 2, rows, 1), lambda b, n: (0, 0, 0, 0)),
            pl.BlockSpec((1, GROUP_WIDTH), lambda b, n: (0, 0)),
        ],
        out_specs=pl.BlockSpec((BLK, GROUP_WIDTH), lambda b, n: (cur(b, n), 0)),
        out_shape=jax.ShapeDtypeStruct((t, GROUP_WIDTH), BF16),
        compiler_params=_cparams(2),
        name="attn_b",
    )(pb, *([pb] * 4), bias, sink_rows, g.reshape(1, GROUP_WIDTH))


def _c_prep_kernel(cq_ref, ckv_ref, kr_ref, tab_ref, gq_ref, gkv_ref, wq_ref, wkv_ref, q_ref, k_ref, v_ref):
    half = C_ROPE // 2
    tc, ts1, ts2 = tab_ref[0], tab_ref[1], tab_ref[2]
    slot = C_NOPE + LANES
    xq = _rms(cq_ref[...], gq_ref[...]).astype(BF16)
    q = _dot(xq, wq_ref[...])
    xkv = _rms(ckv_ref[...], gkv_ref[...]).astype(BF16)
    kv = _dot(xkv, wkv_ref[...])
    kr = _rope128(kr_ref[...], tc, ts1, ts2, half).astype(BF16)
    for h in range(C_HEADS):
        lo = h * slot
        q_ref[h, :, :C_NOPE] = q[:, lo:lo + C_NOPE].astype(BF16)
        q_ref[h, :, C_NOPE:] = _rope128(q[:, lo + C_NOPE:lo + slot], tc, ts1, ts2, half).astype(BF16)
        k_ref[h, :, :C_NOPE] = kv[:, h * C_NOPE:(h + 1) * C_NOPE].astype(BF16)
        k_ref[h, :, C_NOPE:] = kr
        v_ref[h] = kv[:, (C_HEADS + h) * C_NOPE:(C_HEADS + h + 1) * C_NOPE].astype(BF16)


def _c_prep(pf, tabs, g_cq, g_ckv, w_uq_p, w_ukv_p, p_len, tm, col):
    t = pf.shape[0]
    tiles = p_len // tm
    slot = C_NOPE + LANES
    return pl.pallas_call(
        _c_prep_kernel,
        grid=(t // tm,),
        in_specs=[
            pl.BlockSpec((tm, C_Q_RANK), lambda i: (i, col["c_cq"] // C_Q_RANK)),
            pl.BlockSpec((tm, C_KV_RANK), lambda i: (i, col["c_ckv"] // C_KV_RANK)),
            pl.BlockSpec((tm, LANES), lambda i: (i, col["c_kr"] // LANES)),
            pl.BlockSpec((3, tm, LANES), lambda i: (0, i % tiles, 0)),
            pl.BlockSpec((1, C_Q_RANK), lambda i: (0, 0)),
            pl.BlockSpec((1, C_KV_RANK), lambda i: (0, 0)),
            pl.BlockSpec((C_Q_RANK, C_HEADS * slot), lambda i: (0, 0)),
            pl.BlockSpec((C_KV_RANK, C_HEADS * (C_NOPE + C_VDIM)), lambda i: (0, 0)),
        ],
        out_specs=[pl.BlockSpec((C_HEADS, tm, slot), lambda i: (0, i, 0)),
                   pl.BlockSpec((C_HEADS, tm, slot), lambda i: (0, i, 0)),
                   pl.BlockSpec((C_HEADS, tm, C_VDIM), lambda i: (0, i, 0))],
        out_shape=[jax.ShapeDtypeStruct((C_HEADS, t, slot), BF16),
                   jax.ShapeDtypeStruct((C_HEADS, t, slot), BF16),
                   jax.ShapeDtypeStruct((C_HEADS, t, C_VDIM), BF16)],
        compiler_params=_cparams(1),
        name="c_prep",
    )(pf, pf, pf, tabs, g_cq.reshape(1, -1), g_ckv.reshape(1, -1), w_uq_p, w_ukv_p)


def _causal_mask(q0, tq, ext):
    qpos = q0 + lax.broadcasted_iota(jnp.int32, (tq, ext), 0)
    kpos = lax.broadcasted_iota(jnp.int32, (tq, ext), 1)
    return jnp.where(kpos <= qpos, jnp.where(kpos >= PAD_FRONT, 0.0, NEG), NEG)


def _attn_c_kernel(q_ref, k_ref, v_ref, g_ref, o_ref, os_ref, *, tq, p_len):
    qi = pl.program_id(1)
    scale2 = (C_NOPE + C_ROPE) ** -0.5 * LOG2E

    def body(ext):
        mask = _causal_mask(qi * tq, tq, ext)

        def hbody(h, carry):
            s = _dot_nt(q_ref[h], k_ref[h, :ext, :]) * scale2 + mask
            m = jnp.max(s, axis=1, keepdims=True)
            e = jnp.exp2(s - m)
            l = jnp.sum(e, axis=1, keepdims=True)
            os_ref[h] = _dot(e.astype(BF16), v_ref[h, :ext, :]) / l
            return carry

        lax.fori_loop(0, C_HEADS, hbody, 0, unroll=2)

    nq = p_len // tq
    _for_each_extent(qi, nq, tq, p_len, body, nq)
    o = jnp.concatenate([os_ref[h] for h in range(C_HEADS)], axis=1)
    o_ref[...] = _rms(o, g_ref[...]).astype(BF16)


def _attn_c(qc, kc, vc, g, batch, p_len, tq):
    nq = p_len // tq
    t = batch * p_len
    slot = C_NOPE + LANES
    kern = functools.partial(_attn_c_kernel, tq=tq, p_len=p_len)
    return pl.pallas_call(
        kern,
        grid=(batch, nq),
        in_specs=[pl.BlockSpec((C_HEADS, tq, slot), lambda b, q: (0, b * nq + q, 0)),
                  pl.BlockSpec((C_HEADS, p_len, slot), lambda b, q: (0, b, 0)),
                  pl.BlockSpec((C_HEADS, p_len, C_VDIM), lambda b, q: (0, b, 0)),
                  pl.BlockSpec((1, GROUP_WIDTH), lambda b, q: (0, 0))],
        out_specs=pl.BlockSpec((tq, GROUP_WIDTH), lambda b, q: (b * nq + q, 0)),
        out_shape=jax.ShapeDtypeStruct((t, GROUP_WIDTH), BF16),
        scratch_shapes=[pltpu.VMEM((C_HEADS, tq, C_VDIM), F32)],
        compiler_params=_cparams(2),
        name="attn_c",
    )(qc, kc, vc, g.reshape(1, GROUP_WIDTH))


def _attn_d_kernel(q_ref, k_ref, v_ref, bias_ref, lam_ref, g_ref, o_ref, os_ref, *, tq, p_len, lam_init):
    qi = pl.program_id(1)
    lp = lam_ref[...]
    lam = (jnp.exp(jnp.sum(lp[0:1] * lp[1:2], axis=1, keepdims=True))
           - jnp.exp(jnp.sum(lp[2:3] * lp[3:4], axis=1, keepdims=True)) + lam_init)

    def body(ext):
        mask = _causal_mask(qi * tq, tq, ext)

        def hbody(h, carry):
            bm = bias_ref[h, :, :ext] + mask
            qh = q_ref[h]
            kh = k_ref[h, :ext, :]
            es, ls = [], []
            for c in range(2):
                lo = c * D_HDIM
                s = _dot_nt(qh[:, lo:lo + D_HDIM], kh[:, lo:lo + D_HDIM]) + bm
                m = jnp.max(s, axis=1, keepdims=True)
                e = jnp.exp(s - m)
                es.append(e)
                ls.append(jnp.sum(e, axis=1, keepdims=True))
            a = es[0] * (1.0 / ls[0]) - es[1] * (lam / ls[1])
            os_ref[h] = _dot(a.astype(BF16), v_ref[h, :ext, :])
            return carry

        lax.fori_loop(0, D_HEADS, hbody, 0)

    nq = p_len // tq
    _for_each_extent(qi, nq, tq, p_len, body, nq)
    g = g_ref[...]
    outs = [_rms(os_ref[h], g) * (1.0 - lam_init) for h in range(D_HEADS)]
    o_ref[...] = jnp.concatenate(outs, axis=1).astype(BF16)


def _attn_d(pd, bias, lam_p, g, batch, p_len, tq, lam_init):
    nq = p_len // tq
    t = batch * p_len
    kern = functools.partial(_attn_d_kernel, tq=tq, p_len=p_len, lam_init=lam_init)
    return pl.pallas_call(
        kern,
        grid=(batch, nq),
        in_specs=[pl.BlockSpec((D_HEADS, tq, LANES), lambda b, q: (0, b * nq + q, 0)),
                  pl.BlockSpec((D_HEADS, p_len, LANES), lambda b, q: (1, b, 0)),
                  pl.BlockSpec((D_HEADS, p_len, LANES), lambda b, q: (2, b, 0)),
                  pl.BlockSpec((D_HEADS, 1, p_len), lambda b, q: (0, 0, 0)),
                  pl.BlockSpec((4, D_HDIM), lambda b, q: (0, 0)),
                  pl.BlockSpec((1, 2 * D_HDIM), lambda b, q: (0, 0))],
        out_specs=pl.BlockSpec((tq, GROUP_WIDTH), lambda b, q: (b * nq + q, 0)),
        out_shape=jax.ShapeDtypeStruct((t, GROUP_WIDTH), BF16),
        scratch_shapes=[pltpu.VMEM((D_HEADS, tq, 2 * D_HDIM), F32)],
        compiler_params=_cparams(2),
        name="attn_d",
    )(pd, pd, pd, bias, lam_p, g.reshape(1, 2 * D_HDIM))


def _ffn_up_kernel(x_ref, wg_ref, wu_ref, cwg_ref, cwu_ref, cbg_ref, cbu_ref, o_ref,
                   wgb_ref, wub_ref, ha_ref, hb_ref, carry_ref, *, tm, n_row_tiles, tiles_per_seq):
    s = pl.program_id(0)
    d = x_ref.shape[1]
    kc = min(d, MXU_DEPTH)
    nk = d // kc
    rc = next(r for r in (64, 32, 16, 8) if tm % r == 0)
    nr = tm // rc

    @pl.when(s % n_row_tiles == 0)
    def _():
        wgb_ref[...] = wg_ref[...].astype(BF16)
        wub_ref[...] = wu_ref[...].astype(BF16)

    @pl.when(s == 0)
    def _():
        hb_ref[...] = jnp.zeros_like(hb_ref)

    @pl.when((s + tiles_per_seq - 1) % tiles_per_seq == 0)
    def _():
        carry_ref[...] = jnp.zeros_like(carry_ref)

    def run(rd_ref, wr_ref):
        cws = (cwg_ref[...], cwu_ref[...])
        cbs = (cbg_ref[...], cbu_ref[...])
        for step in range(nr):
            for kk in range((step * nk) // nr, ((step + 1) * nk) // nr):
                for b, wb_ref in enumerate((wgb_ref, wub_ref)):
                    part = _dot(x_ref[:, kk * kc:(kk + 1) * kc], wb_ref[kk * kc:(kk + 1) * kc, :])
                    if kk == 0:
                        wr_ref[b] = part
                    else:
                        wr_ref[b] = wr_ref[b] + part
            r0 = step * rc
            ys = []
            for b in range(2):
                top = carry_ref[b] if step == 0 else rd_ref[b, r0 - 8:r0, :]
                cur = rd_ref[b, r0:r0 + rc, :]
                full = jnp.concatenate([top, cur], axis=0)
                cw = cws[b]
                y = cw[2:3] * cur + cw[1:2] * pltpu.roll(full, 1, 0)[8:] + cw[0:1] * pltpu.roll(full, 2, 0)[8:]
                ys.append(y + cbs[b])
            o_ref[r0:r0 + rc, :] = (ys[0] * jax.nn.sigmoid(ys[0]) * ys[1]).astype(BF16)
        for b in range(2):
            carry_ref[b] = rd_ref[b, tm - 8:, :]

    @pl.when(s % 2 == 0)
    def _():
        run(hb_ref, ha_ref)

    @pl.when(s % 2 == 1)
    def _():
        run(ha_ref, hb_ref)


def _ffn_up(hb, w_up, conv_w, conv_b, layer, p_len, tm, tn):
    t, d = hb.shape
    f = w_up.shape[2] // 2
    assert f % tn == 0 and tm % 8 == 0 and p_len % tm == 0
    nj, nt = f // tn, t // tm
    n_tiles = nj * nt
    kern = functools.partial(_ffn_up_kernel, tm=tm, n_row_tiles=nt, tiles_per_seq=p_len // tm)
    cb = conv_b.reshape(conv_b.shape[0], 1, 2 * f)
    cur = lambda s: jnp.minimum(s, n_tiles - 1)
    prv = lambda s: jnp.maximum(s - 1, 0)
    return pl.pallas_call(
        kern,
        grid=(n_tiles + 1,),
        in_specs=[pl.BlockSpec((tm, d), lambda s: (cur(s) % nt, 0)),
                  pl.BlockSpec((None, d, tn), lambda s: (layer, 0, cur(s) // nt)),
                  pl.BlockSpec((None, d, tn), lambda s: (layer, 0, nj + cur(s) // nt)),
                  pl.BlockSpec((None, CONV_W, tn), lambda s: (layer, 0, prv(s) // nt)),
                  pl.BlockSpec((None, CONV_W, tn), lambda s: (layer, 0, nj + prv(s) // nt)),
                  pl.BlockSpec((None, 1, tn), lambda s: (layer, 0, prv(s) // nt)),
                  pl.BlockSpec((None, 1, tn), lambda s: (layer, 0, nj + prv(s) // nt))],
        out_specs=pl.BlockSpec((tm, tn), lambda s: (prv(s) % nt, prv(s) // nt)),
        out_shape=jax.ShapeDtypeStruct((t, f), BF16),
        scratch_shapes=[pltpu.VMEM((d, tn), BF16), pltpu.VMEM((d, tn), BF16),
                        pltpu.VMEM((2, tm, tn), F32), pltpu.VMEM((2, tm, tn), F32), pltpu.VMEM((2, 8, tn), F32)],
        compiler_params=_cparams(1),
        name="ffn_up",
    )(hb, w_up, w_up, conv_w, conv_w, cb, cb)


def _offsets(names, sizes):
    out, o = {}, 0
    for n, s in zip(names, sizes):
        out[n] = o
        o += s
    return out, o


_BF_NAMES = ("a_q", "b_q", "a_k", "a_v", "b_k", "b_v")
_BF_SIZES = (1024, 1024, 128, 128, 128, 128)
_D_NAMES = ("d_q", "d_k", "d_v")
_F32_NAMES = ("i_q", "c_cq", "c_ckv", "ikw", "c_kr")
_F32_SIZES = (1024, 1024, 512, 128, 128)
COL_BF, N_BF = _offsets(_BF_NAMES, _BF_SIZES)
COL_F32, N_F32 = _offsets(_F32_NAMES, _F32_SIZES)


def _prep_w_in(w):
    d = w.shape[0]
    seg = dict(zip(("a_q", "a_k", "a_v", "i_q", "i_k", "i_w", "b_q", "b_k", "b_v", "c_cq", "c_ckv", "c_kr",
                    "d_q", "d_k", "d_v"), jnp.split(w, np.cumsum(IN_SIZES)[:-1].tolist(), axis=1)))
    seg["b_q"] = seg["b_q"] * (B_HDIM ** -0.5)
    seg["d_q"] = seg["d_q"] * (D_HDIM ** -0.5)
    wb = jnp.concatenate([seg[n] for n in _BF_NAMES], axis=1).astype(BF16)
    wd = jnp.concatenate([seg[n] for n in _D_NAMES], axis=1).astype(BF16)
    zeros = lambda n: jnp.zeros((d, n), w.dtype)
    ikw = jnp.concatenate([seg["i_k"], seg["i_w"], zeros(LANES - IDX_HDIM - IDX_HEADS)], axis=1)
    ckr = jnp.concatenate([seg["c_kr"], zeros(LANES - C_ROPE)], axis=1)
    wf = jnp.concatenate([seg["i_q"], seg["c_cq"], seg["c_ckv"], ikw, ckr], axis=1).astype(BF16)
    return wb, wd, wf


def _prep_w_uq(w):
    r = w.shape[0]
    w3 = w.reshape(r, C_HEADS, C_NOPE + C_ROPE)
    return jnp.pad(w3, ((0, 0), (0, 0), (0, LANES - C_ROPE))).reshape(r, C_HEADS * (C_NOPE + LANES)).astype(BF16)


def _prep_w_ukv(w):
    r = w.shape[0]
    w3 = w.reshape(r, C_HEADS, C_NOPE + C_VDIM)
    return jnp.concatenate([w3[:, :, :C_NOPE].reshape(r, -1), w3[:, :, C_NOPE:].reshape(r, -1)], axis=1).astype(BF16)


def _alibi(n):
    return 2.0 ** (-8.0 * np.arange(1, n + 1, dtype=np.float64) / n)


def _swa_bias():
    grp = B_HEADS // B_KV_HEADS
    r = np.arange(BLK)[:, None]
    c = np.arange(2 * BLK)[None, :]
    diff = (r + BLK - c).astype(np.float64)
    ok = (diff >= 0) & (diff < WINDOW)
    slopes = _alibi(B_HEADS).reshape(B_KV_HEADS, grp // 2, 2).transpose(0, 2, 1)
    bias = np.where(ok[None, None, None], -slopes[..., None, None] * diff[None, None, None], NEG)
    return jnp.asarray(bias.reshape(B_KV_HEADS, 2, grp // 2 * BLK, 2 * BLK), F32)


def _swa_sinks(sinks):
    grp = B_HEADS // B_KV_HEADS
    sk = sinks.astype(F32).reshape(B_KV_HEADS, grp // 2, 2).transpose(0, 2, 1)
    return jnp.repeat(sk, BLK, axis=2)[..., None]


def kernel(x, meta_tokens, ln_in_g, ln_in_b, w_in, g_cq, g_ckv, w_uq, w_ukv, sinks, lam_q1, lam_k1, lam_q2, lam_k2,
           g_diff, g_grp, w_out, ln1_g, ln1_b, w_up, conv_w, conv_b, w_down, ln2_g, ln2_b):
    batch, s_len, d = x.shape
    depth = w_in.shape[0]
    p_len = s_len + BLK
    t = batch * p_len
    k_top = min(TOPK_MAX, s_len // 4)
    alpha = (2 * depth) ** 0.25
    tm = _row_tile(p_len, 544)
    tm_ln = _row_tile(p_len, 272)
    tm_ffn = _row_tile(p_len, 1088)
    tq = BLK
    grp = B_HEADS // B_KV_HEADS

    h, hb = _ln_in(x, meta_tokens, ln_in_g, ln_in_b)

    pos = jnp.arange(p_len, dtype=jnp.int32) - PAD_FRONT
    tabs_idx = _rope_tables(pos, IDX_ROPE, IDX_HDIM)
    tabs_c = _rope_tables(pos, C_ROPE, LANES)
    kidx = np.arange(p_len, dtype=np.float64)
    bias_a = jnp.asarray(_alibi(A_HEADS)[:, None, None] * kidx[None, None, :] * LOG2E, F32)
    bias_d = jnp.asarray(_alibi(D_HEADS)[:, None, None] * kidx[None, None, :], F32)
    bias_b = _swa_bias()

    for l in range(depth):
        wb, wd, wf = _prep_w_in(w_in[l])
        pb = _mm(hb, wb, tm_ffn, 2 * MXU_DEPTH, BF16, "proj_bf16")
        pd = _mm_slab(hb, wd, tm_ffn, 2 * MXU_DEPTH, BF16, "proj_d")
        pf = _mm(hb, wf, tm, N_F32 // 2, F32, "proj_f32")

        o_a = _attn_a(pf, pb, tabs_idx, bias_a, g_grp[l, 0], batch, p_len, tq, k_top, {**COL_F32, **COL_BF})
        o_b = _attn_b(pb, bias_b, _swa_sinks(sinks[l]), g_grp[l, 1], batch, p_len, COL_BF)
        qc, kc, vc = _c_prep(pf, tabs_c, g_cq[l], g_ckv[l], _prep_w_uq(w_uq[l]), _prep_w_ukv(w_ukv[l]),
                               p_len, tm_ln, COL_F32)
        o_c = _attn_c(qc, kc, vc, g_grp[l, 2], batch, p_len, tm_ln)
        lam_init = 0.8 - 0.6 * math.exp(-0.3 * l)
        lam_p = jnp.stack([lam_q1[l], lam_k1[l], lam_q2[l], lam_k2[l]]).astype(F32)
        o_d = _attn_d(pd, bias_d, lam_p, g_diff[l], batch, p_len, tm_ln, lam_init)

        mix = _mm_groups([o_a, o_b, o_c, o_d], _cast_layer(w_out, l, BF16), tm, min(1024, d), F32, "out_proj")
        h, hb_ffn = _res_ln(h, mix, ln1_g[l], ln1_b[l], alpha, tm_ln, p_len, True)

        act = _ffn_up(hb_ffn, w_up, conv_w, conv_b, l, p_len, tm_ffn, FF_TILE)
        ffn = _mm(act, _cast_layer(w_down, l, BF16), tm, min(512, d), F32, "ffn_down")
        if l + 1 < depth:
            h, hb = _res_ln(h, ffn, ln2_g[l], ln2_b[l], alpha, tm_ln, p_len, False)
        else:
            out = _res_ln_final(h, ffn, ln2_g[l], ln2_b[l], alpha, batch, p_len)

    return out.reshape(batch, s_len, d)
```

```python
import functools
import math

import jax
import jax.numpy as jnp
import numpy as np
from jax import lax
from jax.experimental import pallas as pl
from jax.experimental.pallas import tpu as pltpu

N_META = 16
BLK = 128
PAD_FRONT = BLK - N_META
A_HEADS, A_HDIM = 8, 128
IDX_HEADS, IDX_HDIM, IDX_ROPE = 16, 64, 32
TOPK_MAX = 256
B_HEADS, B_KV_HEADS, B_HDIM = 16, 2, 64
WINDOW = 128
C_HEADS, C_Q_RANK, C_KV_RANK, C_NOPE, C_ROPE, C_VDIM = 8, 1024, 512, 128, 64, 128
D_HEADS, D_HDIM = 8, 64
GROUP_WIDTH = 1024
CONV_W = 3
ROPE_BASE = 10000.0
NEG = -1e30
IN_SIZES = (1024, 128, 128, 1024, 64, 16, 1024, 128, 128, 1024, 512, 64, 1024, 1024, 1024)

LANES = 128
BF16_SUBLANES = 16
MXU_DEPTH = 256
VMEM_LIMIT = 56 * 1024 * 1024
FF_TILE = 256
N_EXTENT_VARIANTS = 6
LOG2E = 1.4426950408889634

F32 = jnp.float32
BF16 = jnp.bfloat16


def _cparams(n_axes):
    return pltpu.CompilerParams(dimension_semantics=("arbitrary",) * n_axes, vmem_limit_bytes=VMEM_LIMIT)


def _row_tile(p_len, target):
    best = BF16_SUBLANES
    for t in range(BF16_SUBLANES, target + 1, BF16_SUBLANES):
        if p_len % t == 0:
            best = t
    return best


def _dot(a, b):
    return jnp.dot(a, b, preferred_element_type=F32)


def _dot_nt(a, b):
    return lax.dot_general(a, b, (((1,), (1,)), ((), ())), preferred_element_type=F32)


def _ln_rows(x, g, b):
    mu = jnp.mean(x, axis=-1, keepdims=True)
    xc = x - mu
    var = jnp.mean(xc * xc, axis=-1, keepdims=True)
    return xc * lax.rsqrt(var + 1e-5) * g + b


def _ln_in_kernel(x_ref, meta_ref, g_ref, b_ref, h_ref, hb_ref):
    def emit(rows):
        y = _ln_rows(rows, g_ref[...], b_ref[...])
        h_ref[...] = y
        hb_ref[...] = y.astype(BF16)

    @pl.when(pl.program_id(1) == 0)
    def _():
        emit(jnp.concatenate([jnp.zeros((PAD_FRONT, meta_ref.shape[1]), F32), meta_ref[...]], axis=0))

    @pl.when(pl.program_id(1) > 0)
    def _():
        emit(x_ref[...])


def _ln_in(x, meta, g, b):
    batch, s_len, d = x.shape
    nb = s_len // BLK + 1
    t = batch * nb * BLK
    return pl.pallas_call(
        _ln_in_kernel,
        grid=(batch, nb),
        in_specs=[pl.BlockSpec((None, BLK, d), lambda bi, n: (bi, jnp.maximum(n - 1, 0), 0)),
                  pl.BlockSpec((N_META, d), lambda bi, n: (0, 0)),
                  pl.BlockSpec((1, d), lambda bi, n: (0, 0)),
                  pl.BlockSpec((1, d), lambda bi, n: (0, 0))],
        out_specs=[pl.BlockSpec((BLK, d), lambda bi, n: (bi * nb + n, 0)),
                   pl.BlockSpec((BLK, d), lambda bi, n: (bi * nb + n, 0))],
        out_shape=[jax.ShapeDtypeStruct((t, d), F32), jax.ShapeDtypeStruct((t, d), BF16)],
        compiler_params=_cparams(2),
        name="ln_in",
    )(x, meta.astype(F32), g.reshape(1, d), b.reshape(1, d))


def _res_ln_kernel(h_ref, y_ref, g_ref, b_ref, o_ref, ob_ref, *, alpha, tm, tiles_per_seq, zero_pad):
    out = _ln_rows(alpha * h_ref[...] + y_ref[...], g_ref[...], b_ref[...])
    o_ref[...] = out
    if zero_pad:
        p0 = (pl.program_id(0) % tiles_per_seq) * tm
        pos = p0 + lax.broadcasted_iota(jnp.int32, out.shape, 0)
        out = jnp.where(pos >= PAD_FRONT, out, 0.0)
    ob_ref[...] = out.astype(BF16)


def _res_ln(h, y, g, b, alpha, tm, p_len, zero_pad):
    t, d = h.shape
    kern = functools.partial(_res_ln_kernel, alpha=alpha, tm=tm, tiles_per_seq=p_len // tm, zero_pad=zero_pad)
    return pl.pallas_call(
        kern,
        grid=(t // tm,),
        in_specs=[pl.BlockSpec((tm, d), lambda i: (i, 0)),
                  pl.BlockSpec((tm, d), lambda i: (i, 0)),
                  pl.BlockSpec((1, d), lambda i: (0, 0)),
                  pl.BlockSpec((1, d), lambda i: (0, 0))],
        out_specs=[pl.BlockSpec((tm, d), lambda i: (i, 0)),
                   pl.BlockSpec((tm, d), lambda i: (i, 0))],
        out_shape=[jax.ShapeDtypeStruct((t, d), F32), jax.ShapeDtypeStruct((t, d), BF16)],
        compiler_params=_cparams(1),
        name="res_ln",
    )(h, y, g.reshape(1, d), b.reshape(1, d))


def _res_ln_final_kernel(h_ref, y_ref, g_ref, b_ref, o_ref, *, alpha):
    o_ref[...] = _ln_rows(alpha * h_ref[...] + y_ref[...], g_ref[...], b_ref[...])


def _res_ln_final(h, y, g, b, alpha, batch, p_len):
    t, d = h.shape
    nb = p_len // BLK
    out_idx = lambda i: ((i // nb) * (nb - 1) + jnp.maximum(i % nb - 1, 0), 0)
    return pl.pallas_call(
        functools.partial(_res_ln_final_kernel, alpha=alpha),
        grid=(t // BLK,),
        in_specs=[pl.BlockSpec((BLK, d), lambda i: (i, 0)),
                  pl.BlockSpec((BLK, d), lambda i: (i, 0)),
                  pl.BlockSpec((1, d), lambda i: (0, 0)),
                  pl.BlockSpec((1, d), lambda i: (0, 0))],
        out_specs=pl.BlockSpec((BLK, d), out_idx),
        out_shape=jax.ShapeDtypeStruct((batch * (p_len - BLK), d), F32),
        compiler_params=_cparams(1),
        name="res_ln_final",
    )(h, y, g.reshape(1, d), b.reshape(1, d))


def _mm_kernel(a_ref, w_ref, o_ref):
    o_ref[...] = _dot(a_ref[...], w_ref[...]).astype(o_ref.dtype)


def _mm(a, w, tm, tn, out_dtype, name):
    t, k = a.shape
    n = w.shape[1]
    return pl.pallas_call(
        _mm_kernel,
        grid=(n // tn, t // tm),
        in_specs=[pl.BlockSpec((tm, k), lambda j, i: (i, 0)),
                  pl.BlockSpec((k, tn), lambda j, i: (0, j))],
        out_specs=pl.BlockSpec((tm, tn), lambda j, i: (i, j)),
        out_shape=jax.ShapeDtypeStruct((t, n), out_dtype),
        compiler_params=_cparams(2),
        name=name,
    )(a, w)


def _cast_kernel(w_ref, o_ref):
    o_ref[...] = w_ref[...].astype(o_ref.dtype)


def _cast_layer(w, layer, dtype):
    _, r, c = w.shape
    rb = _row_tile(r, 256)
    return pl.pallas_call(
        _cast_kernel,
        grid=(r // rb,),
        in_specs=[pl.BlockSpec((None, rb, c), lambda i: (layer, i, 0))],
        out_specs=pl.BlockSpec((rb, c), lambda i: (i, 0)),
        out_shape=jax.ShapeDtypeStruct((r, c), dtype),
        compiler_params=_cparams(1),
        name="cast_w",
    )(w)


def _mm_groups_kernel(*refs):
    *a_refs, w_ref, o_ref = refs
    k0 = 0
    acc = None
    for a_ref in a_refs:
        kw = a_ref.shape[1]
        part = _dot(a_ref[...], w_ref[k0:k0 + kw, :])
        acc = part if acc is None else acc + part
        k0 += kw
    o_ref[...] = acc.astype(o_ref.dtype)


def _mm_groups(a_list, w, tm, tn, out_dtype, name):
    t = a_list[0].shape[0]
    k, n = w.shape
    assert sum(a.shape[1] for a in a_list) == k
    return pl.pallas_call(
        _mm_groups_kernel,
        grid=(n // tn, t // tm),
        in_specs=[pl.BlockSpec((tm, a.shape[1]), lambda j, i: (i, 0)) for a in a_list]
        + [pl.BlockSpec((k, tn), lambda j, i: (0, j))],
        out_specs=pl.BlockSpec((tm, tn), lambda j, i: (i, j)),
        out_shape=jax.ShapeDtypeStruct((t, n), out_dtype),
        compiler_params=_cparams(2),
        name=name,
    )(*a_list, w)


def _mm_slab_kernel(a_ref, w_ref, o_ref):
    res = _dot(a_ref[...], w_ref[...])
    for c in range(o_ref.shape[0]):
        o_ref[c] = res[:, c * LANES:(c + 1) * LANES].astype(o_ref.dtype)


def _mm_slab(a, w, tm, tn, out_dtype, name):
    t, k = a.shape
    n = w.shape[1]
    return pl.pallas_call(
        _mm_slab_kernel,
        grid=(n // tn, t // tm),
        in_specs=[pl.BlockSpec((tm, k), lambda j, i: (i, 0)),
                  pl.BlockSpec((k, tn), lambda j, i: (0, j))],
        out_specs=pl.BlockSpec((tn // LANES, tm, LANES), lambda j, i: (j, i, 0)),
        out_shape=jax.ShapeDtypeStruct((n // LANES, t, LANES), out_dtype),
        compiler_params=_cparams(2),
        name=name,
    )(a, w)


def _rope128(x, c, s1, s2, half):
    return x * c + pltpu.roll(x, LANES - half, 1) * s1 + pltpu.roll(x, half, 1) * s2


def _rope_tables(pos, rot_dim, period):
    half = rot_dim // 2
    inv = ROPE_BASE ** (-jnp.arange(half, dtype=F32) / half)
    ang = pos.astype(F32)[:, None] * inv[None]
    cos, sin = jnp.cos(ang), jnp.sin(ang)
    n = pos.shape[0]
    lane = np.arange(LANES) % period
    li = lane % half
    cos_l, sin_l = cos[:, li], sin[:, li]
    first = jnp.asarray(lane < half)[None]
    second = jnp.asarray((lane >= half) & (lane < rot_dim))[None]
    c = jnp.where(first | second, cos_l, 1.0)
    s1 = jnp.where(first, -sin_l, 0.0)
    s2 = jnp.where(second, sin_l, 0.0)
    return jnp.stack([c, s1, s2]).astype(F32).reshape(3, n, LANES)


def _rms(x, g, eps=1e-6):
    return x * lax.rsqrt(jnp.mean(x * x, axis=-1, keepdims=True) + eps) * g


def _key_extents(nq, tq, p_len, nvar):
    step = -(-nq // nvar)
    out = []
    for lo in range(0, nq, step):
        hi = min(lo + step, nq)
        out.append((lo, hi, min(p_len, -(-(hi * tq) // BLK) * BLK)))
    return out


def _for_each_extent(qi, nq, tq, p_len, body, nvar=N_EXTENT_VARIANTS):
    for lo, hi, ext in _key_extents(nq, tq, p_len, nvar):
        @pl.when(jnp.logical_and(qi >= lo, qi < hi))
        def _():
            body(ext)


def _attn_a_kernel(iq_ref, wq_ref, ikw_ref, tq_ref, tk_ref, aq_ref, ak_ref, av_ref, bias_ref, g_ref,
                   o_ref, key_ref, mask_ref, qs_ref, os_ref, lo_ref, cand_ref, cnt_ref, *, tq, p_len, k_top):
    qi = pl.program_id(1)
    q0 = qi * tq
    half = IDX_ROPE // 2
    for h in range(A_HEADS):
        qs_ref[h * tq:(h + 1) * tq, :] = aq_ref[:, h * A_HDIM:(h + 1) * A_HDIM]

    def body(ext):
        ik = _rope128(ikw_ref[:ext, :], tk_ref[0, :ext, :], tk_ref[1, :ext, :], tk_ref[2, :ext, :],
                      half)[:, :IDX_HDIM].astype(BF16)
        w = wq_ref[...][:, IDX_HDIM:IDX_HDIM + IDX_HEADS] * ((IDX_HEADS * IDX_HDIM) ** -0.5)
        tc, ts1, ts2 = tq_ref[0], tq_ref[1], tq_ref[2]
        score = jnp.zeros((tq, ext), F32)
        for c in range(IDX_HEADS * IDX_HDIM // LANES):
            chunk = _rope128(iq_ref[:, c * LANES:(c + 1) * LANES], tc, ts1, ts2, half).astype(BF16)
            for j in range(LANES // IDX_HDIM):
                h = c * (LANES // IDX_HDIM) + j
                rel = jnp.maximum(_dot_nt(chunk[:, j * IDX_HDIM:(j + 1) * IDX_HDIM], ik), 0.0)
                score = score + w[:, h:h + 1] * rel
        qpos = q0 + lax.broadcasted_iota(jnp.int32, (tq, ext), 0)
        kpos = lax.broadcasted_iota(jnp.int32, (tq, ext), 1)
        vis = jnp.where(kpos <= qpos, jnp.where(kpos >= PAD_FRONT, 1.0, 0.0), 0.0)
        score = jnp.where(vis > 0.0, score, NEG)
        bits = lax.bitcast_convert_type(score, jnp.int32)
        key_ref[:, :ext] = jnp.where(bits < 0, bits ^ jnp.int32(0x7FFFFFFF), bits)
        kf = jnp.float32(k_top)

        def count(pred):
            return jnp.sum(jnp.where(pred, 1.0, 0.0), axis=1, keepdims=True)

        hs = tq // 2

        def lane_counts(r0, cand):
            terms = [jnp.where(key_ref[r0:r0 + hs, c * LANES:(c + 1) * LANES] >= cand, 1.0, 0.0)
                     for c in range(ext // LANES)]
            while len(terms) > 1:
                terms = [terms[i] + terms[i + 1] if i + 1 < len(terms) else terms[i] for i in range(0, len(terms), 2)]
            return terms[0]

        lo_ref[...] = jnp.full((tq, LANES), -2 ** 31, jnp.int32)
        cand_ref[...] = jnp.full((hs, LANES), -2 ** 31, jnp.int32)
        cnt_ref[...] = jnp.zeros((hs, LANES), F32)

        def finish_second_half():
            tot = jnp.sum(cnt_ref[...], axis=1, keepdims=True)
            return jnp.where(tot >= kf, cand_ref[...], lo_ref[hs:, :])

        def vbody(i, carry):
            inc = jnp.left_shift(jnp.int32(1), 31 - i)
            lo_b = finish_second_half()
            lo_a = lo_ref[:hs, :]
            cand_a = lo_a + inc
            tot_a = jnp.sum(lane_counts(0, cand_a), axis=1, keepdims=True)
            cand_b = lo_b + inc
            cnt_ref[...] = lane_counts(hs, cand_b)
            cand_ref[...] = cand_b
            lo_ref[hs:, :] = lo_b
            lo_ref[:hs, :] = jnp.where(tot_a >= kf, cand_a, lo_a)
            return carry

        lax.fori_loop(0, 32, vbody, 0)
        lo_ref[hs:, :] = finish_second_half()
        thr = lo_ref[:, 0:1]
        key = key_ref[:, :ext]
        n_gt = count(key > thr)
        n_eq = count(key == thr)
        need = kf - n_gt
        n_eq_vis = jnp.sum(jnp.where(key == thr, vis, 0.0), axis=1, keepdims=True)
        mask_ref[:, :ext] = jnp.where(key >= thr, jnp.where(vis > 0.0, 0.0, NEG), NEG)
        surplus = jnp.where(n_eq_vis > 0.0, jnp.where(n_eq > need, 1.0, 0.0), 0.0)

        @pl.when(jnp.max(surplus) > 0.0)
        def _():
            nbits = max(1, (ext - 1).bit_length())
            keyv = key_ref[:, :ext]
            col = lax.broadcasted_iota(jnp.int32, (tq, ext), 1)

            def tbody(i, j):
                cand = j + jnp.left_shift(jnp.int32(1), nbits - 1 - i)
                f = jnp.sum(jnp.where(keyv == thr, jnp.where(col < cand, 1.0, 0.0), 0.0), axis=1, keepdims=True)
                return jnp.where(f < need, cand, j)

            jstar = lax.fori_loop(0, nbits, tbody, jnp.zeros((tq, 1), jnp.int32))
            chosen = jnp.where(keyv > thr, 1.0, jnp.where(keyv == thr, jnp.where(col <= jstar, 1.0, 0.0), 0.0))
            mask_ref[:, :ext] = jnp.where(chosen > 0.0, jnp.where(vis > 0.0, 0.0, NEG), NEG)

        scale2 = A_HDIM ** -0.5 * LOG2E

        def hbody(h, carry):
            r0 = pl.multiple_of(h * tq, tq)
            s = _dot_nt(qs_ref[pl.ds(r0, tq), :], ak_ref[:ext, :]) * scale2 + bias_ref[h, :, :ext] + mask_ref[:, :ext]
            m = jnp.max(s, axis=1, keepdims=True)
            e = jnp.exp2(s - m)
            l = jnp.sum(e, axis=1, keepdims=True)
            os_ref[pl.ds(r0, tq), :] = _dot(e.astype(BF16), av_ref[:ext, :]) / l
            return carry

        lax.fori_loop(0, A_HEADS, hbody, 0, unroll=2)

    _for_each_extent(qi, p_len // tq, tq, p_len, body)
    o = jnp.concatenate([os_ref[h * tq:(h + 1) * tq, :] for h in range(A_HEADS)], axis=1)
    o_ref[...] = _rms(o, g_ref[...]).astype(BF16)


def _attn_a(pf, pb, tabs_idx, bias, g, batch, p_len, tq, k_top, col):
    nq = p_len // tq
    t = batch * p_len
    kern = functools.partial(_attn_a_kernel, tq=tq, p_len=p_len, k_top=k_top)
    return pl.pallas_call(
        kern,
        grid=(batch, nq),
        in_specs=[
            pl.BlockSpec((tq, 1024), lambda b, q: (b * nq + q, col["i_q"] // 1024)),
            pl.BlockSpec((tq, LANES), lambda b, q: (b * nq + q, col["ikw"] // LANES)),
            pl.BlockSpec((p_len, LANES), lambda b, q: (b, col["ikw"] // LANES)),
            pl.BlockSpec((3, tq, LANES), lambda b, q: (0, q, 0)),
            pl.BlockSpec((3, p_len, LANES), lambda b, q: (0, 0, 0)),
            pl.BlockSpec((tq, 1024), lambda b, q: (b * nq + q, col["a_q"] // 1024)),
            pl.BlockSpec((p_len, LANES), lambda b, q: (b, col["a_k"] // LANES)),
            pl.BlockSpec((p_len, LANES), lambda b, q: (b, col["a_v"] // LANES)),
            pl.BlockSpec((A_HEADS, 1, p_len), lambda b, q: (0, 0, 0)),
            pl.BlockSpec((1, GROUP_WIDTH), lambda b, q: (0, 0)),
        ],
        out_specs=pl.BlockSpec((tq, GROUP_WIDTH), lambda b, q: (b * nq + q, 0)),
        out_shape=jax.ShapeDtypeStruct((t, GROUP_WIDTH), BF16),
        scratch_shapes=[pltpu.VMEM((tq, p_len), jnp.int32),
                        pltpu.VMEM((tq, p_len), F32),
                        pltpu.VMEM((A_HEADS * tq, A_HDIM), BF16),
                        pltpu.VMEM((A_HEADS * tq, A_HDIM), F32),
                        pltpu.VMEM((tq, LANES), jnp.int32),
                        pltpu.VMEM((tq // 2, LANES), jnp.int32),
                        pltpu.VMEM((tq // 2, LANES), F32)],
        compiler_params=_cparams(2),
        name="attn_a",
    )(pf, pf, pf, tabs_idx, tabs_idx, pb, pb, pb, bias, g.reshape(1, GROUP_WIDTH))


def _attn_b_kernel(q_ref, kp_ref, kc_ref, vp_ref, vc_ref, bias_ref, sink_ref, g_ref, o_ref):
    n = pl.program_id(1)
    grp = B_HEADS // B_KV_HEADS
    slabs = grp // 2
    rows = slabs * BLK
    first_col = PAD_FRONT - (n - 1) * BLK
    col = lax.broadcasted_iota(jnp.int32, (rows, 2 * BLK), 1)
    pad_mask = jnp.where(col >= first_col, 0.0, NEG)
    lane = lax.broadcasted_iota(jnp.int32, (2 * BLK, LANES), 1)
    low = lane < B_HDIM
    zero = jnp.zeros((2 * BLK, LANES), BF16)
    out_slabs = []
    swap = lambda x: jnp.concatenate([x[:, B_HDIM:], x[:, :B_HDIM]], axis=1)
    kplain = jnp.concatenate([kp_ref[...], kc_ref[...]], axis=0)
    vplain = jnp.concatenate([vp_ref[...], vc_ref[...]], axis=0)
    kswap, vswap = swap(kplain), swap(vplain)
    for gi in range(B_KV_HEADS):
        q = jnp.concatenate([q_ref[:, (gi * slabs + sl) * LANES:(gi * slabs + sl + 1) * LANES] for sl in range(slabs)], axis=0)
        lo_src_k, hi_src_k = (kplain, kswap) if gi == 0 else (kswap, kplain)
        lo_src_v, hi_src_v = (vplain, vswap) if gi == 0 else (vswap, vplain)
        acc = None
        for par in range(2):
            keep = low if par == 0 else jnp.logical_not(low)
            k = jnp.where(keep, lo_src_k if par == 0 else hi_src_k, zero)
            v = jnp.where(keep, lo_src_v if par == 0 else hi_src_v, zero)
            s = _dot_nt(q, k) + bias_ref[gi, par] + pad_mask
            sink = sink_ref[gi, par]
            m = jnp.maximum(jnp.max(s, axis=1, keepdims=True), sink)
            e = jnp.exp(s - m)
            l = jnp.sum(e, axis=1, keepdims=True) + jnp.exp(sink - m)
            part = _dot(e.astype(BF16), v) * (1.0 / l)
            acc = part if acc is None else acc + part
        out_slabs.extend(acc[sl * BLK:(sl + 1) * BLK, :] for sl in range(slabs))
    o = jnp.concatenate(out_slabs, axis=1)
    o_ref[...] = _rms(o, g_ref[...]).astype(BF16)


def _attn_b(pb, bias, sink_rows, g, batch, p_len, col):
    nb = p_len // BLK
    t = batch * p_len
    grp = B_HEADS // B_KV_HEADS
    rows = grp // 2 * BLK
    prev = lambda b, n: b * nb + jnp.maximum(n - 1, 0)
    cur = lambda b, n: b * nb + n
    kv_specs = []
    for name in ("b_k", "b_v"):
        c = col[name] // LANES
        kv_specs.append(pl.BlockSpec((BLK, LANES), lambda b, n, c=c: (prev(b, n), c)))
        kv_specs.append(pl.BlockSpec((BLK, LANES), lambda b, n, c=c: (cur(b, n), c)))
    return pl.pallas_call(
        _attn_b_kernel,
        grid=(batch, nb),
        in_specs=[pl.BlockSpec((BLK, 1024), lambda b, n: (cur(b, n), col["b_q"] // 1024))] + kv_specs + [
            pl.BlockSpec((B_KV_HEADS, 2, rows, 2 * BLK), lambda b, n: (0, 0, 0, 0)),
            pl.BlockSpec((B_KV_HEADS, 2, rows, 1), lambda b, n: (0, 0, 0, 0)),
            pl.BlockSpec((1, GROUP_WIDTH), lambda b, n: (0, 0)),
        ],
        out_specs=pl.BlockSpec((BLK, GROUP_WIDTH), lambda b, n: (cur(b, n), 0)),
        out_shape=jax.ShapeDtypeStruct((t, GROUP_WIDTH), BF16),
        compiler_params=_cparams(2),
        name="attn_b",
    )(pb, *([pb] * 4), bias, sink_rows, g.reshape(1, GROUP_WIDTH))


def _c_prep_kernel(cq_ref, ckv_ref, kr_ref, tab_ref, gq_ref, gkv_ref, wq_ref, wkv_ref, q_ref, k_ref, v_ref):
    half = C_ROPE // 2
    tc, ts1, ts2 = tab_ref[0], tab_ref[1], tab_ref[2]
    slot = C_NOPE + LANES
    xq = _rms(cq_ref[...], gq_ref[...]).astype(BF16)
    q = _dot(xq, wq_ref[...])
    xkv = _rms(ckv_ref[...], gkv_ref[...]).astype(BF16)
    kv = _dot(xkv, wkv_ref[...])
    kr = _rope128(kr_ref[...], tc, ts1, ts2, half).astype(BF16)
    for h in range(C_HEADS):
        lo = h * slot
        q_ref[h, :, :C_NOPE] = q[:, lo:lo + C_NOPE].astype(BF16)
        q_ref[h, :, C_NOPE:] = _rope128(q[:, lo + C_NOPE:lo + slot], tc, ts1, ts2, half).astype(BF16)
        k_ref[h, :, :C_NOPE] = kv[:, h * C_NOPE:(h + 1) * C_NOPE].astype(BF16)
        k_ref[h, :, C_NOPE:] = kr
        v_ref[h] = kv[:, (C_HEADS + h) * C_NOPE:(C_HEADS + h + 1) * C_NOPE].astype(BF16)


def _c_prep(pf, tabs, g_cq, g_ckv, w_uq_p, w_ukv_p, p_len, tm, col):
    t = pf.shape[0]
    tiles = p_len // tm
    slot = C_NOPE + LANES
    return pl.pallas_call(
        _c_prep_kernel,
        grid=(t // tm,),
        in_specs=[
            pl.BlockSpec((tm, C_Q_RANK), lambda i: (i, col["c_cq"] // C_Q_RANK)),
            pl.BlockSpec((tm, C_KV_RANK), lambda i: (i, col["c_ckv"] // C_KV_RANK)),
            pl.BlockSpec((tm, LANES), lambda i: (i, col["c_kr"] // LANES)),
            pl.BlockSpec((3, tm, LANES), lambda i: (0, i % tiles, 0)),
            pl.BlockSpec((1, C_Q_RANK), lambda i: (0, 0)),
            pl.BlockSpec((1, C_KV_RANK), lambda i: (0, 0)),
            pl.BlockSpec((C_Q_RANK, C_HEADS * slot), lambda i: (0, 0)),
            pl.BlockSpec((C_KV_RANK, C_HEADS * (C_NOPE + C_VDIM)), lambda i: (0, 0)),
        ],
        out_specs=[pl.BlockSpec((C_HEADS, tm, slot), lambda i: (0, i, 0)),
                   pl.BlockSpec((C_HEADS, tm, slot), lambda i: (0, i, 0)),
                   pl.BlockSpec((C_HEADS, tm, C_VDIM), lambda i: (0, i, 0))],
        out_shape=[jax.ShapeDtypeStruct((C_HEADS, t, slot), BF16),
                   jax.ShapeDtypeStruct((C_HEADS, t, slot), BF16),
                   jax.ShapeDtypeStruct((C_HEADS, t, C_VDIM), BF16)],
        compiler_params=_cparams(1),
        name="c_prep",
    )(pf, pf, pf, tabs, g_cq.reshape(1, -1), g_ckv.reshape(1, -1), w_uq_p, w_ukv_p)


def _causal_mask(q0, tq, ext):
    qpos = q0 + lax.broadcasted_iota(jnp.int32, (tq, ext), 0)
    kpos = lax.broadcasted_iota(jnp.int32, (tq, ext), 1)
    return jnp.where(kpos <= qpos, jnp.where(kpos >= PAD_FRONT, 0.0, NEG), NEG)


def _attn_c_kernel(q_ref, k_ref, v_ref, g_ref, o_ref, os_ref, *, tq, p_len):
    qi = pl.program_id(1)
    scale2 = (C_NOPE + C_ROPE) ** -0.5 * LOG2E

    def body(ext):
        mask = _causal_mask(qi * tq, tq, ext)

        def hbody(h, carry):
            s = _dot_nt(q_ref[h], k_ref[h, :ext, :]) * scale2 + mask
            m = jnp.max(s, axis=1, keepdims=True)
            e = jnp.exp2(s - m)
            l = jnp.sum(e, axis=1, keepdims=True)
            os_ref[h] = _dot(e.astype(BF16), v_ref[h, :ext, :]) / l
            return carry

        lax.fori_loop(0, C_HEADS, hbody, 0, unroll=2)

    nq = p_len // tq
    _for_each_extent(qi, nq, tq, p_len, body, nq)
    o = jnp.concatenate([os_ref[h] for h in range(C_HEADS)], axis=1)
    o_ref[...] = _rms(o, g_ref[...]).astype(BF16)


def _attn_c(qc, kc, vc, g, batch, p_len, tq):
    nq = p_len // tq
    t = batch * p_len
    slot = C_NOPE + LANES
    kern = functools.partial(_attn_c_kernel, tq=tq, p_len=p_len)
    return pl.pallas_call(
        kern,
        grid=(batch, nq),
        in_specs=[pl.BlockSpec((C_HEADS, tq, slot), lambda b, q: (0, b * nq + q, 0)),
                  pl.BlockSpec((C_HEADS, p_len, slot), lambda b, q: (0, b, 0)),
                  pl.BlockSpec((C_HEADS, p_len, C_VDIM), lambda b, q: (0, b, 0)),
                  pl.BlockSpec((1, GROUP_WIDTH), lambda b, q: (0, 0))],
        out_specs=pl.BlockSpec((tq, GROUP_WIDTH), lambda b, q: (b * nq + q, 0)),
        out_shape=jax.ShapeDtypeStruct((t, GROUP_WIDTH), BF16),
        scratch_shapes=[pltpu.VMEM((C_HEADS, tq, C_VDIM), F32)],
        compiler_params=_cparams(2),
        name="attn_c",
    )(qc, kc, vc, g.reshape(1, GROUP_WIDTH))


def _attn_d_kernel(q_ref, k_ref, v_ref, bias_ref, lam_ref, g_ref, o_ref, os_ref, *, tq, p_len, lam_init):
    qi = pl.program_id(1)
    lp = lam_ref[...]
    lam = (jnp.exp(jnp.sum(lp[0:1] * lp[1:2], axis=1, keepdims=True))
           - jnp.exp(jnp.sum(lp[2:3] * lp[3:4], axis=1, keepdims=True)) + lam_init)

    def body(ext):
        mask = _causal_mask(qi * tq, tq, ext)

        def hbody(h, carry):
            bm = bias_ref[h, :, :ext] + mask
            qh = q_ref[h]
            kh = k_ref[h, :ext, :]
            es, ls = [], []
            for c in range(2):
                lo = c * D_HDIM
                s = _dot_nt(qh[:, lo:lo + D_HDIM], kh[:, lo:lo + D_HDIM]) + bm
                m = jnp.max(s, axis=1, keepdims=True)
                e = jnp.exp(s - m)
                es.append(e)
                ls.append(jnp.sum(e, axis=1, keepdims=True))
            a = es[0] * (1.0 / ls[0]) - es[1] * (lam / ls[1])
            os_ref[h] = _dot(a.astype(BF16), v_ref[h, :ext, :])
            return carry

        lax.fori_loop(0, D_HEADS, hbody, 0, unroll=2)

    nq = p_len // tq
    _for_each_extent(qi, nq, tq, p_len, body, nq)
    g = g_ref[...]
    outs = [_rms(os_ref[h], g) * (1.0 - lam_init) for h in range(D_HEADS)]
    o_ref[...] = jnp.concatenate(outs, axis=1).astype(BF16)


def _attn_d(pd, bias, lam_p, g, batch, p_len, tq, lam_init):
    nq = p_len // tq
    t = batch * p_len
    kern = functools.partial(_attn_d_kernel, tq=tq, p_len=p_len, lam_init=lam_init)
    return pl.pallas_call(
        kern,
        grid=(batch, nq),
        in_specs=[pl.BlockSpec((D_HEADS, tq, LANES), lambda b, q: (0, b * nq + q, 0)),
                  pl.BlockSpec((D_HEADS, p_len, LANES), lambda b, q: (1, b, 0)),
                  pl.BlockSpec((D_HEADS, p_len, LANES), lambda b, q: (2, b, 0)),
                  pl.BlockSpec((D_HEADS, 1, p_len), lambda b, q: (0, 0, 0)),
                  pl.BlockSpec((4, D_HDIM), lambda b, q: (0, 0)),
                  pl.BlockSpec((1, 2 * D_HDIM), lambda b, q: (0, 0))],
        out_specs=pl.BlockSpec((tq, GROUP_WIDTH), lambda b, q: (b * nq + q, 0)),
        out_shape=jax.ShapeDtypeStruct((t, GROUP_WIDTH), BF16),
        scratch_shapes=[pltpu.VMEM((D_HEADS, tq, 2 * D_HDIM), F32)],
        compiler_params=_cparams(2),
        name="attn_d",
    )(pd, pd, pd, bias, lam_p, g.reshape(1, 2 * D_HDIM))


def _ffn_up_kernel(x_ref, wg_ref, wu_ref, cwg_ref, cwu_ref, cbg_ref, cbu_ref, o_ref,
                   wgb_ref, wub_ref, ha_ref, hb_ref, carry_ref, *, tm, n_row_tiles, tiles_per_seq):
    s = pl.program_id(0)
    d = x_ref.shape[1]
    kc = min(d, MXU_DEPTH)
    nk = d // kc
    rc = next(r for r in (64, 32, 16, 8) if tm % r == 0)
    nr = tm // rc

    @pl.when(s % n_row_tiles == 0)
    def _():
        wgb_ref[...] = wg_ref[...].astype(BF16)
        wub_ref[...] = wu_ref[...].astype(BF16)

    @pl.when(s == 0)
    def _():
        hb_ref[...] = jnp.zeros_like(hb_ref)

    @pl.when((s + tiles_per_seq - 1) % tiles_per_seq == 0)
    def _():
        carry_ref[...] = jnp.zeros_like(carry_ref)

    def run(rd_ref, wr_ref):
        cws = (cwg_ref[...], cwu_ref[...])
        cbs = (cbg_ref[...], cbu_ref[...])
        for step in range(nr):
            for kk in range((step * nk) // nr, ((step + 1) * nk) // nr):
                for b, wb_ref in enumerate((wgb_ref, wub_ref)):
                    part = _dot(x_ref[:, kk * kc:(kk + 1) * kc], wb_ref[kk * kc:(kk + 1) * kc, :])
                    if kk == 0:
                        wr_ref[b] = part
                    else:
                        wr_ref[b] = wr_ref[b] + part
            r0 = step * rc
            ys = []
            for b in range(2):
                top = carry_ref[b] if step == 0 else rd_ref[b, r0 - 8:r0, :]
                cur = rd_ref[b, r0:r0 + rc, :]
                full = jnp.concatenate([top, cur], axis=0)
                cw = cws[b]
                y = cw[2:3] * cur + cw[1:2] * pltpu.roll(full, 1, 0)[8:] + cw[0:1] * pltpu.roll(full, 2, 0)[8:]
                ys.append(y + cbs[b])
            o_ref[r0:r0 + rc, :] = (ys[0] * jax.nn.sigmoid(ys[0]) * ys[1]).astype(BF16)
        for b in range(2):
            carry_ref[b] = rd_ref[b, tm - 8:, :]

    @pl.when(s % 2 == 0)
    def _():
        run(hb_ref, ha_ref)

    @pl.when(s % 2 == 1)
    def _():
        run(ha_ref, hb_ref)


def _ffn_up(hb, w_up, conv_w, conv_b, layer, p_len, tm, tn):
    t, d = hb.shape
    f = w_up.shape[2] // 2
    assert f % tn == 0 and tm % 8 == 0 and p_len % tm == 0
    nj, nt = f // tn, t // tm
    n_tiles = nj * nt
    kern = functools.partial(_ffn_up_kernel, tm=tm, n_row_tiles=nt, tiles_per_seq=p_len // tm)
    cb = conv_b.reshape(conv_b.shape[0], 1, 2 * f)
    cur = lambda s: jnp.minimum(s, n_tiles - 1)
    prv = lambda s: jnp.maximum(s - 1, 0)
    return pl.pallas_call(
        kern,
        grid=(n_tiles + 1,),
        in_specs=[pl.BlockSpec((tm, d), lambda s: (cur(s) % nt, 0)),
                  pl.BlockSpec((None, d, tn), lambda s: (layer, 0, cur(s) // nt)),
                  pl.BlockSpec((None, d, tn), lambda s: (layer, 0, nj + cur(s) // nt)),
                  pl.BlockSpec((None, CONV_W, tn), lambda s: (layer, 0, prv(s) // nt)),
                  pl.BlockSpec((None, CONV_W, tn), lambda s: (layer, 0, nj + prv(s) // nt)),
                  pl.BlockSpec((None, 1, tn), lambda s: (layer, 0, prv(s) // nt)),
                  pl.BlockSpec((None, 1, tn), lambda s: (layer, 0, nj + prv(s) // nt))],
        out_specs=pl.BlockSpec((tm, tn), lambda s: (prv(s) % nt, prv(s) // nt)),
        out_shape=jax.ShapeDtypeStruct((t, f), BF16),
        scratch_shapes=[pltpu.VMEM((d, tn), BF16), pltpu.VMEM((d, tn), BF16),
                        pltpu.VMEM((2, tm, tn), F32), pltpu.VMEM((2, tm, tn), F32), pltpu.VMEM((2, 8, tn), F32)],
        compiler_params=_cparams(1),
        name="ffn_up",
    )(hb, w_up, w_up, conv_w, conv_w, cb, cb)


def _offsets(names, sizes):
    out, o = {}, 0
    for n, s in zip(names, sizes):
        out[n] = o
        o += s
    return out, o


_BF_NAMES = ("a_q", "b_q", "a_k", "a_v", "b_k", "b_v")
_BF_SIZES = (1024, 1024, 128, 128, 128, 128)
_D_NAMES = ("d_q", "d_k", "d_v")
_F32_NAMES = ("i_q", "c_cq", "c_ckv", "ikw", "c_kr")
_F32_SIZES = (1024, 1024, 512, 128, 128)
COL_BF, N_BF = _offsets(_BF_NAMES, _BF_SIZES)
COL_F32, N_F32 = _offsets(_F32_NAMES, _F32_SIZES)


def _prep_w_in(w):
    d = w.shape[0]
    seg = dict(zip(("a_q", "a_k", "a_v", "i_q", "i_k", "i_w", "b_q", "b_k", "b_v", "c_cq", "c_ckv", "c_kr",
                    "d_q", "d_k", "d_v"), jnp.split(w, np.cumsum(IN_SIZES)[:-1].tolist(), axis=1)))
    seg["b_q"] = seg["b_q"] * (B_HDIM ** -0.5)
    seg["d_q"] = seg["d_q"] * (D_HDIM ** -0.5)
    wb = jnp.concatenate([seg[n] for n in _BF_NAMES], axis=1).astype(BF16)
    wd = jnp.concatenate([seg[n] for n in _D_NAMES], axis=1).astype(BF16)
    zeros = lambda n: jnp.zeros((d, n), w.dtype)
    ikw = jnp.concatenate([seg["i_k"], seg["i_w"], zeros(LANES - IDX_HDIM - IDX_HEADS)], axis=1)
    ckr = jnp.concatenate([seg["c_kr"], zeros(LANES - C_ROPE)], axis=1)
    wf = jnp.concatenate([seg["i_q"], seg["c_cq"], seg["c_ckv"], ikw, ckr], axis=1).astype(BF16)
    return wb, wd, wf


def _prep_w_uq(w):
    r = w.shape[0]
    w3 = w.reshape(r, C_HEADS, C_NOPE + C_ROPE)
    return jnp.pad(w3, ((0, 0), (0, 0), (0, LANES - C_ROPE))).reshape(r, C_HEADS * (C_NOPE + LANES)).astype(BF16)


def _prep_w_ukv(w):
    r = w.shape[0]
    w3 = w.reshape(r, C_HEADS, C_NOPE + C_VDIM)
    return jnp.concatenate([w3[:, :, :C_NOPE].reshape(r, -1), w3[:, :, C_NOPE:].reshape(r, -1)], axis=1).astype(BF16)


def _alibi(n):
    return 2.0 ** (-8.0 * np.arange(1, n + 1, dtype=np.float64) / n)


def _swa_bias():
    grp = B_HEADS // B_KV_HEADS
    r = np.arange(BLK)[:, None]
    c = np.arange(2 * BLK)[None, :]
    diff = (r + BLK - c).astype(np.float64)
    ok = (diff >= 0) & (diff < WINDOW)
    slopes = _alibi(B_HEADS).reshape(B_KV_HEADS, grp // 2, 2).transpose(0, 2, 1)
    bias = np.where(ok[None, None, None], -slopes[..., None, None] * diff[None, None, None], NEG)
    return jnp.asarray(bias.reshape(B_KV_HEADS, 2, grp // 2 * BLK, 2 * BLK), F32)


def _swa_sinks(sinks):
    grp = B_HEADS // B_KV_HEADS
    sk = sinks.astype(F32).reshape(B_KV_HEADS, grp // 2, 2).transpose(0, 2, 1)
    return jnp.repeat(sk, BLK, axis=2)[..., None]


def kernel(x, meta_tokens, ln_in_g, ln_in_b, w_in, g_cq, g_ckv, w_uq, w_ukv, sinks, lam_q1, lam_k1, lam_q2, lam_k2,
           g_diff, g_grp, w_out, ln1_g, ln1_b, w_up, conv_w, conv_b, w_down, ln2_g, ln2_b):
    batch, s_len, d = x.shape
    depth = w_in.shape[0]
    p_len = s_len + BLK
    t = batch * p_len
    k_top = min(TOPK_MAX, s_len // 4)
    alpha = (2 * depth) ** 0.25
    tm = _row_tile(p_len, 544)
    tm_ln = _row_tile(p_len, 272)
    tm_ffn = _row_tile(p_len, 1088)
    tq = BLK
    grp = B_HEADS // B_KV_HEADS

    h, hb = _ln_in(x, meta_tokens, ln_in_g, ln_in_b)

    pos = jnp.arange(p_len, dtype=jnp.int32) - PAD_FRONT
    tabs_idx = _rope_tables(pos, IDX_ROPE, IDX_HDIM)
    tabs_c = _rope_tables(pos, C_ROPE, LANES)
    kidx = np.arange(p_len, dtype=np.float64)
    bias_a = jnp.asarray(_alibi(A_HEADS)[:, None, None] * kidx[None, None, :] * LOG2E, F32)
    bias_d = jnp.asarray(_alibi(D_HEADS)[:, None, None] * kidx[None, None, :], F32)
    bias_b = _swa_bias()

    for l in range(depth):
        wb, wd, wf = _prep_w_in(w_in[l])
        pb = _mm(hb, wb, tm_ffn, 2 * MXU_DEPTH, BF16, "proj_bf16")
        pd = _mm_slab(hb, wd, tm_ffn, 2 * MXU_DEPTH, BF16, "proj_d")
        pf = _mm(hb, wf, tm, N_F32 // 2, F32, "proj_f32")

        o_a = _attn_a(pf, pb, tabs_idx, bias_a, g_grp[l, 0], batch, p_len, tq, k_top, {**COL_F32, **COL_BF})
        o_b = _attn_b(pb, bias_b, _swa_sinks(sinks[l]), g_grp[l, 1], batch, p_len, COL_BF)
        qc, kc, vc = _c_prep(pf, tabs_c, g_cq[l], g_ckv[l], _prep_w_uq(w_uq[l]), _prep_w_ukv(w_ukv[l]),
                               p_len, tm_ln, COL_F32)
        o_c = _attn_c(qc, kc, vc, g_grp[l, 2], batch, p_len, tm_ln)
        lam_init = 0.8 - 0.6 * math.exp(-0.3 * l)
        lam_p = jnp.stack([lam_q1[l], lam_k1[l], lam_q2[l], lam_k2[l]]).astype(F32)
        o_d = _attn_d(pd, bias_d, lam_p, g_diff[l], batch, p_len, tm_ln, lam_init)

        mix = _mm_groups([o_a, o_b, o_c, o_d], _cast_layer(w_out, l, BF16), tm, min(1024, d), F32, "out_proj")
        h, hb_ffn = _res_ln(h, mix, ln1_g[l], ln1_b[l], alpha, tm_ln, p_len, True)

        act = _ffn_up(hb_ffn, w_up, conv_w, conv_b, l, p_len, tm_ffn, FF_TILE)
        ffn = _mm(act, _cast_layer(w_down, l, BF16), tm, min(512, d), F32, "ffn_down")
        if l + 1 < depth:
            h, hb = _res_ln(h, ffn, ln2_g[l], ln2_b[l], alpha, tm_ln, p_len, False)
        else:
            out = _res_ln_final(h, ffn, ln2_g[l], ln2_b[l], alpha, batch, p_len)

    return out.reshape(batch, s_len, d)
```

```python
import functools
import math

import jax
import jax.numpy as jnp
import numpy as np
from jax import lax
from jax.experimental import pallas as pl
from jax.experimental.pallas import tpu as pltpu

N_META = 16
BLK = 128
PAD_FRONT = BLK - N_META
A_HEADS, A_HDIM = 8, 128
IDX_HEADS, IDX_HDIM, IDX_ROPE = 16, 64, 32
TOPK_MAX = 256
B_HEADS, B_KV_HEADS, B_HDIM = 16, 2, 64
WINDOW = 128
C_HEADS, C_Q_RANK, C_KV_RANK, C_NOPE, C_ROPE, C_VDIM = 8, 1024, 512, 128, 64, 128
D_HEADS, D_HDIM = 8, 64
GROUP_WIDTH = 1024
CONV_W = 3
ROPE_BASE = 10000.0
NEG = -1e30
IN_SIZES = (1024, 128, 128, 1024, 64, 16, 1024, 128, 128, 1024, 512, 64, 1024, 1024, 1024)

LANES = 128
BF16_SUBLANES = 16
MXU_DEPTH = 256
VMEM_LIMIT = 56 * 1024 * 1024
FF_TILE = 256
N_EXTENT_VARIANTS = 6
LOG2E = 1.4426950408889634

F32 = jnp.float32
BF16 = jnp.bfloat16


def _cparams(n_axes):
    return pltpu.CompilerParams(dimension_semantics=("arbitrary",) * n_axes, vmem_limit_bytes=VMEM_LIMIT)


def _row_tile(p_len, target):
    best = BF16_SUBLANES
    for t in range(BF16_SUBLANES, target + 1, BF16_SUBLANES):
        if p_len % t == 0:
            best = t
    return best


def _dot(a, b):
    return jnp.dot(a, b, preferred_element_type=F32)


def _dot_nt(a, b):
    return lax.dot_general(a, b, (((1,), (1,)), ((), ())), preferred_element_type=F32)


def _ln_rows(x, g, b):
    mu = jnp.mean(x, axis=-1, keepdims=True)
    xc = x - mu
    var = jnp.mean(xc * xc, axis=-1, keepdims=True)
    return xc * lax.rsqrt(var + 1e-5) * g + b


def _ln_in_kernel(x_ref, meta_ref, g_ref, b_ref, h_ref, hb_ref):
    def emit(rows):
        y = _ln_rows(rows, g_ref[...], b_ref[...])
        h_ref[...] = y
        hb_ref[...] = y.astype(BF16)

    @pl.when(pl.program_id(1) == 0)
    def _():
        emit(jnp.concatenate([jnp.zeros((PAD_FRONT, meta_ref.shape[1]), F32), meta_ref[...]], axis=0))

    @pl.when(pl.program_id(1) > 0)
    def _():
        emit(x_ref[...])


def _ln_in(x, meta, g, b):
    batch, s_len, d = x.shape
    nb = s_len // BLK + 1
    t = batch * nb * BLK
    return pl.pallas_call(
        _ln_in_kernel,
        grid=(batch, nb),
        in_specs=[pl.BlockSpec((None, BLK, d), lambda bi, n: (bi, jnp.maximum(n - 1, 0), 0)),
                  pl.BlockSpec((N_META, d), lambda bi, n: (0, 0)),
                  pl.BlockSpec((1, d), lambda bi, n: (0, 0)),
                  pl.BlockSpec((1, d), lambda bi, n: (0, 0))],
        out_specs=[pl.BlockSpec((BLK, d), lambda bi, n: (bi * nb + n, 0)),
                   pl.BlockSpec((BLK, d), lambda bi, n: (bi * nb + n, 0))],
        out_shape=[jax.ShapeDtypeStruct((t, d), F32), jax.ShapeDtypeStruct((t, d), BF16)],
        compiler_params=_cparams(2),
        name="ln_in",
    )(x, meta.astype(F32), g.reshape(1, d), b.reshape(1, d))


def _res_ln_kernel(h_ref, y_ref, g_ref, b_ref, o_ref, ob_ref, *, alpha, tm, tiles_per_seq, zero_pad):
    out = _ln_rows(alpha * h_ref[...] + y_ref[...], g_ref[...], b_ref[...])
    o_ref[...] = out
    if zero_pad:
        p0 = (pl.program_id(0) % tiles_per_seq) * tm
        pos = p0 + lax.broadcasted_iota(jnp.int32, out.shape, 0)
        out = jnp.where(pos >= PAD_FRONT, out, 0.0)
    ob_ref[...] = out.astype(BF16)


def _res_ln(h, y, g, b, alpha, tm, p_len, zero_pad):
    t, d = h.shape
    kern = functools.partial(_res_ln_kernel, alpha=alpha, tm=tm, tiles_per_seq=p_len // tm, zero_pad=zero_pad)
    return pl.pallas_call(
        kern,
        grid=(t // tm,),
        in_specs=[pl.BlockSpec((tm, d), lambda i: (i, 0)),
                  pl.BlockSpec((tm, d), lambda i: (i, 0)),
                  pl.BlockSpec((1, d), lambda i: (0, 0)),
                  pl.BlockSpec((1, d), lambda i: (0, 0))],
        out_specs=[pl.BlockSpec((tm, d), lambda i: (i, 0)),
                   pl.BlockSpec((tm, d), lambda i: (i, 0))],
        out_shape=[jax.ShapeDtypeStruct((t, d), F32), jax.ShapeDtypeStruct((t, d), BF16)],
        compiler_params=_cparams(1),
        name="res_ln",
    )(h, y, g.reshape(1, d), b.reshape(1, d))


def _res_ln_final_kernel(h_ref, y_ref, g_ref, b_ref, o_ref, *, alpha):
    o_ref[...] = _ln_rows(alpha * h_ref[...] + y_ref[...], g_ref[...], b_ref[...])


def _res_ln_final(h, y, g, b, alpha, batch, p_len):
    t, d = h.shape
    nb = p_len // BLK
    out_idx = lambda i: ((i // nb) * (nb - 1) + jnp.maximum(i % nb - 1, 0), 0)
    return pl.pallas_call(
        functools.partial(_res_ln_final_kernel, alpha=alpha),
        grid=(t // BLK,),
        in_specs=[pl.BlockSpec((BLK, d), lambda i: (i, 0)),
                  pl.BlockSpec((BLK, d), lambda i: (i, 0)),
                  pl.BlockSpec((1, d), lambda i: (0, 0)),
                  pl.BlockSpec((1, d), lambda i: (0, 0))],
        out_specs=pl.BlockSpec((BLK, d), out_idx),
        out_shape=jax.ShapeDtypeStruct((batch * (p_len - BLK), d), F32),
        compiler_params=_cparams(1),
        name="res_ln_final",
    )(h, y, g.reshape(1, d), b.reshape(1, d))


def _mm_kernel(a_ref, w_ref, o_ref):
    o_ref[...] = _dot(a_ref[...], w_ref[...]).astype(o_ref.dtype)


def _mm(a, w, tm, tn, out_dtype, name):
    t, k = a.shape
    n = w.shape[1]
    return pl.pallas_call(
        _mm_kernel,
        grid=(n // tn, t // tm),
        in_specs=[pl.BlockSpec((tm, k), lambda j, i: (i, 0)),
                  pl.BlockSpec((k, tn), lambda j, i: (0, j))],
        out_specs=pl.BlockSpec((tm, tn), lambda j, i: (i, j)),
        out_shape=jax.ShapeDtypeStruct((t, n), out_dtype),
        compiler_params=_cparams(2),
        name=name,
    )(a, w)


def _cast_kernel(w_ref, o_ref):
    o_ref[...] = w_ref[...].astype(o_ref.dtype)


def _cast_layer(w, layer, dtype):
    _, r, c = w.shape
    rb = _row_tile(r, 256)
    return pl.pallas_call(
        _cast_kernel,
        grid=(r // rb,),
        in_specs=[pl.BlockSpec((None, rb, c), lambda i: (layer, i, 0))],
        out_specs=pl.BlockSpec((rb, c), lambda i: (i, 0)),
        out_shape=jax.ShapeDtypeStruct((r, c), dtype),
        compiler_params=_cparams(1),
        name="cast_w",
    )(w)


def _mm_groups_kernel(*refs):
    *a_refs, w_ref, o_ref = refs
    k0 = 0
    acc = None
    for a_ref in a_refs:
        kw = a_ref.shape[1]
        part = _dot(a_ref[...], w_ref[k0:k0 + kw, :])
        acc = part if acc is None else acc + part
        k0 += kw
    o_ref[...] = acc.astype(o_ref.dtype)


def _mm_groups(a_list, w, tm, tn, out_dtype, name):
    t = a_list[0].shape[0]
    k, n = w.shape
    assert sum(a.shape[1] for a in a_list) == k
    return pl.pallas_call(
        _mm_groups_kernel,
        grid=(n // tn, t // tm),
        in_specs=[pl.BlockSpec((tm, a.shape[1]), lambda j, i: (i, 0)) for a in a_list]
        + [pl.BlockSpec((k, tn), lambda j, i: (0, j))],
        out_specs=pl.BlockSpec((tm, tn), lambda j, i: (i, j)),
        out_shape=jax.ShapeDtypeStruct((t, n), out_dtype),
        compiler_params=_cparams(2),
        name=name,
    )(*a_list, w)


def _mm_slab_kernel(a_ref, w_ref, o_ref):
    res = _dot(a_ref[...], w_ref[...])
    for c in range(o_ref.shape[0]):
        o_ref[c] = res[:, c * LANES:(c + 1) * LANES].astype(o_ref.dtype)


def _mm_slab(a, w, tm, tn, out_dtype, name):
    t, k = a.shape
    n = w.shape[1]
    return pl.pallas_call(
        _mm_slab_kernel,
        grid=(n // tn, t // tm),
        in_specs=[pl.BlockSpec((tm, k), lambda j, i: (i, 0)),
                  pl.BlockSpec((k, tn), lambda j, i: (0, j))],
        out_specs=pl.BlockSpec((tn // LANES, tm, LANES), lambda j, i: (j, i, 0)),
        out_shape=jax.ShapeDtypeStruct((n // LANES, t, LANES), out_dtype),
        compiler_params=_cparams(2),
        name=name,
    )(a, w)


def _rope128(x, c, s1, s2, half):
    return x * c + pltpu.roll(x, LANES - half, 1) * s1 + pltpu.roll(x, half, 1) * s2


def _rope_tables(pos, rot_dim, period):
    half = rot_dim // 2
    inv = ROPE_BASE ** (-jnp.arange(half, dtype=F32) / half)
    ang = pos.astype(F32)[:, None] * inv[None]
    cos, sin = jnp.cos(ang), jnp.sin(ang)
    n = pos.shape[0]
    lane = np.arange(LANES) % period
    li = lane % half
    cos_l, sin_l = cos[:, li], sin[:, li]
    first = jnp.asarray(lane < half)[None]
    second = jnp.asarray((lane >= half) & (lane < rot_dim))[None]
    c = jnp.where(first | second, cos_l, 1.0)
    s1 = jnp.where(first, -sin_l, 0.0)
    s2 = jnp.where(second, sin_l, 0.0)
    return jnp.stack([c, s1, s2]).astype(F32).reshape(3, n, LANES)


def _rms(x, g, eps=1e-6):
    return x * lax.rsqrt(jnp.mean(x * x, axis=-1, keepdims=True) + eps) * g


def _key_extents(nq, tq, p_len, nvar):
    step = -(-nq // nvar)
    out = []
    for lo in range(0, nq, step):
        hi = min(lo + step, nq)
        out.append((lo, hi, min(p_len, -(-(hi * tq) // BLK) * BLK)))
    return out


def _for_each_extent(qi, nq, tq, p_len, body, nvar=N_EXTENT_VARIANTS):
    for lo, hi, ext in _key_extents(nq, tq, p_len, nvar):
        @pl.when(jnp.logical_and(qi >= lo, qi < hi))
        def _():
            body(ext)


def _attn_a_kernel(iq_ref, wq_ref, ikw_ref, tq_ref, tk_ref, aq_ref, ak_ref, av_ref, bias_ref, g_ref,
                   o_ref, key_ref, mask_ref, qs_ref, os_ref, lo_ref, cand_ref, cnt_ref, *, tq, p_len, k_top):
    qi = pl.program_id(1)
    q0 = qi * tq
    half = IDX_ROPE // 2
    for h in range(A_HEADS):
        qs_ref[h * tq:(h + 1) * tq, :] = aq_ref[:, h * A_HDIM:(h + 1) * A_HDIM]

    def body(ext):
        ik = _rope128(ikw_ref[:ext, :], tk_ref[0, :ext, :], tk_ref[1, :ext, :], tk_ref[2, :ext, :],
                      half)[:, :IDX_HDIM].astype(BF16)
        w = wq_ref[...][:, IDX_HDIM:IDX_HDIM + IDX_HEADS] * ((IDX_HEADS * IDX_HDIM) ** -0.5)
        tc, ts1, ts2 = tq_ref[0], tq_ref[1], tq_ref[2]
        score = jnp.zeros((tq, ext), F32)
        for c in range(IDX_HEADS * IDX_HDIM // LANES):
            chunk = _rope128(iq_ref[:, c * LANES:(c + 1) * LANES], tc, ts1, ts2, half).astype(BF16)
            for j in range(LANES // IDX_HDIM):
                h = c * (LANES // IDX_HDIM) + j
                rel = jnp.maximum(_dot_nt(chunk[:, j * IDX_HDIM:(j + 1) * IDX_HDIM], ik), 0.0)
                score = score + w[:, h:h + 1] * rel
        qpos = q0 + lax.broadcasted_iota(jnp.int32, (tq, ext), 0)
        kpos = lax.broadcasted_iota(jnp.int32, (tq, ext), 1)
        vis = jnp.where(kpos <= qpos, jnp.where(kpos >= PAD_FRONT, 1.0, 0.0), 0.0)
        score = jnp.where(vis > 0.0, score, NEG)
        bits = lax.bitcast_convert_type(score, jnp.int32)
        key_ref[:, :ext] = jnp.where(bits < 0, bits ^ jnp.int32(0x7FFFFFFF), bits)
        kf = jnp.float32(k_top)

        def count(pred):
            return jnp.sum(jnp.where(pred, 1.0, 0.0), axis=1, keepdims=True)

        hs = tq // 2

        def lane_counts(r0, cand):
            terms = [jnp.where(key_ref[r0:r0 + hs, c * LANES:(c + 1) * LANES] >= cand, 1.0, 0.0)
                     for c in range(ext // LANES)]
            while len(terms) > 1:
                terms = [terms[i] + terms[i + 1] if i + 1 < len(terms) else terms[i] for i in range(0, len(terms), 2)]
            return terms[0]

        lo_ref[...] = jnp.full((tq, LANES), -2 ** 31, jnp.int32)
        cand_ref[...] = jnp.full((hs, LANES), -2 ** 31, jnp.int32)
        cnt_ref[...] = jnp.zeros((hs, LANES), F32)

        def finish_second_half():
            tot = jnp.sum(cnt_ref[...], axis=1, keepdims=True)
            return jnp.where(tot >= kf, cand_ref[...], lo_ref[hs:, :])

        def vbody(i, carry):
            inc = jnp.left_shift(jnp.int32(1), 31 - i)
            lo_b = finish_second_half()
            lo_a = lo_ref[:hs, :]
            cand_a = lo_a + inc
            tot_a = jnp.sum(lane_counts(0, cand_a), axis=1, keepdims=True)
            cand_b = lo_b + inc
            cnt_ref[...] = lane_counts(hs, cand_b)
            cand_ref[...] = cand_b
            lo_ref[hs:, :] = lo_b
            lo_ref[:hs, :] = jnp.where(tot_a >= kf, cand_a, lo_a)
            return carry

        lax.fori_loop(0, 32, vbody, 0)
        lo_ref[hs:, :] = finish_second_half()
        thr = lo_ref[:, 0:1]
        key = key_ref[:, :ext]
        n_gt = count(key > thr)
        n_eq = count(key == thr)
        need = kf - n_gt
        n_eq_vis = jnp.sum(jnp.where(key == thr, vis, 0.0), axis=1, keepdims=True)
        mask_ref[:, :ext] = jnp.where(key >= thr, jnp.where(vis > 0.0, 0.0, NEG), NEG)
        surplus = jnp.where(n_eq_vis > 0.0, jnp.where(n_eq > need, 1.0, 0.0), 0.0)

        @pl.when(jnp.max(surplus) > 0.0)
        def _():
            nbits = max(1, (ext - 1).bit_length())
            keyv = key_ref[:, :ext]
            col = lax.broadcasted_iota(jnp.int32, (tq, ext), 1)

            def tbody(i, j):
                cand = j + jnp.left_shift(jnp.int32(1), nbits - 1 - i)
                f = jnp.sum(jnp.where(keyv == thr, jnp.where(col < cand, 1.0, 0.0), 0.0), axis=1, keepdims=True)
                return jnp.where(f < need, cand, j)

            jstar = lax.fori_loop(0, nbits, tbody, jnp.zeros((tq, 1), jnp.int32))
            chosen = jnp.where(keyv > thr, 1.0, jnp.where(keyv == thr, jnp.where(col <= jstar, 1.0, 0.0), 0.0))
            mask_ref[:, :ext] = jnp.where(chosen > 0.0, jnp.where(vis > 0.0, 0.0, NEG), NEG)

        scale2 = A_HDIM ** -0.5 * LOG2E

        def hbody(h, carry):
            r0 = pl.multiple_of(h * tq, tq)
            s = _dot_nt(qs_ref[pl.ds(r0, tq), :], ak_ref[:ext, :]) * scale2 + bias_ref[h, :, :ext] + mask_ref[:, :ext]
            m = jnp.max(s, axis=1, keepdims=True)
            e = jnp.exp2(s - m)
            l = jnp.sum(e, axis=1, keepdims=True)
            os_ref[pl.ds(r0, tq), :] = _dot(e.astype(BF16), av_ref[:ext, :]) / l
            return carry

        lax.fori_loop(0, A_HEADS, hbody, 0, unroll=2)

    _for_each_extent(qi, p_len // tq, tq, p_len, body)
    o = jnp.concatenate([os_ref[h * tq:(h + 1) * tq, :] for h in range(A_HEADS)], axis=1)
    o_ref[...] = _rms(o, g_ref[...]).astype(BF16)


def _attn_a(pf, pb, tabs_idx, bias, g, batch, p_len, tq, k_top, col):
    nq = p_len // tq
    t = batch * p_len
    kern = functools.partial(_attn_a_kernel, tq=tq, p_len=p_len, k_top=k_top)
    return pl.pallas_call(
        kern,
        grid=(batch, nq),
        in_specs=[
            pl.BlockSpec((tq, 1024), lambda b, q: (b * nq + q, col["i_q"] // 1024)),
            pl.BlockSpec((tq, LANES), lambda b, q: (b * nq + q, col["ikw"] // LANES)),
            pl.BlockSpec((p_len, LANES), lambda b, q: (b, col["ikw"] // LANES)),
            pl.BlockSpec((3, tq, LANES), lambda b, q: (0, q, 0)),
            pl.BlockSpec((3, p_len, LANES), lambda b, q: (0, 0, 0)),
            pl.BlockSpec((tq, 1024), lambda b, q: (b * nq + q, col["a_q"] // 1024)),
            pl.BlockSpec((p_len, LANES), lambda b, q: (b, col["a_k"] // LANES)),
            pl.BlockSpec((p_len, LANES), lambda b, q: (b, col["a_v"] // LANES)),
            pl.BlockSpec((A_HEADS, 1, p_len), lambda b, q: (0, 0, 0)),
            pl.BlockSpec((1, GROUP_WIDTH), lambda b, q: (0, 0)),
        ],
        out_specs=pl.BlockSpec((tq, GROUP_WIDTH), lambda b, q: (b * nq + q, 0)),
        out_shape=jax.ShapeDtypeStruct((t, GROUP_WIDTH), BF16),
        scratch_shapes=[pltpu.VMEM((tq, p_len), jnp.int32),
                        pltpu.VMEM((tq, p_len), F32),
                        pltpu.VMEM((A_HEADS * tq, A_HDIM), BF16),
                        pltpu.VMEM((A_HEADS * tq, A_HDIM), F32),
                        pltpu.VMEM((tq, LANES), jnp.int32),
                        pltpu.VMEM((tq // 2, LANES), jnp.int32),
                        pltpu.VMEM((tq // 2, LANES), F32)],
        compiler_params=_cparams(2),
        name="attn_a",
    )(pf, pf, pf, tabs_idx, tabs_idx, pb, pb, pb, bias, g.reshape(1, GROUP_WIDTH))


def _attn_b_kernel(q_ref, kp_ref, kc_ref, vp_ref, vc_ref, bias_ref, sink_ref, g_ref, o_ref):
    n = pl.program_id(1)
    grp = B_HEADS // B_KV_HEADS
    slabs = grp // 2
    rows = slabs * BLK
    first_col = PAD_FRONT - (n - 1) * BLK
    col = lax.broadcasted_iota(jnp.int32, (rows, 2 * BLK), 1)
    pad_mask = jnp.where(col >= first_col, 0.0, NEG)
    lane = lax.broadcasted_iota(jnp.int32, (2 * BLK, LANES), 1)
    low = lane < B_HDIM
    zero = jnp.zeros((2 * BLK, LANES), BF16)
    out_slabs = []
    swap = lambda x: jnp.concatenate([x[:, B_HDIM:], x[:, :B_HDIM]], axis=1)
    kplain = jnp.concatenate([kp_ref[...], kc_ref[...]], axis=0)
    vplain = jnp.concatenate([vp_ref[...], vc_ref[...]], axis=0)
    kswap, vswap = swap(kplain), swap(vplain)
    for gi in range(B_KV_HEADS):
        q = jnp.concatenate([q_ref[:, (gi * slabs + sl) * LANES:(gi * slabs + sl + 1) * LANES] for sl in range(slabs)], axis=0)
        lo_src_k, hi_src_k = (kplain, kswap) if gi == 0 else (kswap, kplain)
        lo_src_v, hi_src_v = (vplain, vswap) if gi == 0 else (vswap, vplain)
        acc = None
        for par in range(2):
            keep = low if par == 0 else jnp.logical_not(low)
            k = jnp.where(keep, lo_src_k if par == 0 else hi_src_k, zero)
            v = jnp.where(keep, lo_src_v if par == 0 else hi_src_v, zero)
            s = _dot_nt(q, k) + bias_ref[gi, par] + pad_mask
            sink = sink_ref[gi, par]
            m = jnp.maximum(jnp.max(s, axis=1, keepdims=True), sink)
            e = jnp.exp(s - m)
            l = jnp.sum(e, axis=1, keepdims=True) + jnp.exp(sink - m)
            part = _dot(e.astype(BF16), v) * (1.0 / l)
            acc = part if acc is None else acc + part
        out_slabs.extend(acc[sl * BLK:(sl + 1) * BLK, :] for sl in range(slabs))
    o = jnp.concatenate(out_slabs, axis=1)
    o_ref[...] = _rms(o, g_ref[...]).astype(BF16)


def _attn_b(pb, bias, sink_rows, g, batch, p_len, col):
    nb = p_len // BLK
    t = batch * p_len
    grp = B_HEADS // B_KV_HEADS
    rows = grp // 2 * BLK
    prev = lambda b, n: b * nb + jnp.maximum(n - 1, 0)
    cur = lambda b, n: b * nb + n
    kv_specs = []
    for name in ("b_k", "b_v"):
        c = col[name] // LANES
        kv_specs.append(pl.BlockSpec((BLK, LANES), lambda b, n, c=c: (prev(b, n), c)))
        kv_specs.append(pl.BlockSpec((BLK, LANES), lambda b, n, c=c: (cur(b, n), c)))
    return pl.pallas_call(
        _attn_b_kernel,
        grid=(batch, nb),
        in_specs=[pl.BlockSpec((BLK, 1024), lambda b, n: (cur(b, n), col["b_q"] // 1024))] + kv_specs + [
            pl.BlockSpec((B_KV_HEADS, 2, rows, 2 * BLK), lambda b, n: (0, 0, 0, 0)),
            pl.BlockSpec((B_KV_HEADS, 2, rows, 1), lambda b, n: (0, 0, 0, 0)),
            pl.BlockSpec((1, GROUP_WIDTH), lambda b, n: (0, 0)),
        ],
        out_specs=pl.BlockSpec((BLK, GROUP_WIDTH), lambda b, n: (cur(b, n), 0)),
        out_shape=jax.ShapeDtypeStruct((t, GROUP_WIDTH), BF16),
        compiler_params=_cparams(2),
        name="attn_b",
    )(pb, *([pb] * 4), bias, sink_rows, g.reshape(1, GROUP_WIDTH))


def _c_prep_kernel(cq_ref, ckv_ref, kr_ref, tab_ref, gq_ref, gkv_ref, wq_ref, wkv_ref, q_ref, k_ref, v_ref):
    half = C_ROPE // 2
    tc, ts1, ts2 = tab_ref[0], tab_ref[1], tab_ref[2]
    slot = C_NOPE + LANES
    xq = _rms(cq_ref[...], gq_ref[...]).astype(BF16)
    q = _dot(xq, wq_ref[...])
    xkv = _rms(ckv_ref[...], gkv_ref[...]).astype(BF16)
    kv = _dot(xkv, wkv_ref[...])
    kr = _rope128(kr_ref[...], tc, ts1, ts2, half).astype(BF16)
    for h in range(C_HEADS):
        lo = h * slot
        q_ref[h, :, :C_NOPE] = q[:, lo:lo + C_NOPE].astype(BF16)
        q_ref[h, :, C_NOPE:] = _rope128(q[:, lo + C_NOPE:lo + slot], tc, ts1, ts2, half).astype(BF16)
        k_ref[h, :, :C_NOPE] = kv[:, h * C_NOPE:(h + 1) * C_NOPE].astype(BF16)
        k_ref[h, :, C_NOPE:] = kr
        v_ref[h] = kv[:, (C_HEADS + h) * C_NOPE:(C_HEADS + h + 1) * C_NOPE].astype(BF16)


def _c_prep(pf, tabs, g_cq, g_ckv, w_uq_p, w_ukv_p, p_len, tm, col):
    t = pf.shape[0]
    tiles = p_len // tm
    slot = C_NOPE + LANES
    return pl.pallas_call(
        _c_prep_kernel,
        grid=(t // tm,),
        in_specs=[
            pl.BlockSpec((tm, C_Q_RANK), lambda i: (i, col["c_cq"] // C_Q_RANK)),
            pl.BlockSpec((tm, C_KV_RANK), lambda i: (i, col["c_ckv"] // C_KV_RANK)),
            pl.BlockSpec((tm, LANES), lambda i: (i, col["c_kr"] // LANES)),
            pl.BlockSpec((3, tm, LANES), lambda i: (0, i % tiles, 0)),
            pl.BlockSpec((1, C_Q_RANK), lambda i: (0, 0)),
            pl.BlockSpec((1, C_KV_RANK), lambda i: (0, 0)),
            pl.BlockSpec((C_Q_RANK, C_HEADS * slot), lambda i: (0, 0)),
            pl.BlockSpec((C_KV_RANK, C_HEADS * (C_NOPE + C_VDIM)), lambda i: (0, 0)),
        ],
        out_specs=[pl.BlockSpec((C_HEADS, tm, slot), lambda i: (0, i, 0)),
                   pl.BlockSpec((C_HEADS, tm, slot), lambda i: (0, i, 0)),
                   pl.BlockSpec((C_HEADS, tm, C_VDIM), lambda i: (0, i, 0))],
        out_shape=[jax.ShapeDtypeStruct((C_HEADS, t, slot), BF16),
                   jax.ShapeDtypeStruct((C_HEADS, t, slot), BF16),
                   jax.ShapeDtypeStruct((C_HEADS, t, C_VDIM), BF16)],
        compiler_params=_cparams(1),
        name="c_prep",
    )(pf, pf, pf, tabs, g_cq.reshape(1, -1), g_ckv.reshape(1, -1), w_uq_p, w_ukv_p)


def _causal_mask(q0, tq, ext):
    qpos = q0 + lax.broadcasted_iota(jnp.int32, (tq, ext), 0)
    kpos = lax.broadcasted_iota(jnp.int32, (tq, ext), 1)
    return jnp.where(kpos <= qpos, jnp.where(kpos >= PAD_FRONT, 0.0, NEG), NEG)


def _attn_c_kernel(q_ref, k_ref, v_ref, g_ref, o_ref, os_ref, *, tq, p_len):
    qi = pl.program_id(1)
    scale2 = (C_NOPE + C_ROPE) ** -0.5 * LOG2E

    def body(ext):
        mask = _causal_mask(qi * tq, tq, ext)

        def hbody(h, carry):
            s = _dot_nt(q_ref[h], k_ref[h, :ext, :]) * scale2 + mask
            m = jnp.max(s, axis=1, keepdims=True)
            e = jnp.exp2(s - m)
            l = jnp.sum(e, axis=1, keepdims=True)
            os_ref[h] = _dot(e.astype(BF16), v_ref[h, :ext, :]) / l
            return carry

        lax.fori_loop(0, C_HEADS, hbody, 0, unroll=4)

    nq = p_len // tq
    _for_each_extent(qi, nq, tq, p_len, body, nq)
    o = jnp.concatenate([os_ref[h] for h in range(C_HEADS)], axis=1)
    o_ref[...] = _rms(o, g_ref[...]).astype(BF16)


def _attn_c(qc, kc, vc, g, batch, p_len, tq):
    nq = p_len // tq
    t = batch * p_len
    slot = C_NOPE + LANES
    kern = functools.partial(_attn_c_kernel, tq=tq, p_len=p_len)
    return pl.pallas_call(
        kern,
        grid=(batch, nq),
        in_specs=[pl.BlockSpec((C_HEADS, tq, slot), lambda b, q: (0, b * nq + q, 0)),
                  pl.BlockSpec((C_HEADS, p_len, slot), lambda b, q: (0, b, 0)),
                  pl.BlockSpec((C_HEADS, p_len, C_VDIM), lambda b, q: (0, b, 0)),
                  pl.BlockSpec((1, GROUP_WIDTH), lambda b, q: (0, 0))],
        out_specs=pl.BlockSpec((tq, GROUP_WIDTH), lambda b, q: (b * nq + q, 0)),
        out_shape=jax.ShapeDtypeStruct((t, GROUP_WIDTH), BF16),
        scratch_shapes=[pltpu.VMEM((C_HEADS, tq, C_VDIM), F32)],
        compiler_params=_cparams(2),
        name="attn_c",
    )(qc, kc, vc, g.reshape(1, GROUP_WIDTH))


def _attn_d_kernel(q_ref, k_ref, v_ref, bias_ref, lam_ref, g_ref, o_ref, os_ref, *, tq, p_len, lam_init):
    qi = pl.program_id(1)
    lp = lam_ref[...]
    lam = (jnp.exp(jnp.sum(lp[0:1] * lp[1:2], axis=1, keepdims=True))
           - jnp.exp(jnp.sum(lp[2:3] * lp[3:4], axis=1, keepdims=True)) + lam_init)

    def body(ext):
        mask = _causal_mask(qi * tq, tq, ext)

        def hbody(h, carry):
            bm = bias_ref[h, :, :ext] + mask
            qh = q_ref[h]
            kh = k_ref[h, :ext, :]
            es, ls = [], []
            for c in range(2):
                lo = c * D_HDIM
                s = _dot_nt(qh[:, lo:lo + D_HDIM], kh[:, lo:lo + D_HDIM]) + bm
                m = jnp.max(s, axis=1, keepdims=True)
                e = jnp.exp(s - m)
                es.append(e)
                ls.append(jnp.sum(e, axis=1, keepdims=True))
            a = es[0] * (1.0 / ls[0]) - es[1] * (lam / ls[1])
            os_ref[h] = _dot(a.astype(BF16), v_ref[h, :ext, :])
            return carry

        lax.fori_loop(0, D_HEADS, hbody, 0, unroll=2)

    nq = p_len // tq
    _for_each_extent(qi, nq, tq, p_len, body, nq)
    g = g_ref[...]
    outs = [_rms(os_ref[h], g) * (1.0 - lam_init) for h in range(D_HEADS)]
    o_ref[...] = jnp.concatenate(outs, axis=1).astype(BF16)


def _attn_d(pd, bias, lam_p, g, batch, p_len, tq, lam_init):
    nq = p_len // tq
    t = batch * p_len
    kern = functools.partial(_attn_d_kernel, tq=tq, p_len=p_len, lam_init=lam_init)
    return pl.pallas_call(
        kern,
        grid=(batch, nq),
        in_specs=[pl.BlockSpec((D_HEADS, tq, LANES), lambda b, q: (0, b * nq + q, 0)),
                  pl.BlockSpec((D_HEADS, p_len, LANES), lambda b, q: (1, b, 0)),
                  pl.BlockSpec((D_HEADS, p_len, LANES), lambda b, q: (2, b, 0)),
                  pl.BlockSpec((D_HEADS, 1, p_len), lambda b, q: (0, 0, 0)),
                  pl.BlockSpec((4, D_HDIM), lambda b, q: (0, 0)),
                  pl.BlockSpec((1, 2 * D_HDIM), lambda b, q: (0, 0))],
        out_specs=pl.BlockSpec((tq, GROUP_WIDTH), lambda b, q: (b * nq + q, 0)),
        out_shape=jax.ShapeDtypeStruct((t, GROUP_WIDTH), BF16),
        scratch_shapes=[pltpu.VMEM((D_HEADS, tq, 2 * D_HDIM), F32)],
        compiler_params=_cparams(2),
        name="attn_d",
    )(pd, pd, pd, bias, lam_p, g.reshape(1, 2 * D_HDIM))


def _ffn_up_kernel(x_ref, wg_ref, wu_ref, cwg_ref, cwu_ref, cbg_ref, cbu_ref, o_ref,
                   wgb_ref, wub_ref, ha_ref, hb_ref, carry_ref, *, tm, n_row_tiles, tiles_per_seq):
    s = pl.program_id(0)
    d = x_ref.shape[1]
    kc = min(d, MXU_DEPTH)
    nk = d // kc
    rc = next(r for r in (64, 32, 16, 8) if tm % r == 0)
    nr = tm // rc

    @pl.when(s % n_row_tiles == 0)
    def _():
        wgb_ref[...] = wg_ref[...].astype(BF16)
        wub_ref[...] = wu_ref[...].astype(BF16)

    @pl.when(s == 0)
    def _():
        hb_ref[...] = jnp.zeros_like(hb_ref)

    @pl.when((s + tiles_per_seq - 1) % tiles_per_seq == 0)
    def _():
        carry_ref[...] = jnp.zeros_like(carry_ref)

    def run(rd_ref, wr_ref):
        cws = (cwg_ref[...], cwu_ref[...])
        cbs = (cbg_ref[...], cbu_ref[...])
        for step in range(nr):
            for kk in range((step * nk) // nr, ((step + 1) * nk) // nr):
                for b, wb_ref in enumerate((wgb_ref, wub_ref)):
                    part = _dot(x_ref[:, kk * kc:(kk + 1) * kc], wb_ref[kk * kc:(kk + 1) * kc, :])
                    if kk == 0:
                        wr_ref[b] = part
                    else:
                        wr_ref[b] = wr_ref[b] + part
            r0 = step * rc
            ys = []
            for b in range(2):
                top = carry_ref[b] if step == 0 else rd_ref[b, r0 - 8:r0, :]
                cur = rd_ref[b, r0:r0 + rc, :]
                full = jnp.concatenate([top, cur], axis=0)
                cw = cws[b]
                y = cw[2:3] * cur + cw[1:2] * pltpu.roll(full, 1, 0)[8:] + cw[0:1] * pltpu.roll(full, 2, 0)[8:]
                ys.append(y + cbs[b])
            o_ref[r0:r0 + rc, :] = (ys[0] * jax.nn.sigmoid(ys[0]) * ys[1]).astype(BF16)
        for b in range(2):
            carry_ref[b] = rd_ref[b, tm - 8:, :]

    @pl.when(s % 2 == 0)
    def _():
        run(hb_ref, ha_ref)

    @pl.when(s % 2 == 1)
    def _():
        run(ha_ref, hb_ref)


def _ffn_up(hb, w_up, conv_w, conv_b, layer, p_len, tm, tn):
    t, d = hb.shape
    f = w_up.shape[2] // 2
    assert f % tn == 0 and tm % 8 == 0 and p_len % tm == 0
    nj, nt = f // tn, t // tm
    n_tiles = nj * nt
    kern = functools.partial(_ffn_up_kernel, tm=tm, n_row_tiles=nt, tiles_per_seq=p_len // tm)
    cb = conv_b.reshape(conv_b.shape[0], 1, 2 * f)
    cur = lambda s: jnp.minimum(s, n_tiles - 1)
    prv = lambda s: jnp.maximum(s - 1, 0)
    return pl.pallas_call(
        kern,
        grid=(n_tiles + 1,),
        in_specs=[pl.BlockSpec((tm, d), lambda s: (cur(s) % nt, 0)),
                  pl.BlockSpec((None, d, tn), lambda s: (layer, 0, cur(s) // nt)),
                  pl.BlockSpec((None, d, tn), lambda s: (layer, 0, nj + cur(s) // nt)),
                  pl.BlockSpec((None, CONV_W, tn), lambda s: (layer, 0, prv(s) // nt)),
                  pl.BlockSpec((None, CONV_W, tn), lambda s: (layer, 0, nj + prv(s) // nt)),
                  pl.BlockSpec((None, 1, tn), lambda s: (layer, 0, prv(s) // nt)),
                  pl.BlockSpec((None, 1, tn), lambda s: (layer, 0, nj + prv(s) // nt))],
        out_specs=pl.BlockSpec((tm, tn), lambda s: (prv(s) % nt, prv(s) // nt)),
        out_shape=jax.ShapeDtypeStruct((t, f), BF16),
        scratch_shapes=[pltpu.VMEM((d, tn), BF16), pltpu.VMEM((d, tn), BF16),
                        pltpu.VMEM((2, tm, tn), F32), pltpu.VMEM((2, tm, tn), F32), pltpu.VMEM((2, 8, tn), F32)],
        compiler_params=_cparams(1),
        name="ffn_up",
    )(hb, w_up, w_up, conv_w, conv_w, cb, cb)


def _offsets(names, sizes):
    out, o = {}, 0
    for n, s in zip(names, sizes):
        out[n] = o
        o += s
    return out, o


_BF_NAMES = ("a_q", "b_q", "a_k", "a_v", "b_k", "b_v")
_BF_SIZES = (1024, 1024, 128, 128, 128, 128)
_D_NAMES = ("d_q", "d_k", "d_v")
_F32_NAMES = ("i_q", "c_cq", "c_ckv", "ikw", "c_kr")
_F32_SIZES = (1024, 1024, 512, 128, 128)
COL_BF, N_BF = _offsets(_BF_NAMES, _BF_SIZES)
COL_F32, N_F32 = _offsets(_F32_NAMES, _F32_SIZES)


def _prep_w_in(w):
    d = w.shape[0]
    seg = dict(zip(("a_q", "a_k", "a_v", "i_q", "i_k", "i_w", "b_q", "b_k", "b_v", "c_cq", "c_ckv", "c_kr",
                    "d_q", "d_k", "d_v"), jnp.split(w, np.cumsum(IN_SIZES)[:-1].tolist(), axis=1)))
    seg["b_q"] = seg["b_q"] * (B_HDIM ** -0.5)
    seg["d_q"] = seg["d_q"] * (D_HDIM ** -0.5)
    wb = jnp.concatenate([seg[n] for n in _BF_NAMES], axis=1).astype(BF16)
    wd = jnp.concatenate([seg[n] for n in _D_NAMES], axis=1).astype(BF16)
    zeros = lambda n: jnp.zeros((d, n), w.dtype)
    ikw = jnp.concatenate([seg["i_k"], seg["i_w"], zeros(LANES - IDX_HDIM - IDX_HEADS)], axis=1)
    ckr = jnp.concatenate([seg["c_kr"], zeros(LANES - C_ROPE)], axis=1)
    wf = jnp.concatenate([seg["i_q"], seg["c_cq"], seg["c_ckv"], ikw, ckr], axis=1).astype(BF16)
    return wb, wd, wf


def _prep_w_uq(w):
    r = w.shape[0]
    w3 = w.reshape(r, C_HEADS, C_NOPE + C_ROPE)
    return jnp.pad(w3, ((0, 0), (0, 0), (0, LANES - C_ROPE))).reshape(r, C_HEADS * (C_NOPE + LANES)).astype(BF16)


def _prep_w_ukv(w):
    r = w.shape[0]
    w3 = w.reshape(r, C_HEADS, C_NOPE + C_VDIM)
    return jnp.concatenate([w3[:, :, :C_NOPE].reshape(r, -1), w3[:, :, C_NOPE:].reshape(r, -1)], axis=1).astype(BF16)


def _alibi(n):
    return 2.0 ** (-8.0 * np.arange(1, n + 1, dtype=np.float64) / n)


def _swa_bias():
    grp = B_HEADS // B_KV_HEADS
    r = np.arange(BLK)[:, None]
    c = np.arange(2 * BLK)[None, :]
    diff = (r + BLK - c).astype(np.float64)
    ok = (diff >= 0) & (diff < WINDOW)
    slopes = _alibi(B_HEADS).reshape(B_KV_HEADS, grp // 2, 2).transpose(0, 2, 1)
    bias = np.where(ok[None, None, None], -slopes[..., None, None] * diff[None, None, None], NEG)
    return jnp.asarray(bias.reshape(B_KV_HEADS, 2, grp // 2 * BLK, 2 * BLK), F32)


def _swa_sinks(sinks):
    grp = B_HEADS // B_KV_HEADS
    sk = sinks.astype(F32).reshape(B_KV_HEADS, grp // 2, 2).transpose(0, 2, 1)
    return jnp.repeat(sk, BLK, axis=2)[..., None]


def kernel(x, meta_tokens, ln_in_g, ln_in_b, w_in, g_cq, g_ckv, w_uq, w_ukv, sinks, lam_q1, lam_k1, lam_q2, lam_k2,
           g_diff, g_grp, w_out, ln1_g, ln1_b, w_up, conv_w, conv_b, w_down, ln2_g, ln2_b):
    batch, s_len, d = x.shape
    depth = w_in.shape[0]
    p_len = s_len + BLK
    t = batch * p_len
    k_top = min(TOPK_MAX, s_len // 4)
    alpha = (2 * depth) ** 0.25
    tm = _row_tile(p_len, 544)
    tm_ln = _row_tile(p_len, 272)
    tm_ffn = _row_tile(p_len, 1088)
    tq = BLK
    grp = B_HEADS // B_KV_HEADS

    h, hb = _ln_in(x, meta_tokens, ln_in_g, ln_in_b)

    pos = jnp.arange(p_len, dtype=jnp.int32) - PAD_FRONT
    tabs_idx = _rope_tables(pos, IDX_ROPE, IDX_HDIM)
    tabs_c = _rope_tables(pos, C_ROPE, LANES)
    kidx = np.arange(p_len, dtype=np.float64)
    bias_a = jnp.asarray(_alibi(A_HEADS)[:, None, None] * kidx[None, None, :] * LOG2E, F32)
    bias_d = jnp.asarray(_alibi(D_HEADS)[:, None, None] * kidx[None, None, :], F32)
    bias_b = _swa_bias()

    for l in range(depth):
        wb, wd, wf = _prep_w_in(w_in[l])
        pb = _mm(hb, wb, tm_ffn, 2 * MXU_DEPTH, BF16, "proj_bf16")
        pd = _mm_slab(hb, wd, tm_ffn, 2 * MXU_DEPTH, BF16, "proj_d")
        pf = _mm(hb, wf, tm, N_F32 // 2, F32, "proj_f32")

        o_a = _attn_a(pf, pb, tabs_idx, bias_a, g_grp[l, 0], batch, p_len, tq, k_top, {**COL_F32, **COL_BF})
        o_b = _attn_b(pb, bias_b, _swa_sinks(sinks[l]), g_grp[l, 1], batch, p_len, COL_BF)
        qc, kc, vc = _c_prep(pf, tabs_c, g_cq[l], g_ckv[l], _prep_w_uq(w_uq[l]), _prep_w_ukv(w_ukv[l]),
                               p_len, tm_ln, COL_F32)
        o_c = _attn_c(qc, kc, vc, g_grp[l, 2], batch, p_len, tm_ln)
        lam_init = 0.8 - 0.6 * math.exp(-0.3 * l)
        lam_p = jnp.stack([lam_q1[l], lam_k1[l], lam_q2[l], lam_k2[l]]).astype(F32)
        o_d = _attn_d(pd, bias_d, lam_p, g_diff[l], batch, p_len, tm_ln, lam_init)

        mix = _mm_groups([o_a, o_b, o_c, o_d], _cast_layer(w_out, l, BF16), tm, min(1024, d), F32, "out_proj")
        h, hb_ffn = _res_ln(h, mix, ln1_g[l], ln1_b[l], alpha, tm_ln, p_len, True)

        act = _ffn_up(hb_ffn, w_up, conv_w, conv_b, l, p_len, tm_ffn, FF_TILE)
        ffn = _mm(act, _cast_layer(w_down, l, BF16), tm, min(512, d), F32, "ffn_down")
        if l + 1 < depth:
            h, hb = _res_ln(h, ffn, ln2_g[l], ln2_b[l], alpha, tm_ln, p_len, False)
        else:
            out = _res_ln_final(h, ffn, ln2_g[l], ln2_b[l], alpha, batch, p_len)

    return out.reshape(batch, s_len, d)
```
